```python
import math
import jax, jax.numpy as jnp
from jax import lax
import numpy as np

D_MODEL = 2048
BATCH = 32
SEQ = 256
DEPTH = 2
DEC_BATCH = 2
DEC_SEQ = 2048
PAST_LEN = 256

GRID_W = 64
POS_BASE = 10000.0
N_MIXERS = 2
N_A_LAYERS = (DEPTH + 1) // 2
N_B_LAYERS = DEPTH // 2
N_DIR = 2
EPS = 1e-6
RW_HEAD = 64
RW_HEADS = D_MODEL // RW_HEAD
RW_DECAY_LORA = 96
RW_ICL_LORA = 96
RW_GATE_LORA = 256
RW_GN_EPS = RW_HEAD * 1e-5
GD_K_HEADS = 16
GD_V_HEADS = 32
GD_DK = 128
GD_DV = 128
GD_QK = GD_K_HEADS * GD_DK
GD_V = GD_V_HEADS * GD_DV
GD_CONV = 4
GD_CHUNK = 64
GD_QKV = 2 * GD_QK + GD_V
GD_IN = GD_QKV + GD_V + N_DIR * 2 * GD_V_HEADS
D_FF = ((8 * D_MODEL // 3 + 255) // 256) * 256

kernel_name = "bidir_rwkv7_gdn_prefix_dit_step"


def rms_norm(x, g):
    xf = x.astype(jnp.float32)
    y = xf * lax.rsqrt(jnp.mean(xf * xf, -1, keepdims=True) + EPS)
    return (y * g.astype(jnp.float32)).astype(x.dtype)


def l2norm(x):
    xf = x.astype(jnp.float32)
    return xf * lax.rsqrt(jnp.sum(xf * xf, -1, keepdims=True) + EPS)


def grid_pos_embed(n_tokens):
    rows = n_tokens // GRID_W
    f32 = jnp.float32
    row = jnp.broadcast_to(jnp.arange(rows, dtype=f32)[:, None], (rows, GRID_W)).reshape(-1)
    col = jnp.broadcast_to(jnp.arange(GRID_W, dtype=f32)[None, :], (rows, GRID_W)).reshape(-1)
    quarter = D_MODEL // 4
    omega = 1.0 / (POS_BASE ** (jnp.arange(quarter, dtype=f32) / quarter))
    ar = row[:, None] * omega
    ac = col[:, None] * omega
    return jnp.concatenate([jnp.sin(ar), jnp.cos(ar), jnp.sin(ac), jnp.cos(ac)], -1)


def centred_shift(x):
    prev = jnp.pad(x[:, :-1], ((0, 0), (1, 0), (0, 0)))
    nxt = jnp.pad(x[:, 1:], ((0, 0), (0, 1), (0, 0)))
    return 0.5 * (prev + nxt)


def short_conv(x, w):
    ch = x.shape[-1]
    lo = (GD_CONV - 1) // 2
    hi = GD_CONV - 1 - lo
    return lax.conv_general_dilated(x, w[:, None, :], window_strides=(1,), padding=[(lo, hi)],
                                    dimension_numbers=("NWC", "WIO", "NWC"), feature_group_count=ch)


def rwkv7_scan(r, decay, k, v, a_vec, b_vec, s0, reverse):
    def step(s, inp):
        r_t, w_t, k_t, v_t, a_t, b_t = inp
        sa = jnp.einsum("bhvk,bhk->bhv", s, a_t)
        s = s * w_t[:, :, None, :] + sa[..., None] * b_t[:, :, None, :] + v_t[..., None] * k_t[:, :, None, :]
        return s, jnp.einsum("bhvk,bhk->bhv", s, r_t)
    xs = tuple(jnp.moveaxis(z, 1, 0) for z in (r, decay, k, v, a_vec, b_vec))
    s, ys = lax.scan(step, s0, xs, reverse=reverse)
    return jnp.moveaxis(ys, 0, 1), s


def rwkv7_mix(h, s0, mu, w_r, w_k, w_v, w_o, w0, w1, w2, a0, a1, a2, g1, g2, k_k, k_a, r_k, ln_w, ln_b):
    B, T, _ = h.shape
    f32 = jnp.float32
    xx = centred_shift(h) - h
    xr, xw, xk, xv, xa, xg = (h + xx * mu[j] for j in range(6))
    r = xr @ w_r
    k = xk @ w_k
    v = xv @ w_v
    gate = jax.nn.sigmoid(xg @ g1) @ g2
    lw = jnp.tanh(jnp.einsum("btc,dcl->dbtl", xw, w1))
    w_log = -jax.nn.softplus(-(w0[:, None, None, :] + jnp.einsum("dbtl,dlc->dbtc", lw, w2))) - 0.5
    decay = jnp.exp(-jnp.exp(w_log.astype(f32)))
    a = jax.nn.sigmoid(a0[:, None, None, :] + jnp.einsum("dbtl,dlc->dbtc", jnp.einsum("btc,dcl->dbtl", xa, a1), a2))
    heads = lambda z: z.reshape(z.shape[:-1] + (RW_HEADS, RW_HEAD)).astype(f32)
    kk = l2norm(heads(k * k_k))
    k_dir = heads(k * (1.0 + (a - 1.0) * k_a))
    a_h = heads(a)
    dec_h = heads(decay)
    r_h = heads(r)
    v_h = heads(v)
    s0 = s0.astype(f32)
    y_f, s_f = rwkv7_scan(r_h, dec_h[0], k_dir[0], v_h, -kk, kk * a_h[0], s0[:, 0], False)
    y_b, s_b = rwkv7_scan(r_h, dec_h[1], k_dir[1], v_h, -kk, kk * a_h[1], s0[:, 1], True)
    y = y_f + y_b
    mean = jnp.mean(y, -1, keepdims=True)
    var = jnp.mean(jnp.square(y - mean), -1, keepdims=True)
    y = ((y - mean) * lax.rsqrt(var + RW_GN_EPS)).reshape(B, T, D_MODEL) * ln_w.astype(f32) + ln_b.astype(f32)
    bonus = jnp.sum(jnp.sum(r_h[None] * k_dir * r_k.astype(f32), -1, keepdims=True), 0) * v_h
    y = (y + bonus.reshape(B, T, D_MODEL)).astype(h.dtype)
    out = (y * gate) @ w_o
    return out, jnp.stack([s_f, s_b], 1).astype(h.dtype)


def chunk_gated_delta(q, k, v, g, beta, s0):
    B, T, H, DK = q.shape
    DV = v.shape[-1]
    C = GD_CHUNK
    N = T // C
    chunks = lambda z: z.reshape(B, N, C, H, -1).transpose(0, 1, 3, 2, 4)
    q, k, v = chunks(q), chunks(k), chunks(v)
    g = g.reshape(B, N, C, H).transpose(0, 1, 3, 2)
    beta = beta.reshape(B, N, C, H).transpose(0, 1, 3, 2)
    gc = jnp.cumsum(g, -1)
    causal = jnp.tril(jnp.ones((C, C), bool))
    strict = jnp.tril(jnp.ones((C, C), bool), -1)
    decay = jnp.exp(jnp.where(causal, gc[..., :, None] - gc[..., None, :], -jnp.inf))
    k_beta = k * beta[..., None]
    v_beta = v * beta[..., None]
    A = jnp.where(strict, jnp.einsum("bnhid,bnhjd->bnhij", k_beta, k) * decay, 0.0)
    eye = jnp.eye(C, dtype=A.dtype)
    Tm = lax.linalg.triangular_solve(eye + A, jnp.broadcast_to(eye, A.shape), left_side=True,
                                     lower=True, unit_diagonal=True)
    u = jnp.einsum("bnhij,bnhjv->bnhiv", Tm, v_beta)
    w = jnp.einsum("bnhij,bnhjd->bnhid", Tm, k_beta * jnp.exp(gc)[..., None])
    attn = jnp.einsum("bnhid,bnhjd->bnhij", q, k) * decay

    def step(S, inp):
        q_i, k_i, u_i, w_i, gc_i, attn_i = inp
        v_new = u_i - jnp.einsum("bhck,bhkv->bhcv", w_i, S)
        o = jnp.einsum("bhck,bhkv->bhcv", q_i * jnp.exp(gc_i)[..., None], S) + jnp.einsum("bhij,bhjv->bhiv", attn_i, v_new)
        g_last = gc_i[..., -1]
        S = S * jnp.exp(g_last)[..., None, None] + jnp.einsum(
            "bhck,bhcv->bhkv", k_i * jnp.exp(g_last[..., None] - gc_i)[..., None], v_new)
        return S, o

    xs = tuple(jnp.moveaxis(z, 1, 0) for z in (q, k, u, w, gc, attn))
    S, o = lax.scan(step, s0, xs)
    return jnp.transpose(o, (1, 0, 3, 2, 4)).reshape(B, T, H, DV), S


def gdn_mix(h, s0, w_in, conv_w, a_log, dt_bias, norm_w, w_out):
    B, T, _ = h.shape
    f32 = jnp.float32
    proj = h @ w_in
    qkv = jax.nn.silu(short_conv(proj[..., :GD_QKV], conv_w))
    z = proj[..., GD_QKV:GD_QKV + GD_V].reshape(B, T, GD_V_HEADS, GD_DV)
    ab = proj[..., GD_QKV + GD_V:].astype(f32).reshape(B, T, N_DIR, 2, GD_V_HEADS)
    g = -jnp.exp(a_log.astype(f32)) * jax.nn.softplus(ab[..., 0, :] + dt_bias.astype(f32))
    beta = jax.nn.sigmoid(ab[..., 1, :])
    rep = GD_V_HEADS // GD_K_HEADS
    q = jnp.repeat(l2norm(qkv[..., :GD_QK].reshape(B, T, GD_K_HEADS, GD_DK)) * (GD_DK ** -0.5), rep, axis=2)
    k = jnp.repeat(l2norm(qkv[..., GD_QK:2 * GD_QK].reshape(B, T, GD_K_HEADS, GD_DK)), rep, axis=2)
    v = qkv[..., 2 * GD_QK:].reshape(B, T, GD_V_HEADS, GD_DV).astype(f32)
    s0 = s0.astype(f32)
    flip = lambda t: jnp.flip(t, 1)
    o_f, s_f = chunk_gated_delta(q, k, v, g[:, :, 0], beta[:, :, 0], s0[:, 0])
    o_b, s_b = chunk_gated_delta(flip(q), flip(k), flip(v), flip(g[:, :, 1]), flip(beta[:, :, 1]), s0[:, 1])
    o = o_f + flip(o_b)
    o = o * lax.rsqrt(jnp.mean(o * o, -1, keepdims=True) + EPS) * norm_w.astype(f32) * jax.nn.silu(z.astype(f32))
    out = o.astype(h.dtype).reshape(B, T, GD_V) @ w_out
    return out, jnp.stack([s_f, s_b], 1).astype(h.dtype)


def swiglu(h, w1, w3, w2):
    return (jax.nn.silu(h @ w1) * (h @ w3)) @ w2


def run_trunk(x, cvec, st_a, st_b, shared, rw, gd):
    norm_mix, norm_ffn, norm_final, w_mod, b_mod, ffn_w1, ffn_w3, ffn_w2 = shared
    fin_a, fin_b = [], []
    for i in range(DEPTH):
        mod = (jax.nn.silu(cvec) @ w_mod[i] + b_mod[i]).reshape(cvec.shape[0], 1, 6, D_MODEL)
        sh1, sc1, gt1, sh2, sc2, gt2 = (mod[:, :, j] for j in range(6))
        h = rms_norm(x, norm_mix[i]) * (1.0 + sc1) + sh1
        j = i // N_MIXERS
        if i % N_MIXERS == 0:
            out, s = rwkv7_mix(h, st_a[:, j], *(p[j] for p in rw))
            fin_a.append(s)
        else:
            out, s = gdn_mix(h, st_b[:, j], *(p[j] for p in gd))
            fin_b.append(s)
        x = x + gt1 * out
        h = rms_norm(x, norm_ffn[i]) * (1.0 + sc2) + sh2
        x = x + gt2 * swiglu(h, ffn_w1[i], ffn_w3[i], ffn_w2[i])
    return rms_norm(x, norm_final), jnp.stack(fin_a, 1), jnp.stack(fin_b, 1)


def setup_inputs(seed: int = 0) -> dict:
    key = jax.random.key(seed)
    ks = iter(jax.random.split(key, 64))
    f32 = jnp.float32
    nrm = lambda shape, scale: scale * jax.random.normal(next(ks), shape, f32)
    uni = lambda shape, lo, hi: jax.random.uniform(next(ks), shape, f32, lo, hi)
    D = D_MODEL
    nA, nB = N_A_LAYERS, N_B_LAYERS
    dt = jnp.exp(uni((nB, N_DIR, GD_V_HEADS), math.log(1e-3), math.log(1e-1)))
    return {
        "x_prompt": nrm((BATCH, SEQ, D), 1.0),
        "x_sample": nrm((DEC_BATCH, DEC_SEQ, D), 1.0),
        "state_rwkv": nrm((DEC_BATCH, nA, N_DIR, RW_HEADS, RW_HEAD, RW_HEAD), 0.1),
        "state_gdn": nrm((DEC_BATCH, nB, N_DIR, GD_V_HEADS, GD_DK, GD_DV), 0.1),
        "c": nrm((DEC_BATCH, D), 1.0),
        "c_ctx": nrm((D,), 1.0),
        "norm_mix": 1.0 + nrm((DEPTH, D), 0.02),
        "norm_ffn": 1.0 + nrm((DEPTH, D), 0.02),
        "norm_final": 1.0 + nrm((D,), 0.02),
        "w_mod": nrm((DEPTH, D, 6 * D), 0.5 * D ** -0.5),
        "b_mod": nrm((DEPTH, 6 * D), 0.02),
        "ffn_w1": nrm((DEPTH, D, D_FF), D ** -0.5),
        "ffn_w3": nrm((DEPTH, D, D_FF), D ** -0.5),
        "ffn_w2": nrm((DEPTH, D_FF, D), D_FF ** -0.5),
        "rw_mu": uni((nA, 6, D), 0.0, 1.0),
        "rw_wr": nrm((nA, D, D), D ** -0.5),
        "rw_wk": nrm((nA, D, D), D ** -0.5),
        "rw_wv": nrm((nA, D, D), D ** -0.5),
        "rw_wo": nrm((nA, D, D), D ** -0.5),
        "rw_w0": -3.0 + nrm((nA, N_DIR, D), 1.0),
        "rw_w1": nrm((nA, N_DIR, D, RW_DECAY_LORA), D ** -0.5),
        "rw_w2": nrm((nA, N_DIR, RW_DECAY_LORA, D), 0.3 * RW_DECAY_LORA ** -0.5),
        "rw_a0": nrm((nA, N_DIR, D), 0.5),
        "rw_a1": nrm((nA, N_DIR, D, RW_ICL_LORA), D ** -0.5),
        "rw_a2": nrm((nA, N_DIR, RW_ICL_LORA, D), 0.3 * RW_ICL_LORA ** -0.5),
        "rw_g1": nrm((nA, D, RW_GATE_LORA), D ** -0.5),
        "rw_g2": nrm((nA, RW_GATE_LORA, D), RW_GATE_LORA ** -0.5),
        "rw_kk": 0.85 + nrm((nA, D), 0.05),
        "rw_ka": 1.0 + nrm((nA, D), 0.05),
        "rw_rk": nrm((nA, RW_HEADS, RW_HEAD), 0.1),
        "rw_ln_w": 1.0 + nrm((nA, D), 0.02),
        "rw_ln_b": nrm((nA, D), 0.02),
        "gd_w_in": nrm((nB, D, GD_IN), D ** -0.5),
        "gd_conv": nrm((nB, GD_CONV, GD_QKV), GD_CONV ** -0.5),
        "gd_a_log": jnp.log(uni((nB, N_DIR, GD_V_HEADS), 1.0, 16.0)),
        "gd_dt_bias": dt + jnp.log(-jnp.expm1(-dt)),
        "gd_norm": 1.0 + nrm((nB, GD_DV), 0.02),
        "gd_w_out": nrm((nB, GD_V, D), GD_V ** -0.5),
    }


def reference(x_prompt, x_sample, state_rwkv, state_gdn, c, c_ctx, norm_mix, norm_ffn, norm_final,
              w_mod, b_mod, ffn_w1, ffn_w3, ffn_w2, rw_mu, rw_wr, rw_wk, rw_wv, rw_wo, rw_w0, rw_w1,
              rw_w2, rw_a0, rw_a1, rw_a2, rw_g1, rw_g2, rw_kk, rw_ka, rw_rk, rw_ln_w, rw_ln_b,
              gd_w_in, gd_conv, gd_a_log, gd_dt_bias, gd_norm, gd_w_out):
    shared = (norm_mix, norm_ffn, norm_final, w_mod, b_mod, ffn_w1, ffn_w3, ffn_w2)
    rw = (rw_mu, rw_wr, rw_wk, rw_wv, rw_wo, rw_w0, rw_w1, rw_w2, rw_a0, rw_a1, rw_a2,
          rw_g1, rw_g2, rw_kk, rw_ka, rw_rk, rw_ln_w, rw_ln_b)
    gd = (gd_w_in, gd_conv, gd_a_log, gd_dt_bias, gd_norm, gd_w_out)
    nb = x_prompt.shape[0]
    zero_a = jnp.zeros((nb, N_A_LAYERS, N_DIR, RW_HEADS, RW_HEAD, RW_HEAD), x_prompt.dtype)
    zero_b = jnp.zeros((nb, N_B_LAYERS, N_DIR, GD_V_HEADS, GD_DK, GD_DV), x_prompt.dtype)
    y_prompt, new_state_rwkv, new_state_gdn = run_trunk(x_prompt, c_ctx[None, :], zero_a, zero_b, shared, rw, gd)
    x_lat = x_sample + grid_pos_embed(x_sample.shape[1]).astype(x_sample.dtype)[None]
    y_sample, _, _ = run_trunk(x_lat, c, state_rwkv, state_gdn, shared, rw, gd)
    return (y_prompt, y_sample, new_state_rwkv, new_state_gdn)
```

```python
import functools
import math

import numpy as np
import jax
import jax.numpy as jnp
from jax import lax
from jax.experimental import pallas as pl
from jax.experimental.pallas import tpu as pltpu

F32 = jnp.float32
BF16 = jnp.bfloat16
HIGHEST = lax.Precision.HIGHEST

EPS = 1e-6
GRID_W = 64
POS_BASE = 10000.0
RW_HEAD = 64
RW_GROUP_HEADS = 4
RW_LANES = RW_HEAD * RW_GROUP_HEADS
RW_LORA_PAD = 128
GD_DK = 128
GD_DV = 128
GD_CONV = 4
CHUNK = 64
SUBLANES = 8
VMEM_LIMIT = 56 * 1024 * 1024


def _cparams(*sem):
    return pltpu.CompilerParams(dimension_semantics=sem, vmem_limit_bytes=VMEM_LIMIT)


def _dot(a, b):
    return jnp.dot(a.astype(BF16), b.astype(BF16), preferred_element_type=F32)


def _dot_hi(a, b):
    return jnp.dot(a, b, precision=HIGHEST, preferred_element_type=F32)


def _dot_nt_hi(a, b):
    return lax.dot_general(a, b, (((1,), (1,)), ((), ())), precision=HIGHEST, preferred_element_type=F32)


def _dot_tn_hi(a, b):
    return lax.dot_general(a, b, (((0,), (0,)), ((), ())), precision=HIGHEST, preferred_element_type=F32)


def _bmm_hi(a, b):
    return lax.dot_general(a, b, (((2,), (1,)), ((0,), (0,))), precision=HIGHEST, preferred_element_type=F32)


def _iota(shape, dim):
    return lax.broadcasted_iota(jnp.int32, shape, dim)


def _silu(x):
    return x * jax.nn.sigmoid(x)


def _softplus(x):
    return jnp.maximum(x, 0.0) + jnp.log(1.0 + jnp.exp(-jnp.abs(x)))


def _rms_mod(x, nw, sh, sc):
    y = x * lax.rsqrt(jnp.mean(x * x, -1, keepdims=True) + EPS)
    return (y * nw) * (1.0 + sc) + sh


def _unit_tri_inverse(n_mat):
    c_len = n_mat.shape[-1]
    ij = _iota(n_mat.shape, 1) ^ _iota(n_mat.shape, 2)
    d_mat = jnp.where(ij == 0, 1.0, jnp.where(ij == 1, n_mat, 0.0))
    half = 2
    while half < c_len:
        l_mat = jnp.where((ij >= half) & (ij < 2 * half), n_mat, 0.0)
        d_mat = d_mat + _bmm_hi(_bmm_hi(d_mat, l_mat), d_mat)
        half *= 2
    return d_mat


def _mod_kernel(c_ref, w_ref, b_ref, o_ref):
    o_ref[0] = _dot(_silu(c_ref[...]), w_ref[0]) + b_ref[0]


def _modulation(cond, w_mod, b_mod):
    depth, d, n = w_mod.shape
    tn = 1024
    return pl.pallas_call(
        _mod_kernel,
        grid=(depth, n // tn),
        in_specs=[pl.BlockSpec((SUBLANES, d), lambda l, j: (0, 0)),
                  pl.BlockSpec((1, d, tn), lambda l, j: (l, 0, j)),
                  pl.BlockSpec((1, 1, tn), lambda l, j: (l, 0, j))],
        out_specs=pl.BlockSpec((1, SUBLANES, tn), lambda l, j: (l, 0, j)),
        out_shape=jax.ShapeDtypeStruct((depth, SUBLANES, n), F32),
        compiler_params=_cparams("arbitrary", "arbitrary"),
        name="modulation",
    )(cond, w_mod, b_mod.reshape(depth, 1, n))


def _shift_mix(x_ref, xp_ref, xn_ref, mod_ref, nw_ref, has_prev, has_next):
    nw = nw_ref[...]
    sh = mod_ref[0, 0:1, :]
    sc = mod_ref[0, 1:2, :]
    h = _rms_mod(x_ref[...], nw, sh, sc)
    tm = h.shape[0]
    hp = _rms_mod(xp_ref[SUBLANES - 1:SUBLANES, :], nw, sh, sc) * has_prev
    hn = _rms_mod(xn_ref[0:1, :], nw, sh, sc) * has_next
    rows = _iota(h.shape, 0)
    prev = jnp.where(rows == 0, hp, pltpu.roll(h, 1, 0))
    nxt = jnp.where(rows == tm - 1, hn, pltpu.roll(h, tm - 1, 0))
    return h, 0.5 * (prev + nxt) - h


def _rkv_kernel(grp_ref, hp_ref, hn_ref, x_ref, xp_ref, xn_ref, mod_ref, nw_ref, mu_ref, w_ref, o_ref):
    i = pl.program_id(1)
    h, xx = _shift_mix(x_ref, xp_ref, xn_ref, mod_ref, nw_ref,
                       hp_ref[i].astype(F32), hn_ref[i].astype(F32))
    o_ref[0] = _dot(h + xx * mu_ref[0], w_ref[0])


def _lora_kernel(grp_ref, hp_ref, hn_ref, x_ref, xp_ref, xn_ref, mod_ref, nw_ref, mu_ref,
                 g1_ref, g2_ref, w1_ref, w2_ref, a1_ref, a2_ref, w0_ref, a0_ref,
                 gate_ref, lw_ref, a_ref):
    i = pl.program_id(0)
    h, xx = _shift_mix(x_ref, xp_ref, xn_ref, mod_ref, nw_ref,
                       hp_ref[i].astype(F32), hn_ref[i].astype(F32))
    xw = h + xx * mu_ref[1:2, :]
    xa = h + xx * mu_ref[4:5, :]
    xg = h + xx * mu_ref[5:6, :]
    gate_ref[...] = _dot(jax.nn.sigmoid(_dot(xg, g1_ref[...])), g2_ref[...])
    for d in range(2):
        lw = jnp.tanh(_dot(xw, w1_ref[d]))
        w_log = -_softplus(-(w0_ref[d] + _dot(lw, w2_ref[d]))) - 0.5
        lw_ref[d] = -jnp.exp(w_log)
        a_ref[d] = jax.nn.sigmoid(a0_ref[d] + _dot(_dot(xa, a1_ref[d]), a2_ref[d]))


def _tile_tables(regions, tm):
    row_grp, row_pos, row_len = [], [], []
    for n_seq, t_len, groups in regions:
        for s in range(n_seq):
            row_grp += [groups[s]] * t_len
            row_pos += list(range(t_len))
            row_len += [t_len] * t_len
    row_grp, row_pos, row_len = (np.asarray(v).reshape(-1, tm) for v in (row_grp, row_pos, row_len))
    assert (row_grp == row_grp[:, :1]).all(), "a row tile must not straddle modulation groups"
    as_i32 = lambda v: jnp.asarray(np.asarray(v, np.int32))
    return (as_i32(row_grp[:, 0]), as_i32(row_pos[:, 0] > 0),
            as_i32(row_pos[:, -1] < row_len[:, -1] - 1))


def _halo_specs(tm, d, m_rows, n_lead):
    blocks = tm // SUBLANES
    last = m_rows // SUBLANES - 1
    if n_lead == 1:
        cur = lambda j, i, *_: (i, 0)
        prev = lambda j, i, *_: (jnp.maximum(i * blocks - 1, 0), 0)
        nxt = lambda j, i, *_: (jnp.minimum((i + 1) * blocks, last), 0)
    else:
        cur = lambda i, *_: (i, 0)
        prev = lambda i, *_: (jnp.maximum(i * blocks - 1, 0), 0)
        nxt = lambda i, *_: (jnp.minimum((i + 1) * blocks, last), 0)
    return [pl.BlockSpec((tm, d), cur), pl.BlockSpec((SUBLANES, d), prev), pl.BlockSpec((SUBLANES, d), nxt)]


def _rwkv_inputs(x, mod, tables, nw, mu, w_rkv, g1, g2, w1, w2, a1, a2, w0, a0, tm):
    m_rows, d = x.shape
    n_tiles = m_rows // tm
    mu_rkv = jnp.stack([mu[0], mu[2], mu[3]])[:, None, :]
    rkv = pl.pallas_call(
        _rkv_kernel,
        grid_spec=pltpu.PrefetchScalarGridSpec(
            num_scalar_prefetch=3, grid=(3, n_tiles),
            in_specs=_halo_specs(tm, d, m_rows, 1) + [
                pl.BlockSpec((1, 6, d), lambda j, i, g, *_: (g[i], 0, 0)),
                pl.BlockSpec((1, d), lambda j, i, *_: (0, 0)),
                pl.BlockSpec((1, 1, d), lambda j, i, *_: (j, 0, 0)),
                pl.BlockSpec((1, d, d), lambda j, i, *_: (j, 0, 0))],
            out_specs=pl.BlockSpec((1, tm, d), lambda j, i, *_: (j, i, 0))),
        out_shape=jax.ShapeDtypeStruct((3, m_rows, d), F32),
        compiler_params=_cparams("arbitrary", "arbitrary"),
        name="rwkv_rkv_proj",
    )(*tables, x, x, x, mod, nw, mu_rkv, w_rkv)

    full = lambda a: pl.BlockSpec(a.shape, lambda i, *_: (0,) * a.ndim)
    small = [g1, g2, w1, w2, a1, a2, w0, a0]
    gate, lw, a = pl.pallas_call(
        _lora_kernel,
        grid_spec=pltpu.PrefetchScalarGridSpec(
            num_scalar_prefetch=3, grid=(n_tiles,),
            in_specs=_halo_specs(tm, d, m_rows, 0) + [
                pl.BlockSpec((1, 6, d), lambda i, g, *_: (g[i], 0, 0)),
                pl.BlockSpec((1, d), lambda i, *_: (0, 0)),
                full(mu)] + [full(s) for s in small],
            out_specs=[pl.BlockSpec((tm, d), lambda i, *_: (i, 0)),
                       pl.BlockSpec((2, tm, d), lambda i, *_: (0, i, 0)),
                       pl.BlockSpec((2, tm, d), lambda i, *_: (0, i, 0))]),
        out_shape=[jax.ShapeDtypeStruct((m_rows, d), F32),
                   jax.ShapeDtypeStruct((2, m_rows, d), F32),
                   jax.ShapeDtypeStruct((2, m_rows, d), F32)],
        compiler_params=_cparams("arbitrary"),
        name="rwkv_lora_proj",
    )(*tables, x, x, x, mod, nw, mu, *small)
    return rkv, gate, lw, a


def _rwkv_scan_kernel(*refs, t_len, has_s0):
    if has_s0:
        (r_ref, k_ref, v_ref, gate_ref, lw_ref, a_ref, kk_ref, ka_ref, rk_ref, lnw_ref, lnb_ref,
         s0_ref, y_ref, sfin_ref, ybuf, bonus_buf, s_ref) = refs
    else:
        (r_ref, k_ref, v_ref, gate_ref, lw_ref, a_ref, kk_ref, ka_ref, rk_ref, lnw_ref, lnb_ref,
         y_ref, sfin_ref, ybuf, bonus_buf, s_ref) = refs
    c_len, lanes, heads = CHUNK, RW_LANES, RW_GROUP_HEADS
    n_chunks = t_len // c_len
    stack = heads * c_len

    same_head = (_iota((stack, lanes), 0) // c_len == _iota((stack, lanes), 1) // RW_HEAD).astype(F32)
    ones_blk = same_head
    rep = (_iota((RW_HEAD, lanes), 1) % RW_HEAD == _iota((RW_HEAD, lanes), 0)).astype(F32)
    t_idx = _iota((stack, c_len), 0) % c_len
    s_idx = _iota((stack, c_len), 1)
    tri_t = _iota((c_len, c_len), 0)
    tri_s = _iota((c_len, c_len), 1)

    kk_p, ka_p, rk_p = kk_ref[...], ka_ref[...], rk_ref[...]
    lnw, lnb = lnw_ref[...], lnb_ref[...]

    def fold(z):
        z = z * same_head
        return z[0:c_len] + z[c_len:2 * c_len] + z[2 * c_len:3 * c_len] + z[3 * c_len:4 * c_len]

    def tile4(z):
        return jnp.concatenate([z, z, z, z], axis=0) * same_head

    for d in range(2):
        if has_s0:
            s_ref[d] = _dot_hi(s0_ref[0, d], rep) * ones_blk
        else:
            s_ref[d] = jnp.zeros((lanes, lanes), F32)

    def chunk_step(c, d):
        r0 = pl.multiple_of(c * c_len, c_len)
        rows = pl.ds(r0, c_len)
        r, k, v = r_ref[rows, :], k_ref[rows, :], v_ref[rows, :]
        lw, a = lw_ref[d, rows, :], a_ref[d, rows, :]
        kx = k * kk_p
        kkn = kx * lax.rsqrt(_dot_hi(kx * kx, ones_blk) + EPS)
        kd = k * (1.0 + (a - 1.0) * ka_p)
        if d == 0:
            incl, strict, incl4, strict4 = tri_s <= tri_t, None, s_idx <= t_idx, s_idx < t_idx
        else:
            incl, strict, incl4, strict4 = tri_s >= tri_t, None, s_idx >= t_idx, s_idx > t_idx
        cw = _dot_hi(incl.astype(F32), lw)
        wtot = jnp.sum(lw, axis=0, keepdims=True)
        e_neg = jnp.exp(-cw)
        at = -kkn * jnp.exp(cw - lw)
        bt = kkn * a * e_neg
        kt = kd * e_neg
        rt = r * jnp.exp(cw)
        lhs = jnp.concatenate([tile4(at), tile4(rt)], axis=0)
        g_b = _dot_nt_hi(lhs, bt)
        g_k = _dot_nt_hi(lhs, kt)
        n_s = jnp.where(strict4, g_b[0:stack], 0.0)
        ak_s = jnp.where(strict4, g_k[0:stack], 0.0)
        rb_s = jnp.where(incl4, g_b[stack:], 0.0)
        rk_s = jnp.where(incl4, g_k[stack:], 0.0)
        t_s = _unit_tri_inverse(n_s.reshape(heads, c_len, c_len)).reshape(stack, c_len)
        s_mat = s_ref[d]
        x = _dot_nt_hi(at, s_mat) + fold(_dot_hi(ak_s, v))
        u = fold(_dot_hi(t_s, x))
        y = _dot_nt_hi(rt, s_mat) + fold(_dot_hi(rb_s, u) + _dot_hi(rk_s, v))
        s_ref[d] = (s_mat + (_dot_tn_hi(u, bt) + _dot_tn_hi(v, kt)) * ones_blk) * jnp.exp(wtot)
        bonus = _dot_hi(r * kd * rk_p, ones_blk) * v
        if d == 0:
            ybuf[rows, :] = y
            bonus_buf[rows, :] = bonus
        else:
            y = ybuf[rows, :] + y
            mean = _dot_hi(y, ones_blk) * (1.0 / RW_HEAD)
            yc = y - mean
            var = _dot_hi(yc * yc, ones_blk) * (1.0 / RW_HEAD)
            yn = yc * lax.rsqrt(var + RW_HEAD * 1e-5) * lnw + lnb
            y_ref[rows, :] = ((yn + bonus_buf[rows, :] + bonus) * gate_ref[rows, :]).astype(y_ref.dtype)

    def fwd_body(c, carry):
        chunk_step(c, 0)
        return carry

    def bwd_body(c, carry):
        chunk_step(n_chunks - 1 - c, 1)
        return carry

    lax.fori_loop(0, n_chunks, fwd_body, 0)
    lax.fori_loop(0, n_chunks, bwd_body, 0)
    rep_t = (_iota((lanes, RW_HEAD), 0) % RW_HEAD == _iota((lanes, RW_HEAD), 1)).astype(F32)
    for d in range(2):
        sfin_ref[0, d] = _dot_hi(s_ref[d], rep_t)


def _rwkv_scan(rkv, gate, lw, a, kk, ka, rk, lnw, lnb, s0, row0, n_seq, t_len):
    _, m_rows, d = rkv.shape
    n_groups = d // RW_LANES
    blk0 = row0 // t_len
    tok = lambda b, g: (blk0 + b, g)
    in_specs = [pl.BlockSpec((None, t_len, RW_LANES), lambda b, g, j=j: (j, blk0 + b, g)) for j in range(3)]
    in_specs += [pl.BlockSpec((t_len, RW_LANES), tok),
                 pl.BlockSpec((2, t_len, RW_LANES), lambda b, g: (0, blk0 + b, g)),
                 pl.BlockSpec((2, t_len, RW_LANES), lambda b, g: (0, blk0 + b, g))]
    in_specs += [pl.BlockSpec((1, RW_LANES), lambda b, g: (0, g))] * 5
    args = [rkv, rkv, rkv, gate, lw, a, kk, ka, rk, lnw, lnb]
    if s0 is not None:
        in_specs.append(pl.BlockSpec((1, 2, RW_LANES, RW_HEAD), lambda b, g: (b, 0, g, 0)))
        args.append(s0)
    y, sfin = pl.pallas_call(
        functools.partial(_rwkv_scan_kernel, t_len=t_len, has_s0=s0 is not None),
        grid=(n_seq, n_groups),
        in_specs=in_specs,
        out_specs=[pl.BlockSpec((t_len, RW_LANES), lambda b, g: (b, g)),
                   pl.BlockSpec((1, 2, RW_LANES, RW_HEAD), lambda b, g: (b, 0, g, 0))],
        out_shape=[jax.ShapeDtypeStruct((n_seq * t_len, d), BF16),
                   jax.ShapeDtypeStruct((n_seq, 2, d, RW_HEAD), F32)],
        scratch_shapes=[pltpu.VMEM((t_len, RW_LANES), F32), pltpu.VMEM((t_len, RW_LANES), F32),
                        pltpu.VMEM((2, RW_LANES, RW_LANES), F32)],
        compiler_params=_cparams("arbitrary", "arbitrary"),
        name=f"rwkv_scan_t{t_len}",
    )(*args)
    return y, sfin


def _out_proj_kernel(grp_ref, y_ref, w_ref, x_ref, mod_ref, o_ref, *, gate_row):
    o_ref[...] = x_ref[...] + mod_ref[0, gate_row:gate_row + 1, :] * _dot(y_ref[...], w_ref[...])


def _out_proj(y, w, x, mod, grp, gate_row, tm):
    m_rows, k_dim = y.shape
    d = w.shape[1]
    tn = 1024
    return pl.pallas_call(
        functools.partial(_out_proj_kernel, gate_row=gate_row),
        grid_spec=pltpu.PrefetchScalarGridSpec(
            num_scalar_prefetch=1, grid=(d // tn, m_rows // tm),
            in_specs=[pl.BlockSpec((tm, k_dim), lambda n, i, g: (i, 0)),
                      pl.BlockSpec((k_dim, tn), lambda n, i, g: (0, n)),
                      pl.BlockSpec((tm, tn), lambda n, i, g: (i, n)),
                      pl.BlockSpec((1, 6, tn), lambda n, i, g: (g[i], 0, n))],
            out_specs=pl.BlockSpec((tm, tn), lambda n, i, g: (i, n))),
        out_shape=jax.ShapeDtypeStruct((m_rows, d), F32),
        compiler_params=_cparams("arbitrary", "arbitrary"),
        name="out_proj",
    )(grp, y, w, x, mod)


def _in_proj_kernel(grp_ref, x_ref, mod_ref, nw_ref, w_ref, o_ref):
    h = _rms_mod(x_ref[...], nw_ref[...], mod_ref[0, 0:1, :], mod_ref[0, 1:2, :])
    o_ref[...] = _dot(h, w_ref[...])


def _in_proj(x, mod, grp, nw, w, tm, tn):
    m_rows, d = x.shape
    n = w.shape[1]
    return pl.pallas_call(
        _in_proj_kernel,
        grid_spec=pltpu.PrefetchScalarGridSpec(
            num_scalar_prefetch=1, grid=(n // tn, m_rows // tm),
            in_specs=[pl.BlockSpec((tm, d), lambda n_, i, g: (i, 0)),
                      pl.BlockSpec((1, 6, d), lambda n_, i, g: (g[i], 0, 0)),
                      pl.BlockSpec((1, d), lambda n_, i, g: (0, 0)),
                      pl.BlockSpec((d, tn), lambda n_, i, g: (0, n_))],
            out_specs=pl.BlockSpec((tm, tn), lambda n_, i, g: (i, n_))),
        out_shape=jax.ShapeDtypeStruct((m_rows, n), F32),
        compiler_params=_cparams("arbitrary", "arbitrary"),
        name="gdn_in_proj",
    )(grp, x, mod, nw, w)


def _ffn_kernel(grp_ref, x_ref, mod_ref, nw_ref, w1_ref, w3_ref, w2_ref, fw_ref, o_ref, h_buf, acc,
                *, final_norm):
    f = pl.program_id(1)

    @pl.when(f == 0)
    def _():
        h_buf[...] = _rms_mod(x_ref[...], nw_ref[...], mod_ref[0, 3:4, :], mod_ref[0, 4:5, :]).astype(BF16)
        acc[...] = jnp.zeros_like(acc)

    h = h_buf[...]
    gate = jnp.dot(h, w1_ref[...], preferred_element_type=F32)
    up = jnp.dot(h, w3_ref[...], preferred_element_type=F32)
    acc[...] += _dot(_silu(gate) * up, w2_ref[...])

    @pl.when(f == pl.num_programs(1) - 1)
    def _():
        y = x_ref[...] + mod_ref[0, 5:6, :] * acc[...]
        if final_norm:
            y = y * lax.rsqrt(jnp.mean(y * y, -1, keepdims=True) + EPS) * fw_ref[...]
        o_ref[...] = y


def _ffn(x, mod, grp, nw, w1, w3, w2, fw, final_norm, tm):
    m_rows, d = x.shape
    d_ff = w1.shape[1]
    tf = 512
    return pl.pallas_call(
        functools.partial(_ffn_kernel, final_norm=final_norm),
        grid_spec=pltpu.PrefetchScalarGridSpec(
            num_scalar_prefetch=1, grid=(m_rows // tm, d_ff // tf),
            in_specs=[pl.BlockSpec((tm, d), lambda i, f, g: (i, 0)),
                      pl.BlockSpec((1, 6, d), lambda i, f, g: (g[i], 0, 0)),
                      pl.BlockSpec((1, d), lambda i, f, g: (0, 0)),
                      pl.BlockSpec((d, tf), lambda i, f, g: (0, f)),
                      pl.BlockSpec((d, tf), lambda i, f, g: (0, f)),
                      pl.BlockSpec((tf, d), lambda i, f, g: (f, 0)),
                      pl.BlockSpec((1, d), lambda i, f, g: (0, 0))],
            out_specs=pl.BlockSpec((tm, d), lambda i, f, g: (i, 0)),
            scratch_shapes=[pltpu.VMEM((tm, d), BF16), pltpu.VMEM((tm, d), F32)]),
        out_shape=jax.ShapeDtypeStruct((m_rows, d), F32),
        compiler_params=_cparams("arbitrary", "arbitrary"),
        name="ffn",
    )(grp, x, mod, nw, w1, w3, w2, fw)


def _gdn_kernel(*refs, t_len, has_s0):
    if has_s0:
        (q_ref, k_ref, v_ref, z_ref, ab_ref, cq_ref, ck_ref, cv_ref, alog_ref, dtb_ref, nw_ref,
         s0_ref, o_ref, sfin_ref, qkv_buf, obuf, s_ref) = refs
    else:
        (q_ref, k_ref, v_ref, z_ref, ab_ref, cq_ref, ck_ref, cv_ref, alog_ref, dtb_ref, nw_ref,
         o_ref, sfin_ref, qkv_buf, obuf, s_ref) = refs
    c_len = CHUNK
    n_chunks = t_len // c_len
    kh = pl.program_id(1)
    n_vh = 32
    tri_t = _iota((c_len, c_len), 0)
    tri_s = _iota((c_len, c_len), 1)
    lane_ab = _iota((c_len, 128), 1)
    row_abt = _iota((128, c_len), 0)
    lane_1 = _iota((1, 128), 1)

    for i in range(4):
        if has_s0:
            s_ref[i] = s0_ref[0, i // 2, i % 2]
        else:
            s_ref[i] = jnp.zeros((GD_DK, GD_DV), F32)

    def conv_silu(ref, w_ref, c):
        r0 = pl.multiple_of(c * c_len, c_len)
        main = ref[pl.ds(r0, c_len), :]
        up0 = pl.multiple_of(jnp.maximum(r0 - SUBLANES, 0), SUBLANES)
        dn0 = pl.multiple_of(jnp.minimum(r0 + c_len, t_len - SUBLANES), SUBLANES)
        up = ref[pl.ds(up0, SUBLANES), :] * (c > 0).astype(F32)
        dn = ref[pl.ds(dn0, SUBLANES), :] * (c < n_chunks - 1).astype(F32)
        ext = jnp.concatenate([up, main, dn], axis=0)
        w = w_ref[...]
        acc = ext[SUBLANES - 1:SUBLANES - 1 + c_len] * w[0:1]
        for j in range(1, GD_CONV):
            acc = acc + ext[SUBLANES - 1 + j:SUBLANES - 1 + j + c_len] * w[j:j + 1]
        return _silu(acc)

    def l2n(z):
        return z * lax.rsqrt(jnp.sum(z * z, -1, keepdims=True) + EPS)

    def chunk_step(c, d):
        r0 = pl.multiple_of(c * c_len, c_len)
        rows = pl.ds(r0, c_len)
        if d == 0:
            q = l2n(conv_silu(q_ref, cq_ref, c)) * (GD_DK ** -0.5)
            k = l2n(conv_silu(k_ref, ck_ref, c))
            v2 = conv_silu(v_ref, cv_ref, c)
            qkv_buf[rows, 0:GD_DK] = q
            qkv_buf[rows, GD_DK:2 * GD_DK] = k
            qkv_buf[rows, 2 * GD_DK:] = v2
        else:
            q = qkv_buf[rows, 0:GD_DK]
            k = qkv_buf[rows, GD_DK:2 * GD_DK]
            v2 = qkv_buf[rows, 2 * GD_DK:]
        ab = ab_ref[rows, :]
        g_all = -jnp.exp(alog_ref[...]) * _softplus(ab + dtb_ref[...])
        beta_all = jax.nn.sigmoid(ab)
        incl = (tri_s <= tri_t) if d == 0 else (tri_s >= tri_t)
        strict = (tri_s < tri_t) if d == 0 else (tri_s > tri_t)
        gc_all = _dot_hi(incl.astype(F32), g_all)
        incl_t = (tri_t <= tri_s) if d == 0 else (tri_t >= tri_s)
        gct_all = _dot_tn_hi(g_all, incl_t.astype(F32))
        gtot_all = jnp.sum(g_all, axis=0, keepdims=True)
        g_kk = _dot_nt_hi(k, k)
        g_qk = _dot_nt_hi(q, k)
        for vl in range(2):
            col_g = d * 64 + 2 * kh + vl
            col_b = col_g + n_vh
            gc_col = jnp.sum(jnp.where(lane_ab == col_g, gc_all, 0.0), axis=1, keepdims=True)
            beta = jnp.sum(jnp.where(lane_ab == col_b, beta_all, 0.0), axis=1, keepdims=True)
            gc_row = jnp.sum(jnp.where(row_abt == col_g, gct_all, 0.0), axis=0, keepdims=True)
            g_last = jnp.sum(jnp.where(lane_1 == col_g, gtot_all, 0.0), axis=1, keepdims=True)
            decay = jnp.where(incl, jnp.exp(jnp.minimum(gc_col - gc_row, 0.0)), 0.0)
            a_mat = jnp.where(strict, beta * g_kk * decay, 0.0)
            t_mat = _unit_tri_inverse(-a_mat[None])[0]
            v = v2[:, vl * GD_DV:(vl + 1) * GD_DV]
            e_gc = jnp.exp(gc_col)
            u = _dot_hi(t_mat, v * beta)
            w = _dot_hi(t_mat, k * (beta * e_gc))
            s_mat = s_ref[2 * d + vl]
            v_new = u - _dot_hi(w, s_mat)
            o = _dot_hi(q * e_gc, s_mat) + _dot_hi(g_qk * decay, v_new)
            s_ref[2 * d + vl] = s_mat * jnp.exp(g_last) + _dot_tn_hi(k * jnp.exp(g_last - gc_col), v_new)
            cols = pl.ds(vl * GD_DV, GD_DV)
            if d == 0:
                obuf[rows, cols] = o
            else:
                o = obuf[rows, cols] + o
                z = z_ref[rows, cols]
                o = o * lax.rsqrt(jnp.mean(o * o, -1, keepdims=True) + EPS) * nw_ref[...] * _silu(z)
                o_ref[rows, cols] = o.astype(o_ref.dtype)

    def fwd_body(c, carry):
        chunk_step(c, 0)
        return carry

    def bwd_body(c, carry):
        chunk_step(n_chunks - 1 - c, 1)
        return carry

    lax.fori_loop(0, n_chunks, fwd_body, 0)
    lax.fori_loop(0, n_chunks, bwd_body, 0)
    for i in range(4):
        sfin_ref[0, i // 2, i % 2] = s_ref[i]


def _gdn_scan(proj, ab, conv_w, alog_row, dtb_row, nw, s0, row0, n_seq, t_len):
    n_kh = 16
    blk0 = row0 // t_len
    vw = 2 * GD_DV
    in_specs = [pl.BlockSpec((t_len, GD_DK), lambda b, h: (blk0 + b, h)),
                pl.BlockSpec((t_len, GD_DK), lambda b, h: (blk0 + b, n_kh + h)),
                pl.BlockSpec((t_len, vw), lambda b, h: (blk0 + b, n_kh + h)),
                pl.BlockSpec((t_len, vw), lambda b, h: (blk0 + b, 2 * n_kh + h)),
                pl.BlockSpec((t_len, 128), lambda b, h: (blk0 + b, 0)),
                pl.BlockSpec((GD_CONV, GD_DK), lambda b, h: (0, h)),
                pl.BlockSpec((GD_CONV, GD_DK), lambda b, h: (0, n_kh + h)),
                pl.BlockSpec((GD_CONV, vw), lambda b, h: (0, n_kh + h)),
                pl.BlockSpec((1, 128), lambda b, h: (0, 0)),
                pl.BlockSpec((1, 128), lambda b, h: (0, 0)),
                pl.BlockSpec((1, GD_DV), lambda b, h: (0, 0))]
    args = [proj, proj, proj, proj, ab, conv_w, conv_w, conv_w, alog_row, dtb_row, nw]
    if s0 is not None:
        in_specs.append(pl.BlockSpec((1, 2, 2, GD_DK, GD_DV), lambda b, h: (b, 0, h, 0, 0)))
        args.append(s0)
    o, sfin = pl.pallas_call(
        functools.partial(_gdn_kernel, t_len=t_len, has_s0=s0 is not None),
        grid=(n_seq, n_kh),
        in_specs=in_specs,
        out_specs=[pl.BlockSpec((t_len, vw), lambda b, h: (b, h)),
                   pl.BlockSpec((1, 2, 2, GD_DK, GD_DV), lambda b, h: (b, 0, h, 0, 0))],
        out_shape=[jax.ShapeDtypeStruct((n_seq * t_len, 2 * n_kh * GD_DV), BF16),
                   jax.ShapeDtypeStruct((n_seq, 2, 2 * n_kh, GD_DK, GD_DV), F32)],
        scratch_shapes=[pltpu.VMEM((t_len, 2 * GD_DK + vw), F32), pltpu.VMEM((t_len, vw), F32),
                        pltpu.VMEM((4, GD_DK, GD_DV), F32)],
        compiler_params=_cparams("arbitrary", "arbitrary"),
        name=f"gdn_scan_t{t_len}",
    )(*args)
    return o, sfin


def _grid_pos_embed(n_tokens, d_model):
    rows = n_tokens // GRID_W
    row = jnp.broadcast_to(jnp.arange(rows, dtype=F32)[:, None], (rows, GRID_W)).reshape(-1)
    col = jnp.broadcast_to(jnp.arange(GRID_W, dtype=F32)[None, :], (rows, GRID_W)).reshape(-1)
    quarter = d_model // 4
    omega = 1.0 / (POS_BASE ** (jnp.arange(quarter, dtype=F32) / quarter))
    ar = row[:, None] * omega
    ac = col[:, None] * omega
    return jnp.concatenate([jnp.sin(ar), jnp.cos(ar), jnp.sin(ac), jnp.cos(ac)], -1)


def _pad_axis(a, axis, size):
    pad = [(0, 0)] * a.ndim
    pad[axis] = (0, size - a.shape[axis])
    return jnp.pad(a, pad)


def kernel(x_prompt, x_sample, state_rwkv, state_gdn, c, c_ctx, norm_mix, norm_ffn, norm_final, w_mod, b_mod, ffn_w1, ffn_w3, ffn_w2, rw_mu, rw_wr, rw_wk, rw_wv, rw_wo, rw_w0, rw_w1, rw_w2, rw_a0, rw_a1, rw_a2, rw_g1, rw_g2, rw_kk, rw_ka, rw_rk, rw_ln_w, rw_ln_b, gd_w_in, gd_conv, gd_a_log, gd_dt_bias, gd_norm, gd_w_out):
    n_p, t_p, d = x_prompt.shape
    n_s, t_s, _ = x_sample.shape
    rows_p = n_p * t_p
    bf = lambda a: a.astype(BF16)

    x = jnp.concatenate([x_prompt.reshape(rows_p, d),
                         (x_sample + _grid_pos_embed(t_s, d)[None]).reshape(n_s * t_s, d)], axis=0)
    m_rows = x.shape[0]
    regions = [(n_p, t_p, [0] * n_p), (n_s, t_s, list(range(1, n_s + 1)))]
    tm_shift = 256
    tm_big = 512
    tables = _tile_tables(regions, tm_shift)
    grp_big = _tile_tables(regions, tm_big)[0]

    cond = _pad_axis(jnp.concatenate([c_ctx[None], c], axis=0), 0, SUBLANES)
    mod = _modulation(cond, w_mod, b_mod).reshape(w_mod.shape[0], SUBLANES, 6, d)

    row = lambda a: a.reshape(1, -1)
    w_rkv = bf(jnp.stack([rw_wr[0], rw_wk[0], rw_wv[0]]))
    lp = RW_LORA_PAD
    rkv, gate, lw, a = _rwkv_inputs(
        x, mod[0], tables, row(norm_mix[0]), rw_mu[0], w_rkv,
        bf(rw_g1[0]), bf(rw_g2[0]),
        bf(_pad_axis(rw_w1[0], 2, lp)), bf(_pad_axis(rw_w2[0], 1, lp)),
        bf(_pad_axis(rw_a1[0], 2, lp)), bf(_pad_axis(rw_a2[0], 1, lp)),
        rw_w0[0][:, None, :], rw_a0[0][:, None, :], tm_shift)
    scan_args = (rkv, gate, lw, a, row(rw_kk[0]), row(rw_ka[0]), row(rw_rk[0]), row(rw_ln_w[0]), row(rw_ln_b[0]))
    y_p, s_rwkv = _rwkv_scan(*scan_args, None, 0, n_p, t_p)
    s0_rw = state_rwkv[:, 0].reshape(n_s, 2, d, RW_HEAD)
    y_s, _ = _rwkv_scan(*scan_args, s0_rw, rows_p, n_s, t_s)
    y = jnp.concatenate([y_p, y_s], axis=0)
    x = _out_proj(y, bf(rw_wo[0]), x, mod[0], grp_big, 2, tm_big)
    x = _ffn(x, mod[0], grp_big, row(norm_ffn[0]), bf(ffn_w1[0]), bf(ffn_w3[0]), bf(ffn_w2[0]),
             row(norm_final), False, tm_big)

    n_main = 3 * 4096
    proj = _in_proj(x, mod[1], grp_big, row(norm_mix[1]), bf(gd_w_in[0][:, :n_main]), tm_big, 1024)
    ab = _in_proj(x, mod[1], grp_big, row(norm_mix[1]), bf(gd_w_in[0][:, n_main:]), tm_big, 128)
    zeros32 = jnp.zeros((2, 32), F32)
    alog_row = jnp.stack([gd_a_log[0], zeros32], axis=1).reshape(1, 128)
    dtb_row = jnp.stack([gd_dt_bias[0], zeros32], axis=1).reshape(1, 128)
    gd_args = (proj, ab, gd_conv[0], alog_row, dtb_row, row(gd_norm[0]))
    o_p, s_gdn = _gdn_scan(*gd_args, None, 0, n_p, t_p)
    o_s, _ = _gdn_scan(*gd_args, state_gdn[:, 0], rows_p, n_s, t_s)
    o = jnp.concatenate([o_p, o_s], axis=0)
    x = _out_proj(o, bf(gd_w_out[0]), x, mod[1], grp_big, 2, tm_big)
    x = _ffn(x, mod[1], grp_big, row(norm_ffn[1]), bf(ffn_w1[1]), bf(ffn_w3[1]), bf(ffn_w2[1]),
             row(norm_final), True, tm_big)

    y_prompt = x[:rows_p].reshape(n_p, t_p, d)
    y_sample = x[rows_p:].reshape(n_s, t_s, d)
    new_state_rwkv = s_rwkv.reshape(n_p, 1, 2, d // RW_HEAD, RW_HEAD, RW_HEAD)
    new_state_gdn = s_gdn.reshape(n_p, 1, 2, 32, GD_DK, GD_DV)
    return (y_prompt, y_sample, new_state_rwkv, new_state_gdn)
```

```python
import functools

import numpy as np
import jax
import jax.numpy as jnp
from jax import lax
from jax.experimental import pallas as pl
from jax.experimental.pallas import tpu as pltpu

F32 = jnp.float32
BF16 = jnp.bfloat16

EPS = 1e-6
GRID_W = 64
POS_BASE = 10000.0
RW_HEAD = 64
RW_GROUP_HEADS = 4
RW_LANES = RW_HEAD * RW_GROUP_HEADS
RW_LORA_PAD = 128
GD_DK = 128
GD_DV = 128
GD_CONV = 4
CHUNK = 64
RW_SUPER_CHUNKS = 8
SUBLANES = 8
VMEM_LIMIT = 56 * 1024 * 1024


def _cparams(*sem):
    return pltpu.CompilerParams(dimension_semantics=sem, vmem_limit_bytes=VMEM_LIMIT)


def _dot(a, b):
    return jnp.dot(a.astype(BF16), b.astype(BF16), preferred_element_type=F32)


_DIMS = {"nn": (((1,), (0,)), ((), ())),
         "nt": (((1,), (1,)), ((), ())),
         "tn": (((0,), (0,)), ((), ())),
         "bnn": (((2,), (1,)), ((0,), (0,)))}


def _split_bf16(x, pieces):
    out = []
    for _ in range(pieces - 1):
        p = x.astype(BF16)
        out.append(p)
        x = x - p.astype(F32)
    out.append(x.astype(BF16))
    return out


def _mm(a, b, dims="nn", mode="b"):
    dn = _DIMS[dims]
    dg = lambda x, y: lax.dot_general(x, y, dn, preferred_element_type=F32)
    if mode == "b":
        return dg(a.astype(BF16), b.astype(BF16))
    if mode == "la":
        a0 = a.astype(BF16)
        return sum(dg(a0, p) for p in _split_bf16(b, 3))
    if mode == "ra":
        b0 = b.astype(BF16)
        return sum(dg(p, b0) for p in _split_bf16(a, 3))
    raise ValueError(mode)


P_INV = "b"
P_GRAM = "b"
P_APPLY = "b"
P_SUM = "ra"
P_CUM = "la"


def _iota(shape, dim):
    return lax.broadcasted_iota(jnp.int32, shape, dim)


def _silu(x):
    return x * jax.nn.sigmoid(x)


def _softplus(x):
    return jnp.maximum(x, 0.0) + jnp.log(1.0 + jnp.exp(-jnp.abs(x)))


def _rms_mod(x, nw, sh, sc):
    y = x * lax.rsqrt(jnp.mean(x * x, -1, keepdims=True) + EPS)
    return (y * nw) * (1.0 + sc) + sh


def _unit_tri_inverse(n_mat):
    c_len = n_mat.shape[-1]
    ij = _iota(n_mat.shape, 1) ^ _iota(n_mat.shape, 2)
    d_mat = jnp.where(ij == 0, 1.0, jnp.where(ij == 1, n_mat, 0.0))
    half = 2
    while half < c_len:
        l_mat = jnp.where((ij >= half) & (ij < 2 * half), n_mat, 0.0)
        d_mat = d_mat + _mm(_mm(d_mat, l_mat, "bnn", P_INV), d_mat, "bnn", P_INV)
        half *= 2
    return d_mat


def _mod_kernel(c_ref, w_ref, b_ref, o_ref):
    o_ref[0] = _dot(_silu(c_ref[...]), w_ref[0]) + b_ref[0]


def _modulation(cond, w_mod, b_mod):
    depth, d, n = w_mod.shape
    tn = 1024
    return pl.pallas_call(
        _mod_kernel,
        grid=(depth, n // tn),
        in_specs=[pl.BlockSpec((SUBLANES, d), lambda l, j: (0, 0)),
                  pl.BlockSpec((1, d, tn), lambda l, j: (l, 0, j)),
                  pl.BlockSpec((1, 1, tn), lambda l, j: (l, 0, j))],
        out_specs=pl.BlockSpec((1, SUBLANES, tn), lambda l, j: (l, 0, j)),
        out_shape=jax.ShapeDtypeStruct((depth, SUBLANES, n), F32),
        compiler_params=_cparams("arbitrary", "arbitrary"),
        name="modulation",
    )(cond, w_mod, b_mod.reshape(depth, 1, n))


def _shift_mix(x_ref, xp_ref, xn_ref, mod_ref, nw_ref, has_prev, has_next):
    nw = nw_ref[...]
    sh = mod_ref[0, 0:1, :]
    sc = mod_ref[0, 1:2, :]
    h = _rms_mod(x_ref[...], nw, sh, sc)
    tm = h.shape[0]
    hp = _rms_mod(xp_ref[SUBLANES - 1:SUBLANES, :], nw, sh, sc) * has_prev
    hn = _rms_mod(xn_ref[0:1, :], nw, sh, sc) * has_next
    rows = _iota(h.shape, 0)
    prev = jnp.where(rows == 0, hp, pltpu.roll(h, 1, 0))
    nxt = jnp.where(rows == tm - 1, hn, pltpu.roll(h, tm - 1, 0))
    return h, 0.5 * (prev + nxt) - h


def _rkv_kernel(grp_ref, hp_ref, hn_ref, x_ref, xp_ref, xn_ref, mod_ref, nw_ref, mu_ref, w_ref, o_ref):
    i = pl.program_id(1)
    h, xx = _shift_mix(x_ref, xp_ref, xn_ref, mod_ref, nw_ref,
                       hp_ref[i].astype(F32), hn_ref[i].astype(F32))
    o_ref[0] = _dot(h + xx * mu_ref[0], w_ref[0])


def _lora_kernel(grp_ref, hp_ref, hn_ref, x_ref, xp_ref, xn_ref, mod_ref, nw_ref, mu_ref,
                 g1_ref, g2_ref, w1_ref, w2_ref, a1_ref, a2_ref, w0_ref, a0_ref,
                 gate_ref, lw_ref, a_ref):
    i = pl.program_id(0)
    h, xx = _shift_mix(x_ref, xp_ref, xn_ref, mod_ref, nw_ref,
                       hp_ref[i].astype(F32), hn_ref[i].astype(F32))
    xw = h + xx * mu_ref[1:2, :]
    xa = h + xx * mu_ref[4:5, :]
    xg = h + xx * mu_ref[5:6, :]
    gate_ref[...] = _dot(jax.nn.sigmoid(_dot(xg, g1_ref[...])), g2_ref[...])
    for d in range(2):
        lw = jnp.tanh(_dot(xw, w1_ref[d]))
        w_log = -_softplus(-(w0_ref[d] + _dot(lw, w2_ref[d]))) - 0.5
        lw_ref[d] = -jnp.exp(w_log)
        a_ref[d] = jax.nn.sigmoid(a0_ref[d] + _dot(_dot(xa, a1_ref[d]), a2_ref[d]))


def _tile_tables(regions, tm):
    row_grp, row_pos, row_len = [], [], []
    for n_seq, t_len, groups in regions:
        for s in range(n_seq):
            row_grp += [groups[s]] * t_len
            row_pos += list(range(t_len))
            row_len += [t_len] * t_len
    row_grp, row_pos, row_len = (np.asarray(v).reshape(-1, tm) for v in (row_grp, row_pos, row_len))
    assert (row_grp == row_grp[:, :1]).all(), "a row tile must not straddle modulation groups"
    as_i32 = lambda v: jnp.asarray(np.asarray(v, np.int32))
    return (as_i32(row_grp[:, 0]), as_i32(row_pos[:, 0] > 0),
            as_i32(row_pos[:, -1] < row_len[:, -1] - 1))


def _halo_specs(tm, d, m_rows, n_lead):
    blocks = tm // SUBLANES
    last = m_rows // SUBLANES - 1
    if n_lead == 1:
        cur = lambda j, i, *_: (i, 0)
        prev = lambda j, i, *_: (jnp.maximum(i * blocks - 1, 0), 0)
        nxt = lambda j, i, *_: (jnp.minimum((i + 1) * blocks, last), 0)
    else:
        cur = lambda i, *_: (i, 0)
        prev = lambda i, *_: (jnp.maximum(i * blocks - 1, 0), 0)
        nxt = lambda i, *_: (jnp.minimum((i + 1) * blocks, last), 0)
    return [pl.BlockSpec((tm, d), cur), pl.BlockSpec((SUBLANES, d), prev), pl.BlockSpec((SUBLANES, d), nxt)]


def _rwkv_inputs(x, mod, tables, nw, mu, w_rkv, g1, g2, w1, w2, a1, a2, w0, a0, tm):
    m_rows, d = x.shape
    n_tiles = m_rows // tm
    mu_rkv = jnp.stack([mu[0], mu[2], mu[3]])[:, None, :]
    rkv = pl.pallas_call(
        _rkv_kernel,
        grid_spec=pltpu.PrefetchScalarGridSpec(
            num_scalar_prefetch=3, grid=(3, n_tiles),
            in_specs=_halo_specs(tm, d, m_rows, 1) + [
                pl.BlockSpec((1, 6, d), lambda j, i, g, *_: (g[i], 0, 0)),
                pl.BlockSpec((1, d), lambda j, i, *_: (0, 0)),
                pl.BlockSpec((1, 1, d), lambda j, i, *_: (j, 0, 0)),
                pl.BlockSpec((1, d, d), lambda j, i, *_: (j, 0, 0))],
            out_specs=pl.BlockSpec((1, tm, d), lambda j, i, *_: (j, i, 0))),
        out_shape=jax.ShapeDtypeStruct((3, m_rows, d), F32),
        compiler_params=_cparams("arbitrary", "arbitrary"),
        name="rwkv_rkv_proj",
    )(*tables, x, x, x, mod, nw, mu_rkv, w_rkv)

    full = lambda a: pl.BlockSpec(a.shape, lambda i, *_: (0,) * a.ndim)
    small = [g1, g2, w1, w2, a1, a2, w0, a0]
    gate, lw, a = pl.pallas_call(
        _lora_kernel,
        grid_spec=pltpu.PrefetchScalarGridSpec(
            num_scalar_prefetch=3, grid=(n_tiles,),
            in_specs=_halo_specs(tm, d, m_rows, 0) + [
                pl.BlockSpec((1, 6, d), lambda i, g, *_: (g[i], 0, 0)),
                pl.BlockSpec((1, d), lambda i, *_: (0, 0)),
                full(mu)] + [full(s) for s in small],
            out_specs=[pl.BlockSpec((tm, d), lambda i, *_: (i, 0)),
                       pl.BlockSpec((2, tm, d), lambda i, *_: (0, i, 0)),
                       pl.BlockSpec((2, tm, d), lambda i, *_: (0, i, 0))]),
        out_shape=[jax.ShapeDtypeStruct((m_rows, d), F32),
                   jax.ShapeDtypeStruct((2, m_rows, d), F32),
                   jax.ShapeDtypeStruct((2, m_rows, d), F32)],
        compiler_params=_cparams("arbitrary"),
        name="rwkv_lora_proj",
    )(*tables, x, x, x, mod, nw, mu, *small)
    return rkv, gate, lw, a


def _rwkv_scan_kernel(*refs, t_len, n_sb, has_s0):
    (r_ref, k_ref, v_ref, gate_ref, lw_ref, a_ref, kk_ref, ka_ref, rk_ref, lnw_ref, lnb_ref) = refs[:11]
    rest = refs[11:]
    if has_s0:
        s0_ref, rest = rest[0], rest[1:]
    (y_ref, sfin_ref, r2_buf, au_buf, bw_buf, kw_buf, y0_buf, w_buf, ybuf, bonus_buf, s_ref) = rest
    c_len, lanes, heads = CHUNK, RW_LANES, RW_GROUP_HEADS
    n_chunks = t_len // c_len
    sc = min(n_chunks, RW_SUPER_CHUNKS)
    n_super = n_chunks // sc
    stack = heads * c_len

    same_head = (_iota((stack, lanes), 0) // c_len == _iota((stack, lanes), 1) // RW_HEAD).astype(F32)
    ones_blk = same_head
    rep = (_iota((RW_HEAD, lanes), 1) % RW_HEAD == _iota((RW_HEAD, lanes), 0)).astype(F32)
    t_idx = _iota((stack, c_len), 0) % c_len
    s_idx = _iota((stack, c_len), 1)
    tri_t = _iota((c_len, c_len), 0)
    tri_s = _iota((c_len, c_len), 1)

    kk_p, ka_p, rk_p = kk_ref[...], ka_ref[...], rk_ref[...]
    lnw, lnb = lnw_ref[...], lnb_ref[...]

    def fold(z):
        z = z * same_head
        return z[0:c_len] + z[c_len:2 * c_len] + z[2 * c_len:3 * c_len] + z[3 * c_len:4 * c_len]

    def tile4(z):
        return jnp.concatenate([z, z, z, z], axis=0) * same_head

    for s in range(n_sb):
        for d in range(2):
            if has_s0:
                s_ref[s * 2 + d] = _mm(s0_ref[s, d], rep, "nn", "ra") * ones_blk
            else:
                s_ref[s * 2 + d] = jnp.zeros((lanes, lanes), F32)
    ybuf[...] = jnp.zeros_like(ybuf)
    bonus_buf[...] = jnp.zeros_like(bonus_buf)

    def chunk_rows(s, d, p):
        c = p if d == 0 else n_chunks - 1 - p
        return pl.ds(pl.multiple_of((s * n_chunks + c) * c_len, c_len), c_len)

    def phase_a(p, j):
        for s in range(n_sb):
            for d in range(2):
                rows = chunk_rows(s, d, p)
                r, k, v = r_ref[rows, :], k_ref[rows, :], v_ref[rows, :]
                lw, a = lw_ref[d, rows, :], a_ref[d, rows, :]
                kx = k * kk_p
                kkn = kx * lax.rsqrt(_mm(kx * kx, ones_blk, "nn", P_SUM) + EPS)
                kd = k * (1.0 + (a - 1.0) * ka_p)
                if d == 0:
                    incl, incl4, strict4 = tri_s <= tri_t, s_idx <= t_idx, s_idx < t_idx
                else:
                    incl, incl4, strict4 = tri_s >= tri_t, s_idx >= t_idx, s_idx > t_idx
                cw = _mm(incl.astype(F32), lw, "nn", P_CUM)
                w_row = jnp.exp(jnp.sum(lw, axis=0, keepdims=True))
                e_neg = jnp.exp(-cw)
                at = -kkn * jnp.exp(cw - lw)
                bt = kkn * a * e_neg
                kt = kd * e_neg
                rt = r * jnp.exp(cw)
                lhs = jnp.concatenate([tile4(at), tile4(rt)], axis=0)
                g_b = _mm(lhs, bt, "nt", P_GRAM)
                g_k = _mm(lhs, kt, "nt", P_GRAM)
                n_s = jnp.where(strict4, g_b[0:stack], 0.0)
                ak_s = jnp.where(strict4, g_k[0:stack], 0.0)
                rb_s = jnp.where(incl4, g_b[stack:], 0.0)
                rk_s = jnp.where(incl4, g_k[stack:], 0.0)
                t_s = _unit_tri_inverse(n_s.reshape(heads, c_len, c_len)).reshape(stack, c_len)
                a2 = fold(_mm(t_s, at, "nn", P_APPLY))
                u0 = fold(_mm(t_s, fold(_mm(ak_s, v, "nn", P_APPLY)), "nn", P_APPLY))
                r2 = rt + fold(_mm(rb_s, a2, "nn", P_APPLY))
                y0 = fold(_mm(rb_s, u0, "nn", P_APPLY) + _mm(rk_s, v, "nn", P_APPLY))
                slot = (s * 2 + d) * sc + j
                r2_buf[slot] = r2.astype(BF16)
                au_buf[slot, :, 0:lanes] = a2.astype(BF16)
                au_buf[slot, :, lanes:] = u0.astype(BF16)
                bw_buf[slot] = (bt * w_row).astype(BF16)
                kw_buf[slot] = (kt * w_row).astype(BF16)
                y0_buf[slot] = y0
                w_buf[slot] = jnp.broadcast_to(w_row, (SUBLANES, lanes))
                bonus_buf[rows, :] += _mm(r * kd * rk_p, ones_blk, "nn", P_SUM) * v

    def phase_b(p, j):
        for s in range(n_sb):
            for d in range(2):
                rows = chunk_rows(s, d, p)
                slot = (s * 2 + d) * sc + j
                s_mat = s_ref[s * 2 + d]
                s_bf = s_mat.astype(BF16)
                ybuf[rows, :] += _mm(r2_buf[slot], s_bf, "nt", P_APPLY) + y0_buf[slot]
                bw = bw_buf[slot]
                au_bw = _mm(au_buf[slot], bw, "tn", P_APPLY)
                m_s = au_bw[0:lanes] * ones_blk
                s_add = (au_bw[lanes:] + _mm(v_ref[rows, :], kw_buf[slot], "tn", P_APPLY)) * ones_blk
                s_ref[s * 2 + d] = s_mat * w_buf[slot, 0:1, :] + _mm(s_bf, m_s, "nn", P_APPLY) + s_add

    def super_body(sp, carry):
        def a_body(j, c2):
            phase_a(sp * sc + j, j)
            return c2

        def b_body(j, c2):
            phase_b(sp * sc + j, j)
            return c2

        lax.fori_loop(0, sc, a_body, 0)
        lax.fori_loop(0, sc, b_body, 0)
        return carry

    lax.fori_loop(0, n_super, super_body, 0)

    def c_body(cg, carry):
        rows = pl.ds(pl.multiple_of(cg * c_len, c_len), c_len)
        y = ybuf[rows, :]
        mean = _mm(y, ones_blk, "nn", P_SUM) * (1.0 / RW_HEAD)
        yc = y - mean
        var = _mm(yc * yc, ones_blk, "nn", P_SUM) * (1.0 / RW_HEAD)
        yn = yc * lax.rsqrt(var + RW_HEAD * 1e-5) * lnw + lnb
        y_ref[rows, :] = ((yn + bonus_buf[rows, :]) * gate_ref[rows, :]).astype(y_ref.dtype)
        return carry

    lax.fori_loop(0, n_sb * n_chunks, c_body, 0)
    rep_t = (_iota((lanes, RW_HEAD), 0) % RW_HEAD == _iota((lanes, RW_HEAD), 1)).astype(F32)
    for s in range(n_sb):
        for d in range(2):
            sfin_ref[s, d] = _mm(s_ref[s * 2 + d], rep_t, "nn", "ra")


def _rwkv_scan(rkv, gate, lw, a, kk, ka, rk, lnw, lnb, s0, row0, n_seq, t_len, n_sb):
    _, m_rows, d = rkv.shape
    n_groups = d // RW_LANES
    blk_rows = n_sb * t_len
    assert row0 % blk_rows == 0 and n_seq % n_sb == 0
    blk0 = row0 // blk_rows
    sc = min(t_len // CHUNK, RW_SUPER_CHUNKS)
    assert (t_len // CHUNK) % sc == 0
    n_slots = n_sb * 2 * sc
    tok = lambda b, g: (blk0 + b, g)
    in_specs = [pl.BlockSpec((None, blk_rows, RW_LANES), lambda b, g, j=j: (j, blk0 + b, g)) for j in range(3)]
    in_specs += [pl.BlockSpec((blk_rows, RW_LANES), tok),
                 pl.BlockSpec((2, blk_rows, RW_LANES), lambda b, g: (0, blk0 + b, g)),
                 pl.BlockSpec((2, blk_rows, RW_LANES), lambda b, g: (0, blk0 + b, g))]
    in_specs += [pl.BlockSpec((1, RW_LANES), lambda b, g: (0, g))] * 5
    args = [rkv, rkv, rkv, gate, lw, a, kk, ka, rk, lnw, lnb]
    if s0 is not None:
        in_specs.append(pl.BlockSpec((n_sb, 2, RW_LANES, RW_HEAD), lambda b, g: (b, 0, g, 0)))
        args.append(s0)
    y, sfin = pl.pallas_call(
        functools.partial(_rwkv_scan_kernel, t_len=t_len, n_sb=n_sb, has_s0=s0 is not None),
        grid=(n_seq // n_sb, n_groups),
        in_specs=in_specs,
        out_specs=[pl.BlockSpec((blk_rows, RW_LANES), lambda b, g: (b, g)),
                   pl.BlockSpec((n_sb, 2, RW_LANES, RW_HEAD), lambda b, g: (b, 0, g, 0))],
        out_shape=[jax.ShapeDtypeStruct((n_seq * t_len, d), BF16),
                   jax.ShapeDtypeStruct((n_seq, 2, d, RW_HEAD), F32)],
        scratch_shapes=[pltpu.VMEM((n_slots, CHUNK, RW_LANES), BF16),
                        pltpu.VMEM((n_slots, CHUNK, 2 * RW_LANES), BF16),
                        pltpu.VMEM((n_slots, CHUNK, RW_LANES), BF16),
                        pltpu.VMEM((n_slots, CHUNK, RW_LANES), BF16),
                        pltpu.VMEM((n_slots, CHUNK, RW_LANES), F32),
                        pltpu.VMEM((n_slots, SUBLANES, RW_LANES), F32),
                        pltpu.VMEM((blk_rows, RW_LANES), F32),
                        pltpu.VMEM((blk_rows, RW_LANES), F32),
                        pltpu.VMEM((2 * n_sb, RW_LANES, RW_LANES), F32)],
        compiler_params=_cparams("arbitrary", "arbitrary"),
        name=f"rwkv_scan_t{t_len}",
    )(*args)
    return y, sfin


def _out_proj_kernel(grp_ref, y_ref, w_ref, x_ref, mod_ref, o_ref, *, gate_row):
    o_ref[...] = x_ref[...] + mod_ref[0, gate_row:gate_row + 1, :] * _dot(y_ref[...], w_ref[...])


def _out_proj(y, w, x, mod, grp, gate_row, tm):
    m_rows, k_dim = y.shape
    d = w.shape[1]
    tn = 1024
    return pl.pallas_call(
        functools.partial(_out_proj_kernel, gate_row=gate_row),
        grid_spec=pltpu.PrefetchScalarGridSpec(
            num_scalar_prefetch=1, grid=(d // tn, m_rows // tm),
            in_specs=[pl.BlockSpec((tm, k_dim), lambda n, i, g: (i, 0)),
                      pl.BlockSpec((k_dim, tn), lambda n, i, g: (0, n)),
                      pl.BlockSpec((tm, tn), lambda n, i, g: (i, n)),
                      pl.BlockSpec((1, 6, tn), lambda n, i, g: (g[i], 0, n))],
            out_specs=pl.BlockSpec((tm, tn), lambda n, i, g: (i, n))),
        out_shape=jax.ShapeDtypeStruct((m_rows, d), F32),
        compiler_params=_cparams("arbitrary", "arbitrary"),
        name="out_proj",
    )(grp, y, w, x, mod)


def _in_proj_kernel(grp_ref, x_ref, mod_ref, nw_ref, w_ref, o_ref):
    h = _rms_mod(x_ref[...], nw_ref[...], mod_ref[0, 0:1, :], mod_ref[0, 1:2, :])
    o_ref[...] = _dot(h, w_ref[...])


def _in_proj(x, mod, grp, nw, w, tm, tn):
    m_rows, d = x.shape
    n = w.shape[1]
    return pl.pallas_call(
        _in_proj_kernel,
        grid_spec=pltpu.PrefetchScalarGridSpec(
            num_scalar_prefetch=1, grid=(n // tn, m_rows // tm),
            in_specs=[pl.BlockSpec((tm, d), lambda n_, i, g: (i, 0)),
                      pl.BlockSpec((1, 6, d), lambda n_, i, g: (g[i], 0, 0)),
                      pl.BlockSpec((1, d), lambda n_, i, g: (0, 0)),
                      pl.BlockSpec((d, tn), lambda n_, i, g: (0, n_))],
            out_specs=pl.BlockSpec((tm, tn), lambda n_, i, g: (i, n_))),
        out_shape=jax.ShapeDtypeStruct((m_rows, n), F32),
        compiler_params=_cparams("arbitrary", "arbitrary"),
        name="gdn_in_proj",
    )(grp, x, mod, nw, w)


def _ffn_kernel(grp_ref, x_ref, mod_ref, nw_ref, w1_ref, w3_ref, w2_ref, fw_ref, o_ref, h_buf, acc,
                *, final_norm):
    f = pl.program_id(1)

    @pl.when(f == 0)
    def _():
        h_buf[...] = _rms_mod(x_ref[...], nw_ref[...], mod_ref[0, 3:4, :], mod_ref[0, 4:5, :]).astype(BF16)
        acc[...] = jnp.zeros_like(acc)

    h = h_buf[...]
    gate = jnp.dot(h, w1_ref[...], preferred_element_type=F32)
    up = jnp.dot(h, w3_ref[...], preferred_element_type=F32)
    acc[...] += _dot(_silu(gate) * up, w2_ref[...])

    @pl.when(f == pl.num_programs(1) - 1)
    def _():
        y = x_ref[...] + mod_ref[0, 5:6, :] * acc[...]
        if final_norm:
            y = y * lax.rsqrt(jnp.mean(y * y, -1, keepdims=True) + EPS) * fw_ref[...]
        o_ref[...] = y


def _ffn(x, mod, grp, nw, w1, w3, w2, fw, final_norm, tm):
    m_rows, d = x.shape
    d_ff = w1.shape[1]
    tf = 512
    return pl.pallas_call(
        functools.partial(_ffn_kernel, final_norm=final_norm),
        grid_spec=pltpu.PrefetchScalarGridSpec(
            num_scalar_prefetch=1, grid=(m_rows // tm, d_ff // tf),
            in_specs=[pl.BlockSpec((tm, d), lambda i, f, g: (i, 0)),
                      pl.BlockSpec((1, 6, d), lambda i, f, g: (g[i], 0, 0)),
                      pl.BlockSpec((1, d), lambda i, f, g: (0, 0)),
                      pl.BlockSpec((d, tf), lambda i, f, g: (0, f)),
                      pl.BlockSpec((d, tf), lambda i, f, g: (0, f)),
                      pl.BlockSpec((tf, d), lambda i, f, g: (f, 0)),
                      pl.BlockSpec((1, d), lambda i, f, g: (0, 0))],
            out_specs=pl.BlockSpec((tm, d), lambda i, f, g: (i, 0)),
            scratch_shapes=[pltpu.VMEM((tm, d), BF16), pltpu.VMEM((tm, d), F32)]),
        out_shape=jax.ShapeDtypeStruct((m_rows, d), F32),
        compiler_params=_cparams("arbitrary", "arbitrary"),
        name="ffn",
    )(grp, x, mod, nw, w1, w3, w2, fw)


def _gdn_kernel(*refs, t_len, n_sb, has_s0):
    (q_ref, k_ref, v_ref, z_ref, ab_ref, cq_ref, ck_ref, cv_ref, alog_ref, dtb_ref, nw_ref) = refs[:11]
    rest = refs[11:]
    if has_s0:
        s0_ref, rest = rest[0], rest[1:]
    o_ref, sfin_ref, wq_buf, u_buf, attn_buf, ket_buf, gl_buf, obuf, s_ref = rest
    c_len = CHUNK
    n_chunks = t_len // c_len
    n_tot = n_sb * n_chunks
    last_row0 = n_sb * t_len - SUBLANES
    kh = pl.program_id(1)
    n_vh = 32
    tri_t = _iota((c_len, c_len), 0)
    tri_s = _iota((c_len, c_len), 1)
    lane_ab = _iota((c_len, 128), 1)
    row_abt = _iota((128, c_len), 0)
    lane_1 = _iota((1, 128), 1)
    lower = (tri_s <= tri_t).astype(F32)

    for s in range(n_sb):
        for j in range(4):
            if has_s0:
                s_ref[s * 4 + j] = s0_ref[s, j // 2, j % 2]
            else:
                s_ref[s * 4 + j] = jnp.zeros((GD_DK, GD_DV), F32)

    def conv_silu(ref, w_ref, cg):
        c = cg % n_chunks
        r0 = pl.multiple_of(cg * c_len, c_len)
        main = ref[pl.ds(r0, c_len), :]
        up0 = pl.multiple_of(jnp.maximum(r0 - SUBLANES, 0), SUBLANES)
        dn0 = pl.multiple_of(jnp.minimum(r0 + c_len, last_row0), SUBLANES)
        up = ref[pl.ds(up0, SUBLANES), :] * (c > 0).astype(F32)
        dn = ref[pl.ds(dn0, SUBLANES), :] * (c < n_chunks - 1).astype(F32)
        ext = jnp.concatenate([up, main, dn], axis=0)
        w = w_ref[...]
        acc = ext[SUBLANES - 1:SUBLANES - 1 + c_len] * w[0:1]
        for j in range(1, GD_CONV):
            acc = acc + ext[SUBLANES - 1 + j:SUBLANES - 1 + j + c_len] * w[j:j + 1]
        return _silu(acc)

    def l2n(z):
        return z * lax.rsqrt(jnp.sum(z * z, -1, keepdims=True) + EPS)

    def phase_a(cg):
        rows = pl.ds(pl.multiple_of(cg * c_len, c_len), c_len)
        q = l2n(conv_silu(q_ref, cq_ref, cg)) * (GD_DK ** -0.5)
        k = l2n(conv_silu(k_ref, ck_ref, cg))
        v2 = conv_silu(v_ref, cv_ref, cg)
        k_t = k.T
        ab = ab_ref[rows, :]
        g_all = -jnp.exp(alog_ref[...]) * _softplus(ab + dtb_ref[...])
        beta_all = jax.nn.sigmoid(ab)
        prefix = _mm(lower, g_all, "nn", P_CUM)
        gtot_all = jnp.sum(g_all, axis=0, keepdims=True)
        gc_all = jnp.where(lane_ab < 64, prefix, gtot_all - prefix + g_all)
        gct_all = gc_all.T
        g_kk = _mm(k, k_t, "nn", P_GRAM)
        g_qk = _mm(q, k_t, "nn", P_GRAM)
        per = []
        for d in range(2):
            incl = (tri_s <= tri_t) if d == 0 else (tri_s >= tri_t)
            strict = (tri_s < tri_t) if d == 0 else (tri_s > tri_t)
            for vl in range(2):
                col_g = d * 64 + 2 * kh + vl
                col_b = col_g + n_vh
                gc_col = jnp.sum(jnp.where(lane_ab == col_g, gc_all, 0.0), axis=1, keepdims=True)
                beta = jnp.sum(jnp.where(lane_ab == col_b, beta_all, 0.0), axis=1, keepdims=True)
                gc_row = jnp.sum(jnp.where(row_abt == col_g, gct_all, 0.0), axis=0, keepdims=True)
                g_last = jnp.sum(jnp.where(lane_1 == col_g, gtot_all, 0.0), axis=1, keepdims=True)
                decay = jnp.where(incl, jnp.exp(jnp.minimum(gc_col - gc_row, 0.0)), 0.0)
                a_mat = jnp.where(strict, beta * g_kk * decay, 0.0)
                per.append((gc_col, beta, gc_row, g_last, decay, a_mat))
        t_all = _unit_tri_inverse(jnp.stack([-p[5] for p in per]))
        for j, (gc_col, beta, gc_row, g_last, decay, _) in enumerate(per):
            vl = j % 2
            v = v2[:, vl * GD_DV:(vl + 1) * GD_DV]
            e_gc = jnp.exp(gc_col)
            uw = _mm(t_all[j], jnp.concatenate([v * beta, k * (beta * e_gc)], axis=1), "nn", P_APPLY)
            idx = cg * 4 + j
            u_buf[idx] = uw[:, :GD_DV]
            wq_buf[idx, 0:c_len, :] = uw[:, GD_DV:].astype(BF16)
            wq_buf[idx, c_len:, :] = (q * e_gc).astype(BF16)
            attn_buf[idx] = (g_qk * decay).astype(BF16)
            ket_buf[idx] = (k_t * jnp.exp(g_last - gc_row)).astype(BF16)
            gl_buf[idx] = jnp.broadcast_to(jnp.exp(g_last), (SUBLANES, GD_DV))

    def a_body(i, carry):
        phase_a(2 * i)
        phase_a(2 * i + 1)
        return carry

    lax.fori_loop(0, n_tot // 2, a_body, 0)

    def b_body(i, carry):
        for s in range(n_sb):
            for d in range(2):
                cg = s * n_chunks + (i if d == 0 else n_chunks - 1 - i)
                rows = pl.ds(pl.multiple_of(cg * c_len, c_len), c_len)
                for vl in range(2):
                    j = 2 * d + vl
                    idx = cg * 4 + j
                    s_mat = s_ref[s * 4 + j]
                    ws_qs = _mm(wq_buf[idx], s_mat, "nn", P_APPLY)
                    v_new = (u_buf[idx] - ws_qs[0:c_len]).astype(BF16)
                    o = ws_qs[c_len:] + _mm(attn_buf[idx], v_new, "nn", P_APPLY)
                    s_ref[s * 4 + j] = s_mat * gl_buf[idx, 0:1, :] + _mm(ket_buf[idx], v_new, "nn", P_APPLY)
                    obuf[d, rows, pl.ds(vl * GD_DV, GD_DV)] = o
        return carry

    lax.fori_loop(0, n_chunks, b_body, 0)

    def c_body(cg, carry):
        rows = pl.ds(pl.multiple_of(cg * c_len, c_len), c_len)
        for vl in range(2):
            cols = pl.ds(vl * GD_DV, GD_DV)
            o = obuf[0, rows, cols] + obuf[1, rows, cols]
            o = o * lax.rsqrt(jnp.mean(o * o, -1, keepdims=True) + EPS) * nw_ref[...] * _silu(z_ref[rows, cols])
            o_ref[rows, cols] = o.astype(o_ref.dtype)
        return carry

    lax.fori_loop(0, n_tot, c_body, 0)
    for s in range(n_sb):
        for j in range(4):
            sfin_ref[s, j // 2, j % 2] = s_ref[s * 4 + j]


def _gdn_scan(proj, ab, conv_w, alog_row, dtb_row, nw, s0, row0, n_seq, t_len, n_sb):
    n_kh = 16
    blk_rows = n_sb * t_len
    assert row0 % blk_rows == 0 and n_seq % n_sb == 0 and (n_sb * t_len // CHUNK) % 2 == 0
    blk0 = row0 // blk_rows
    vw = 2 * GD_DV
    n_units = n_sb * (t_len // CHUNK) * 4
    in_specs = [pl.BlockSpec((blk_rows, GD_DK), lambda b, h: (blk0 + b, h)),
                pl.BlockSpec((blk_rows, GD_DK), lambda b, h: (blk0 + b, n_kh + h)),
                pl.BlockSpec((blk_rows, vw), lambda b, h: (blk0 + b, n_kh + h)),
                pl.BlockSpec((blk_rows, vw), lambda b, h: (blk0 + b, 2 * n_kh + h)),
                pl.BlockSpec((blk_rows, 128), lambda b, h: (blk0 + b, 0)),
                pl.BlockSpec((GD_CONV, GD_DK), lambda b, h: (0, h)),
                pl.BlockSpec((GD_CONV, GD_DK), lambda b, h: (0, n_kh + h)),
                pl.BlockSpec((GD_CONV, vw), lambda b, h: (0, n_kh + h)),
                pl.BlockSpec((1, 128), lambda b, h: (0, 0)),
                pl.BlockSpec((1, 128), lambda b, h: (0, 0)),
                pl.BlockSpec((1, GD_DV), lambda b, h: (0, 0))]
    args = [proj, proj, proj, proj, ab, conv_w, conv_w, conv_w, alog_row, dtb_row, nw]
    if s0 is not None:
        in_specs.append(pl.BlockSpec((n_sb, 2, 2, GD_DK, GD_DV), lambda b, h: (b, 0, h, 0, 0)))
        args.append(s0)
    o, sfin = pl.pallas_call(
        functools.partial(_gdn_kernel, t_len=t_len, n_sb=n_sb, has_s0=s0 is not None),
        grid=(n_seq // n_sb, n_kh),
        in_specs=in_specs,
        out_specs=[pl.BlockSpec((blk_rows, vw), lambda b, h: (b, h)),
                   pl.BlockSpec((n_sb, 2, 2, GD_DK, GD_DV), lambda b, h: (b, 0, h, 0, 0))],
        out_shape=[jax.ShapeDtypeStruct((n_seq * t_len, 2 * n_kh * GD_DV), BF16),
                   jax.ShapeDtypeStruct((n_seq, 2, 2 * n_kh, GD_DK, GD_DV), F32)],
        scratch_shapes=[pltpu.VMEM((n_units, 2 * CHUNK, GD_DV), BF16),
                        pltpu.VMEM((n_units, CHUNK, GD_DV), F32),
                        pltpu.VMEM((n_units, CHUNK, CHUNK), BF16),
                        pltpu.VMEM((n_units, GD_DK, CHUNK), BF16),
                        pltpu.VMEM((n_units, SUBLANES, GD_DV), F32),
                        pltpu.VMEM((2, blk_rows, vw), F32),
                        pltpu.VMEM((4 * n_sb, GD_DK, GD_DV), F32)],
        compiler_params=_cparams("arbitrary", "arbitrary"),
        name=f"gdn_scan_t{t_len}",
    )(*args)
    return o, sfin


def _grid_pos_embed(n_tokens, d_model):
    rows = n_tokens // GRID_W
    row = jnp.broadcast_to(jnp.arange(rows, dtype=F32)[:, None], (rows, GRID_W)).reshape(-1)
    col = jnp.broadcast_to(jnp.arange(GRID_W, dtype=F32)[None, :], (rows, GRID_W)).reshape(-1)
    quarter = d_model // 4
    omega = 1.0 / (POS_BASE ** (jnp.arange(quarter, dtype=F32) / quarter))
    ar = row[:, None] * omega
    ac = col[:, None] * omega
    return jnp.concatenate([jnp.sin(ar), jnp.cos(ar), jnp.sin(ac), jnp.cos(ac)], -1)


def _pad_axis(a, axis, size):
    pad = [(0, 0)] * a.ndim
    pad[axis] = (0, size - a.shape[axis])
    return jnp.pad(a, pad)


def kernel(x_prompt, x_sample, state_rwkv, state_gdn, c, c_ctx, norm_mix, norm_ffn, norm_final, w_mod, b_mod, ffn_w1, ffn_w3, ffn_w2, rw_mu, rw_wr, rw_wk, rw_wv, rw_wo, rw_w0, rw_w1, rw_w2, rw_a0, rw_a1, rw_a2, rw_g1, rw_g2, rw_kk, rw_ka, rw_rk, rw_ln_w, rw_ln_b, gd_w_in, gd_conv, gd_a_log, gd_dt_bias, gd_norm, gd_w_out):
    n_p, t_p, d = x_prompt.shape
    n_s, t_s, _ = x_sample.shape
    rows_p = n_p * t_p
    bf = lambda a: a.astype(BF16)

    x = jnp.concatenate([x_prompt.reshape(rows_p, d),
                         (x_sample + _grid_pos_embed(t_s, d)[None]).reshape(n_s * t_s, d)], axis=0)
    regions = [(n_p, t_p, [0] * n_p), (n_s, t_s, list(range(1, n_s + 1)))]
    tm_shift = 256
    tm_big = 512
    tables = _tile_tables(regions, tm_shift)
    grp_big = _tile_tables(regions, tm_big)[0]

    cond = _pad_axis(jnp.concatenate([c_ctx[None], c], axis=0), 0, SUBLANES)
    mod = _modulation(cond, w_mod, b_mod).reshape(w_mod.shape[0], SUBLANES, 6, d)

    row = lambda a: a.reshape(1, -1)
    w_rkv = bf(jnp.stack([rw_wr[0], rw_wk[0], rw_wv[0]]))
    lp = RW_LORA_PAD
    rkv, gate, lw, a = _rwkv_inputs(
        x, mod[0], tables, row(norm_mix[0]), rw_mu[0], w_rkv,
        bf(rw_g1[0]), bf(rw_g2[0]),
        bf(_pad_axis(rw_w1[0], 2, lp)), bf(_pad_axis(rw_w2[0], 1, lp)),
        bf(_pad_axis(rw_a1[0], 2, lp)), bf(_pad_axis(rw_a2[0], 1, lp)),
        rw_w0[0][:, None, :], rw_a0[0][:, None, :], tm_shift)
    scan_args = (rkv, gate, lw, a, row(rw_kk[0]), row(rw_ka[0]), row(rw_rk[0]), row(rw_ln_w[0]), row(rw_ln_b[0]))
    y_p, s_rwkv = _rwkv_scan(*scan_args, None, 0, n_p, t_p, 2)
    s0_rw = state_rwkv[:, 0].reshape(n_s, 2, d, RW_HEAD)
    y_s, _ = _rwkv_scan(*scan_args, s0_rw, rows_p, n_s, t_s, 1)
    y = jnp.concatenate([y_p, y_s], axis=0)
    x = _out_proj(y, bf(rw_wo[0]), x, mod[0], grp_big, 2, tm_big)
    x = _ffn(x, mod[0], grp_big, row(norm_ffn[0]), bf(ffn_w1[0]), bf(ffn_w3[0]), bf(ffn_w2[0]),
             row(norm_final), False, tm_big)

    n_main = 3 * 4096
    proj = _in_proj(x, mod[1], grp_big, row(norm_mix[1]), bf(gd_w_in[0][:, :n_main]), tm_big, 1024)
    ab = _in_proj(x, mod[1], grp_big, row(norm_mix[1]), bf(gd_w_in[0][:, n_main:]), tm_big, 128)
    zeros32 = jnp.zeros((2, 32), F32)
    alog_row = jnp.stack([gd_a_log[0], zeros32], axis=1).reshape(1, 128)
    dtb_row = jnp.stack([gd_dt_bias[0], zeros32], axis=1).reshape(1, 128)
    gd_args = (proj, ab, gd_conv[0], alog_row, dtb_row, row(gd_norm[0]))
    o_p, s_gdn = _gdn_scan(*gd_args, None, 0, n_p, t_p, 2)
    o_s, _ = _gdn_scan(*gd_args, state_gdn[:, 0], rows_p, n_s, t_s, 1)
    o = jnp.concatenate([o_p, o_s], axis=0)
    x = _out_proj(o, bf(gd_w_out[0]), x, mod[1], grp_big, 2, tm_big)
    x = _ffn(x, mod[1], grp_big, row(norm_ffn[1]), bf(ffn_w1[1]), bf(ffn_w3[1]), bf(ffn_w2[1]),
             row(norm_final), True, tm_big)

    y_prompt = x[:rows_p].reshape(n_p, t_p, d)
    y_sample = x[rows_p:].reshape(n_s, t_s, d)
    new_state_rwkv = s_rwkv.reshape(n_p, 1, 2, d // RW_HEAD, RW_HEAD, RW_HEAD)
    new_state_gdn = s_gdn.reshape(n_p, 1, 2, 32, GD_DK, GD_DV)
    return (y_prompt, y_sample, new_state_rwkv, new_state_gdn)
```

```python
import functools

import numpy as np
import jax
import jax.numpy as jnp
from jax import lax
from jax.experimental import pallas as pl
from jax.experimental.pallas import tpu as pltpu

F32 = jnp.float32
BF16 = jnp.bfloat16

EPS = 1e-6
GRID_W = 64
POS_BASE = 10000.0
RW_HEAD = 64
RW_GROUP_HEADS = 4
RW_LANES = RW_HEAD * RW_GROUP_HEADS
RW_LORA_PAD = 128
GD_DK = 128
GD_DV = 128
GD_CONV = 4
CHUNK = 64
RW_SUPER_CHUNKS = 8
SUBLANES = 8
VMEM_LIMIT = 56 * 1024 * 1024


def _cparams(*sem):
    return pltpu.CompilerParams(dimension_semantics=sem, vmem_limit_bytes=VMEM_LIMIT)


def _dot(a, b):
    return jnp.dot(a.astype(BF16), b.astype(BF16), preferred_element_type=F32)


_DIMS = {"nn": (((1,), (0,)), ((), ())),
         "nt": (((1,), (1,)), ((), ())),
         "tn": (((0,), (0,)), ((), ())),
         "bnn": (((2,), (1,)), ((0,), (0,)))}


def _split_bf16(x, pieces):
    out = []
    for _ in range(pieces - 1):
        p = x.astype(BF16)
        out.append(p)
        x = x - p.astype(F32)
    out.append(x.astype(BF16))
    return out


def _mm(a, b, dims="nn", mode="b"):
    dn = _DIMS[dims]
    dg = lambda x, y: lax.dot_general(x, y, dn, preferred_element_type=F32)
    if mode == "b":
        return dg(a.astype(BF16), b.astype(BF16))
    if mode == "la":
        a0 = a.astype(BF16)
        return sum(dg(a0, p) for p in _split_bf16(b, 3))
    if mode == "ra":
        b0 = b.astype(BF16)
        return sum(dg(p, b0) for p in _split_bf16(a, 3))
    raise ValueError(mode)


P_INV = "b"
P_GRAM = "b"
P_APPLY = "b"
P_SUM = "ra"
P_CUM = "la"


def _iota(shape, dim):
    return lax.broadcasted_iota(jnp.int32, shape, dim)


def _silu(x):
    return x * jax.nn.sigmoid(x)


def _softplus(x):
    return jnp.maximum(x, 0.0) + jnp.log(1.0 + jnp.exp(-jnp.abs(x)))


def _rms_mod(x, nw, sh, sc):
    y = x * lax.rsqrt(jnp.mean(x * x, -1, keepdims=True) + EPS)
    return (y * nw) * (1.0 + sc) + sh


def _unit_tri_inverse(n_mat):
    c_len = n_mat.shape[-1]
    ij = _iota(n_mat.shape, 1) ^ _iota(n_mat.shape, 2)
    d_mat = jnp.where(ij == 0, 1.0, jnp.where(ij == 1, n_mat, 0.0))
    half = 2
    while half < c_len:
        l_mat = jnp.where((ij >= half) & (ij < 2 * half), n_mat, 0.0)
        d_mat = d_mat + _mm(_mm(d_mat, l_mat, "bnn", P_INV), d_mat, "bnn", P_INV)
        half *= 2
    return d_mat


def _mod_kernel(c_ref, w_ref, b_ref, o_ref):
    o_ref[0] = _dot(_silu(c_ref[...]), w_ref[0]) + b_ref[0]


def _modulation(cond, w_mod, b_mod):
    depth, d, n = w_mod.shape
    tn = 1024
    return pl.pallas_call(
        _mod_kernel,
        grid=(depth, n // tn),
        in_specs=[pl.BlockSpec((SUBLANES, d), lambda l, j: (0, 0)),
                  pl.BlockSpec((1, d, tn), lambda l, j: (l, 0, j)),
                  pl.BlockSpec((1, 1, tn), lambda l, j: (l, 0, j))],
        out_specs=pl.BlockSpec((1, SUBLANES, tn), lambda l, j: (l, 0, j)),
        out_shape=jax.ShapeDtypeStruct((depth, SUBLANES, n), F32),
        compiler_params=_cparams("arbitrary", "arbitrary"),
        name="modulation",
    )(cond, w_mod, b_mod.reshape(depth, 1, n))


def _shift_mix(x_ref, xp_ref, xn_ref, mod_ref, nw_ref, has_prev, has_next):
    nw = nw_ref[...]
    sh = mod_ref[0, 0:1, :]
    sc = mod_ref[0, 1:2, :]
    h = _rms_mod(x_ref[...], nw, sh, sc)
    tm = h.shape[0]
    hp = _rms_mod(xp_ref[SUBLANES - 1:SUBLANES, :], nw, sh, sc) * has_prev
    hn = _rms_mod(xn_ref[0:1, :], nw, sh, sc) * has_next
    rows = _iota(h.shape, 0)
    prev = jnp.where(rows == 0, hp, pltpu.roll(h, 1, 0))
    nxt = jnp.where(rows == tm - 1, hn, pltpu.roll(h, tm - 1, 0))
    return h, 0.5 * (prev + nxt) - h


def _rkv_kernel(grp_ref, hp_ref, hn_ref, x_ref, xp_ref, xn_ref, mod_ref, nw_ref, mu_ref, w_ref, o_ref):
    i = pl.program_id(1)
    h, xx = _shift_mix(x_ref, xp_ref, xn_ref, mod_ref, nw_ref,
                       hp_ref[i].astype(F32), hn_ref[i].astype(F32))
    o_ref[0] = _dot(h + xx * mu_ref[0], w_ref[0])


def _lora_kernel(grp_ref, hp_ref, hn_ref, x_ref, xp_ref, xn_ref, mod_ref, nw_ref, mu_ref,
                 g1_ref, g2_ref, w1_ref, w2_ref, a1_ref, a2_ref, w0_ref, a0_ref,
                 gate_ref, lw_ref, a_ref):
    i = pl.program_id(0)
    h, xx = _shift_mix(x_ref, xp_ref, xn_ref, mod_ref, nw_ref,
                       hp_ref[i].astype(F32), hn_ref[i].astype(F32))
    xw = h + xx * mu_ref[1:2, :]
    xa = h + xx * mu_ref[4:5, :]
    xg = h + xx * mu_ref[5:6, :]
    gate_ref[...] = _dot(jax.nn.sigmoid(_dot(xg, g1_ref[...])), g2_ref[...])
    for d in range(2):
        lw = jnp.tanh(_dot(xw, w1_ref[d]))
        w_log = -_softplus(-(w0_ref[d] + _dot(lw, w2_ref[d]))) - 0.5
        lw_ref[d] = -jnp.exp(w_log)
        a_ref[d] = jax.nn.sigmoid(a0_ref[d] + _dot(_dot(xa, a1_ref[d]), a2_ref[d]))


def _tile_tables(regions, tm):
    row_grp, row_pos, row_len = [], [], []
    for n_seq, t_len, groups in regions:
        for s in range(n_seq):
            row_grp += [groups[s]] * t_len
            row_pos += list(range(t_len))
            row_len += [t_len] * t_len
    row_grp, row_pos, row_len = (np.asarray(v).reshape(-1, tm) for v in (row_grp, row_pos, row_len))
    assert (row_grp == row_grp[:, :1]).all(), "a row tile must not straddle modulation groups"
    as_i32 = lambda v: jnp.asarray(np.asarray(v, np.int32))
    return (as_i32(row_grp[:, 0]), as_i32(row_pos[:, 0] > 0),
            as_i32(row_pos[:, -1] < row_len[:, -1] - 1))


def _halo_specs(tm, d, m_rows, n_lead):
    blocks = tm // SUBLANES
    last = m_rows // SUBLANES - 1
    if n_lead == 1:
        cur = lambda j, i, *_: (i, 0)
        prev = lambda j, i, *_: (jnp.maximum(i * blocks - 1, 0), 0)
        nxt = lambda j, i, *_: (jnp.minimum((i + 1) * blocks, last), 0)
    else:
        cur = lambda i, *_: (i, 0)
        prev = lambda i, *_: (jnp.maximum(i * blocks - 1, 0), 0)
        nxt = lambda i, *_: (jnp.minimum((i + 1) * blocks, last), 0)
    return [pl.BlockSpec((tm, d), cur), pl.BlockSpec((SUBLANES, d), prev), pl.BlockSpec((SUBLANES, d), nxt)]


def _rwkv_inputs(x, mod, tables, nw, mu, w_rkv, g1, g2, w1, w2, a1, a2, w0, a0, tm):
    m_rows, d = x.shape
    n_tiles = m_rows // tm
    mu_rkv = jnp.stack([mu[0], mu[2], mu[3]])[:, None, :]
    rkv = pl.pallas_call(
        _rkv_kernel,
        grid_spec=pltpu.PrefetchScalarGridSpec(
            num_scalar_prefetch=3, grid=(3, n_tiles),
            in_specs=_halo_specs(tm, d, m_rows, 1) + [
                pl.BlockSpec((1, 6, d), lambda j, i, g, *_: (g[i], 0, 0)),
                pl.BlockSpec((1, d), lambda j, i, *_: (0, 0)),
                pl.BlockSpec((1, 1, d), lambda j, i, *_: (j, 0, 0)),
                pl.BlockSpec((1, d, d), lambda j, i, *_: (j, 0, 0))],
            out_specs=pl.BlockSpec((1, tm, d), lambda j, i, *_: (j, i, 0))),
        out_shape=jax.ShapeDtypeStruct((3, m_rows, d), F32),
        compiler_params=_cparams("arbitrary", "arbitrary"),
        name="rwkv_rkv_proj",
    )(*tables, x, x, x, mod, nw, mu_rkv, w_rkv)

    full = lambda a: pl.BlockSpec(a.shape, lambda i, *_: (0,) * a.ndim)
    small = [g1, g2, w1, w2, a1, a2, w0, a0]
    gate, lw, a = pl.pallas_call(
        _lora_kernel,
        grid_spec=pltpu.PrefetchScalarGridSpec(
            num_scalar_prefetch=3, grid=(n_tiles,),
            in_specs=_halo_specs(tm, d, m_rows, 0) + [
                pl.BlockSpec((1, 6, d), lambda i, g, *_: (g[i], 0, 0)),
                pl.BlockSpec((1, d), lambda i, *_: (0, 0)),
                full(mu)] + [full(s) for s in small],
            out_specs=[pl.BlockSpec((tm, d), lambda i, *_: (i, 0)),
                       pl.BlockSpec((2, tm, d), lambda i, *_: (0, i, 0)),
                       pl.BlockSpec((2, tm, d), lambda i, *_: (0, i, 0))]),
        out_shape=[jax.ShapeDtypeStruct((m_rows, d), F32),
                   jax.ShapeDtypeStruct((2, m_rows, d), F32),
                   jax.ShapeDtypeStruct((2, m_rows, d), F32)],
        compiler_params=_cparams("arbitrary"),
        name="rwkv_lora_proj",
    )(*tables, x, x, x, mod, nw, mu, *small)
    return rkv, gate, lw, a


def _rwkv_scan_kernel(*refs, t_len, n_sb, has_s0):
    (r_ref, k_ref, v_ref, gate_ref, lw_ref, a_ref, kk_ref, ka_ref, rk_ref, lnw_ref, lnb_ref) = refs[:11]
    rest = refs[11:]
    if has_s0:
        s0_ref, rest = rest[0], rest[1:]
    (y_ref, sfin_ref, r2_buf, au_buf, bw_buf, kw_buf, y0_buf, w_buf, ybuf, bonus_buf, s_ref) = rest
    c_len, lanes, heads = CHUNK, RW_LANES, RW_GROUP_HEADS
    n_chunks = t_len // c_len
    sc = min(n_chunks, RW_SUPER_CHUNKS)
    n_super = n_chunks // sc
    n_pp = max(1, 4 // (2 * n_sb))
    stack = heads * c_len

    same_head = (_iota((stack, lanes), 0) // c_len == _iota((stack, lanes), 1) // RW_HEAD).astype(F32)
    ones_blk = same_head
    rep = (_iota((RW_HEAD, lanes), 1) % RW_HEAD == _iota((RW_HEAD, lanes), 0)).astype(F32)
    t_idx = _iota((stack, c_len), 0) % c_len
    s_idx = _iota((stack, c_len), 1)
    tri_t = _iota((c_len, c_len), 0)
    tri_s = _iota((c_len, c_len), 1)

    kk_p, ka_p, rk_p = kk_ref[...], ka_ref[...], rk_ref[...]
    lnw, lnb = lnw_ref[...], lnb_ref[...]

    def fold(z):
        z = z * same_head
        return z[0:c_len] + z[c_len:2 * c_len] + z[2 * c_len:3 * c_len] + z[3 * c_len:4 * c_len]

    def tile4(z):
        return jnp.concatenate([z, z, z, z], axis=0) * same_head

    for s in range(n_sb):
        for d in range(2):
            if has_s0:
                s_ref[s * 2 + d] = _mm(s0_ref[s, d], rep, "nn", "ra") * ones_blk
            else:
                s_ref[s * 2 + d] = jnp.zeros((lanes, lanes), F32)
    ybuf[...] = jnp.zeros_like(ybuf)
    bonus_buf[...] = jnp.zeros_like(bonus_buf)

    def chunk_rows(s, d, p):
        c = p if d == 0 else n_chunks - 1 - p
        return pl.ds(pl.multiple_of((s * n_chunks + c) * c_len, c_len), c_len)

    incl_f = [(tri_s <= tri_t).astype(F32), (tri_s >= tri_t).astype(F32)]
    incl4 = [s_idx <= t_idx, s_idx >= t_idx]
    strict4 = [s_idx < t_idx, s_idx > t_idx]

    def phase_a(p0, j0):
        units = [(s, d, q) for s in range(n_sb) for d in range(2) for q in range(n_pp)]
        dirs = [d for _, d, _ in units]
        rows = [chunk_rows(s, d, p0 + q) for s, d, q in units]
        r = [r_ref[rw, :] for rw in rows]
        k = [k_ref[rw, :] for rw in rows]
        v = [v_ref[rw, :] for rw in rows]
        lw = [lw_ref[d, rw, :] for d, rw in zip(dirs, rows)]
        a = [a_ref[d, rw, :] for d, rw in zip(dirs, rows)]
        bonus_old = [bonus_buf[rw, :] for rw in rows]
        kx = [ki * kk_p for ki in k]
        ss = [_mm(x * x, ones_blk, "nn", P_SUM) for x in kx]
        cw = [_mm(incl_f[d], x, "nn", P_CUM) for d, x in zip(dirs, lw)]
        bsum = [_mm(ri * ki * (1.0 + (ai - 1.0) * ka_p) * rk_p, ones_blk, "nn", P_SUM)
                for ri, ki, ai in zip(r, k, a)]
        kkn = [x * lax.rsqrt(y + EPS) for x, y in zip(kx, ss)]
        kd = [ki * (1.0 + (ai - 1.0) * ka_p) for ki, ai in zip(k, a)]
        w_row = [jnp.exp(jnp.sum(x, axis=0, keepdims=True)) for x in lw]
        e_neg = [jnp.exp(-x) for x in cw]
        at = [-kn * jnp.exp(c - l) for kn, c, l in zip(kkn, cw, lw)]
        bt = [kn * ai * e for kn, ai, e in zip(kkn, a, e_neg)]
        kt = [x * e for x, e in zip(kd, e_neg)]
        rt = [ri * jnp.exp(c) for ri, c in zip(r, cw)]
        lhs = [jnp.concatenate([tile4(x), tile4(y)], axis=0) for x, y in zip(at, rt)]
        g_b = [_mm(x, y, "nt", P_GRAM) for x, y in zip(lhs, bt)]
        g_k = [_mm(x, y, "nt", P_GRAM) for x, y in zip(lhs, kt)]
        n_s = [jnp.where(strict4[d], g[0:stack], 0.0) for d, g in zip(dirs, g_b)]
        ak_s = [jnp.where(strict4[d], g[0:stack], 0.0) for d, g in zip(dirs, g_k)]
        rb_s = [jnp.where(incl4[d], g[stack:], 0.0) for d, g in zip(dirs, g_b)]
        rk_s = [jnp.where(incl4[d], g[stack:], 0.0) for d, g in zip(dirs, g_k)]
        t_all = _unit_tri_inverse(jnp.concatenate([x.reshape(heads, c_len, c_len) for x in n_s], axis=0))
        t_s = [t_all[i * heads:(i + 1) * heads].reshape(stack, c_len) for i in range(len(units))]
        a2 = [fold(_mm(t, x, "nn", P_APPLY)) for t, x in zip(t_s, at)]
        akv = [fold(_mm(x, vi, "nn", P_APPLY)) for x, vi in zip(ak_s, v)]
        rkv = [_mm(x, vi, "nn", P_APPLY) for x, vi in zip(rk_s, v)]
        u0 = [fold(_mm(t, x, "nn", P_APPLY)) for t, x in zip(t_s, akv)]
        r2 = [x + fold(_mm(rb, y, "nn", P_APPLY)) for x, rb, y in zip(rt, rb_s, a2)]
        y0 = [fold(_mm(rb, x, "nn", P_APPLY) + y) for rb, x, y in zip(rb_s, u0, rkv)]
        for i, (s, d, q) in enumerate(units):
            slot = (s * 2 + d) * sc + j0 + q
            r2_buf[slot] = r2[i].astype(BF16)
            au_buf[slot, :, 0:lanes] = a2[i].astype(BF16)
            au_buf[slot, :, lanes:] = u0[i].astype(BF16)
            bw_buf[slot] = (bt[i] * w_row[i]).astype(BF16)
            kw_buf[slot] = (kt[i] * w_row[i]).astype(BF16)
            y0_buf[slot] = y0[i]
            w_buf[slot] = jnp.broadcast_to(w_row[i], (SUBLANES, lanes))
            bonus_buf[rows[i], :] = bonus_old[i] + bsum[i] * v[i]

    def phase_b(p, j):
        units = [(s, d) for s in range(n_sb) for d in range(2)]
        rows = [chunk_rows(s, d, p) for s, d in units]
        slots = [(s * 2 + d) * sc + j for s, d in units]
        s_mat = [s_ref[s * 2 + d] for s, d in units]
        y_old = [ybuf[rw, :] for rw in rows]
        vv = [v_ref[rw, :] for rw in rows]
        s_bf = [x.astype(BF16) for x in s_mat]
        au_bw = [_mm(au_buf[sl], bw_buf[sl], "tn", P_APPLY) for sl in slots]
        vk = [_mm(x, kw_buf[sl], "tn", P_APPLY) for x, sl in zip(vv, slots)]
        y = [_mm(r2_buf[sl], x, "nt", P_APPLY) + y0_buf[sl] for sl, x in zip(slots, s_bf)]
        s_new = [x * w_buf[sl, 0:1, :] + _mm(xb, m[0:lanes] * ones_blk, "nn", P_APPLY) + (m[lanes:] + z) * ones_blk
                 for x, xb, m, z, sl in zip(s_mat, s_bf, au_bw, vk, slots)]
        for i, (s, d) in enumerate(units):
            ybuf[rows[i], :] = y_old[i] + y[i]
            s_ref[s * 2 + d] = s_new[i]

    def super_body(sp, carry):
        def a_body(j, c2):
            phase_a(sp * sc + j * n_pp, j * n_pp)
            return c2

        def b_body(j, c2):
            phase_b(sp * sc + j, j)
            return c2

        lax.fori_loop(0, sc // n_pp, a_body, 0)
        lax.fori_loop(0, sc, b_body, 0)
        return carry

    lax.fori_loop(0, n_super, super_body, 0)

    n_cc = 4

    def c_body(i, carry):
        rows = [pl.ds(pl.multiple_of((i * n_cc + q) * c_len, c_len), c_len) for q in range(n_cc)]
        y = [ybuf[rw, :] for rw in rows]
        mean = [_mm(x, ones_blk, "nn", P_SUM) * (1.0 / RW_HEAD) for x in y]
        yc = [x - m for x, m in zip(y, mean)]
        var = [_mm(x * x, ones_blk, "nn", P_SUM) * (1.0 / RW_HEAD) for x in yc]
        for rw, x, vr in zip(rows, yc, var):
            yn = x * lax.rsqrt(vr + RW_HEAD * 1e-5) * lnw + lnb
            y_ref[rw, :] = ((yn + bonus_buf[rw, :]) * gate_ref[rw, :]).astype(y_ref.dtype)
        return carry

    lax.fori_loop(0, n_sb * n_chunks // n_cc, c_body, 0)
    rep_t = (_iota((lanes, RW_HEAD), 0) % RW_HEAD == _iota((lanes, RW_HEAD), 1)).astype(F32)
    for s in range(n_sb):
        for d in range(2):
            sfin_ref[s, d] = _mm(s_ref[s * 2 + d], rep_t, "nn", "ra")


def _rwkv_scan(rkv, gate, lw, a, kk, ka, rk, lnw, lnb, s0, row0, n_seq, t_len, n_sb):
    _, m_rows, d = rkv.shape
    n_groups = d // RW_LANES
    blk_rows = n_sb * t_len
    assert row0 % blk_rows == 0 and n_seq % n_sb == 0
    blk0 = row0 // blk_rows
    sc = min(t_len // CHUNK, RW_SUPER_CHUNKS)
    assert (t_len // CHUNK) % sc == 0
    n_slots = n_sb * 2 * sc
    tok = lambda b, g: (blk0 + b, g)
    in_specs = [pl.BlockSpec((None, blk_rows, RW_LANES), lambda b, g, j=j: (j, blk0 + b, g)) for j in range(3)]
    in_specs += [pl.BlockSpec((blk_rows, RW_LANES), tok),
                 pl.BlockSpec((2, blk_rows, RW_LANES), lambda b, g: (0, blk0 + b, g)),
                 pl.BlockSpec((2, blk_rows, RW_LANES), lambda b, g: (0, blk0 + b, g))]
    in_specs += [pl.BlockSpec((1, RW_LANES), lambda b, g: (0, g))] * 5
    args = [rkv, rkv, rkv, gate, lw, a, kk, ka, rk, lnw, lnb]
    if s0 is not None:
        in_specs.append(pl.BlockSpec((n_sb, 2, RW_LANES, RW_HEAD), lambda b, g: (b, 0, g, 0)))
        args.append(s0)
    y, sfin = pl.pallas_call(
        functools.partial(_rwkv_scan_kernel, t_len=t_len, n_sb=n_sb, has_s0=s0 is not None),
        grid=(n_seq // n_sb, n_groups),
        in_specs=in_specs,
        out_specs=[pl.BlockSpec((blk_rows, RW_LANES), lambda b, g: (b, g)),
                   pl.BlockSpec((n_sb, 2, RW_LANES, RW_HEAD), lambda b, g: (b, 0, g, 0))],
        out_shape=[jax.ShapeDtypeStruct((n_seq * t_len, d), BF16),
                   jax.ShapeDtypeStruct((n_seq, 2, d, RW_HEAD), F32)],
        scratch_shapes=[pltpu.VMEM((n_slots, CHUNK, RW_LANES), BF16),
                        pltpu.VMEM((n_slots, CHUNK, 2 * RW_LANES), BF16),
                        pltpu.VMEM((n_slots, CHUNK, RW_LANES), BF16),
                        pltpu.VMEM((n_slots, CHUNK, RW_LANES), BF16),
                        pltpu.VMEM((n_slots, CHUNK, RW_LANES), F32),
                        pltpu.VMEM((n_slots, SUBLANES, RW_LANES), F32),
                        pltpu.VMEM((blk_rows, RW_LANES), F32),
                        pltpu.VMEM((blk_rows, RW_LANES), F32),
                        pltpu.VMEM((2 * n_sb, RW_LANES, RW_LANES), F32)],
        compiler_params=_cparams("arbitrary", "arbitrary"),
        name=f"rwkv_scan_t{t_len}",
    )(*args)
    return y, sfin


def _out_proj_kernel(grp_ref, y_ref, w_ref, x_ref, mod_ref, o_ref, *, gate_row):
    o_ref[...] = x_ref[...] + mod_ref[0, gate_row:gate_row + 1, :] * _dot(y_ref[...], w_ref[...])


def _out_proj(y, w, x, mod, grp, gate_row, tm):
    m_rows, k_dim = y.shape
    d = w.shape[1]
    tn = 1024
    return pl.pallas_call(
        functools.partial(_out_proj_kernel, gate_row=gate_row),
        grid_spec=pltpu.PrefetchScalarGridSpec(
            num_scalar_prefetch=1, grid=(d // tn, m_rows // tm),
            in_specs=[pl.BlockSpec((tm, k_dim), lambda n, i, g: (i, 0)),
                      pl.BlockSpec((k_dim, tn), lambda n, i, g: (0, n)),
                      pl.BlockSpec((tm, tn), lambda n, i, g: (i, n)),
                      pl.BlockSpec((1, 6, tn), lambda n, i, g: (g[i], 0, n))],
            out_specs=pl.BlockSpec((tm, tn), lambda n, i, g: (i, n))),
        out_shape=jax.ShapeDtypeStruct((m_rows, d), F32),
        compiler_params=_cparams("arbitrary", "arbitrary"),
        name="out_proj",
    )(grp, y, w, x, mod)


def _in_proj_kernel(grp_ref, x_ref, mod_ref, nw_ref, w_ref, o_ref):
    h = _rms_mod(x_ref[...], nw_ref[...], mod_ref[0, 0:1, :], mod_ref[0, 1:2, :])
    o_ref[...] = _dot(h, w_ref[...])


def _in_proj(x, mod, grp, nw, w, tm, tn):
    m_rows, d = x.shape
    n = w.shape[1]
    return pl.pallas_call(
        _in_proj_kernel,
        grid_spec=pltpu.PrefetchScalarGridSpec(
            num_scalar_prefetch=1, grid=(n // tn, m_rows // tm),
            in_specs=[pl.BlockSpec((tm, d), lambda n_, i, g: (i, 0)),
                      pl.BlockSpec((1, 6, d), lambda n_, i, g: (g[i], 0, 0)),
                      pl.BlockSpec((1, d), lambda n_, i, g: (0, 0)),
                      pl.BlockSpec((d, tn), lambda n_, i, g: (0, n_))],
            out_specs=pl.BlockSpec((tm, tn), lambda n_, i, g: (i, n_))),
        out_shape=jax.ShapeDtypeStruct((m_rows, n), F32),
        compiler_params=_cparams("arbitrary", "arbitrary"),
        name="gdn_in_proj",
    )(grp, x, mod, nw, w)


def _ffn_kernel(grp_ref, x_ref, mod_ref, nw_ref, w1_ref, w3_ref, w2_ref, fw_ref, o_ref, h_buf, acc,
                *, final_norm):
    f = pl.program_id(1)

    @pl.when(f == 0)
    def _():
        h_buf[...] = _rms_mod(x_ref[...], nw_ref[...], mod_ref[0, 3:4, :], mod_ref[0, 4:5, :]).astype(BF16)
        acc[...] = jnp.zeros_like(acc)

    h = h_buf[...]
    gate = jnp.dot(h, w1_ref[...], preferred_element_type=F32)
    up = jnp.dot(h, w3_ref[...], preferred_element_type=F32)
    acc[...] += _dot(_silu(gate) * up, w2_ref[...])

    @pl.when(f == pl.num_programs(1) - 1)
    def _():
        y = x_ref[...] + mod_ref[0, 5:6, :] * acc[...]
        if final_norm:
            y = y * lax.rsqrt(jnp.mean(y * y, -1, keepdims=True) + EPS) * fw_ref[...]
        o_ref[...] = y


def _ffn(x, mod, grp, nw, w1, w3, w2, fw, final_norm, tm):
    m_rows, d = x.shape
    d_ff = w1.shape[1]
    tf = 512
    return pl.pallas_call(
        functools.partial(_ffn_kernel, final_norm=final_norm),
        grid_spec=pltpu.PrefetchScalarGridSpec(
            num_scalar_prefetch=1, grid=(m_rows // tm, d_ff // tf),
            in_specs=[pl.BlockSpec((tm, d), lambda i, f, g: (i, 0)),
                      pl.BlockSpec((1, 6, d), lambda i, f, g: (g[i], 0, 0)),
                      pl.BlockSpec((1, d), lambda i, f, g: (0, 0)),
                      pl.BlockSpec((d, tf), lambda i, f, g: (0, f)),
                      pl.BlockSpec((d, tf), lambda i, f, g: (0, f)),
                      pl.BlockSpec((tf, d), lambda i, f, g: (f, 0)),
                      pl.BlockSpec((1, d), lambda i, f, g: (0, 0))],
            out_specs=pl.BlockSpec((tm, d), lambda i, f, g: (i, 0)),
            scratch_shapes=[pltpu.VMEM((tm, d), BF16), pltpu.VMEM((tm, d), F32)]),
        out_shape=jax.ShapeDtypeStruct((m_rows, d), F32),
        compiler_params=_cparams("arbitrary", "arbitrary"),
        name="ffn",
    )(grp, x, mod, nw, w1, w3, w2, fw)


def _gdn_kernel(*refs, t_len, n_sb, has_s0):
    (q_ref, k_ref, v_ref, z_ref, ab_ref, cq_ref, ck_ref, cv_ref, alog_ref, dtb_ref, nw_ref) = refs[:11]
    rest = refs[11:]
    if has_s0:
        s0_ref, rest = rest[0], rest[1:]
    o_ref, sfin_ref, wq_buf, u_buf, attn_buf, ket_buf, gl_buf, obuf, s_ref = rest
    c_len = CHUNK
    n_chunks = t_len // c_len
    n_tot = n_sb * n_chunks
    last_row0 = n_sb * t_len - SUBLANES
    kh = pl.program_id(1)
    n_vh = 32
    tri_t = _iota((c_len, c_len), 0)
    tri_s = _iota((c_len, c_len), 1)
    lane_ab = _iota((c_len, 128), 1)
    row_abt = _iota((128, c_len), 0)
    lane_1 = _iota((1, 128), 1)
    lower = (tri_s <= tri_t).astype(F32)

    for s in range(n_sb):
        for j in range(4):
            if has_s0:
                s_ref[s * 4 + j] = s0_ref[s, j // 2, j % 2]
            else:
                s_ref[s * 4 + j] = jnp.zeros((GD_DK, GD_DV), F32)

    def conv_silu(ref, w_ref, cg):
        c = cg % n_chunks
        r0 = pl.multiple_of(cg * c_len, c_len)
        main = ref[pl.ds(r0, c_len), :]
        up0 = pl.multiple_of(jnp.maximum(r0 - SUBLANES, 0), SUBLANES)
        dn0 = pl.multiple_of(jnp.minimum(r0 + c_len, last_row0), SUBLANES)
        up = ref[pl.ds(up0, SUBLANES), :] * jnp.where(c > 0, 1.0, 0.0)
        dn = ref[pl.ds(dn0, SUBLANES), :] * jnp.where(c < n_chunks - 1, 1.0, 0.0)
        ext = jnp.concatenate([up, main, dn], axis=0)
        w = w_ref[...]
        acc = ext[SUBLANES - 1:SUBLANES - 1 + c_len] * w[0:1]
        for j in range(1, GD_CONV):
            acc = acc + ext[SUBLANES - 1 + j:SUBLANES - 1 + j + c_len] * w[j:j + 1]
        return _silu(acc)

    def l2n(z):
        return z * lax.rsqrt(jnp.sum(z * z, -1, keepdims=True) + EPS)

    incl_m = [tri_s <= tri_t, tri_s >= tri_t]
    strict_m = [tri_s < tri_t, tri_s > tri_t]
    n_cu = 2

    def phase_a(cg0):
        cgs = [cg0 + i for i in range(n_cu)]
        rows = [pl.ds(pl.multiple_of(cg * c_len, c_len), c_len) for cg in cgs]
        q = [l2n(conv_silu(q_ref, cq_ref, cg)) * (GD_DK ** -0.5) for cg in cgs]
        k = [l2n(conv_silu(k_ref, ck_ref, cg)) for cg in cgs]
        v2 = [conv_silu(v_ref, cv_ref, cg) for cg in cgs]
        ab = [ab_ref[rw, :] for rw in rows]
        k_t = [x.T for x in k]
        g_all = [-jnp.exp(alog_ref[...]) * _softplus(x + dtb_ref[...]) for x in ab]
        beta_all = [jax.nn.sigmoid(x) for x in ab]
        prefix = [_mm(lower, g, "nn", P_CUM) for g in g_all]
        g_kk = [_mm(x, y, "nn", P_GRAM) for x, y in zip(k, k_t)]
        g_qk = [_mm(x, y, "nn", P_GRAM) for x, y in zip(q, k_t)]
        gtot_all = [jnp.sum(g, axis=0, keepdims=True) for g in g_all]
        gc_all = [jnp.where(lane_ab < 64, p, t - p + g) for p, t, g in zip(prefix, gtot_all, g_all)]
        gct_all = [x.T for x in gc_all]
        per = []
        for i in range(n_cu):
            for d in range(2):
                for vl in range(2):
                    col_g = d * 64 + 2 * kh + vl
                    col_b = col_g + n_vh
                    gc_col = jnp.sum(jnp.where(lane_ab == col_g, gc_all[i], 0.0), axis=1, keepdims=True)
                    beta = jnp.sum(jnp.where(lane_ab == col_b, beta_all[i], 0.0), axis=1, keepdims=True)
                    gc_row = jnp.sum(jnp.where(row_abt == col_g, gct_all[i], 0.0), axis=0, keepdims=True)
                    g_last = jnp.sum(jnp.where(lane_1 == col_g, gtot_all[i], 0.0), axis=1, keepdims=True)
                    decay = jnp.where(incl_m[d], jnp.exp(jnp.minimum(gc_col - gc_row, 0.0)), 0.0)
                    a_mat = jnp.where(strict_m[d], beta * g_kk[i] * decay, 0.0)
                    per.append((i, 2 * d + vl, gc_col, beta, gc_row, g_last, decay, a_mat))
        t_all = _unit_tri_inverse(jnp.stack([-p[7] for p in per]))
        rhs = [jnp.concatenate([v2[i][:, (j % 2) * GD_DV:(j % 2 + 1) * GD_DV] * beta,
                                k[i] * (beta * jnp.exp(gc_col))], axis=1)
               for i, j, gc_col, beta, _, _, _, _ in per]
        uw = [_mm(t_all[n], x, "nn", P_APPLY) for n, x in enumerate(rhs)]
        for n, (i, j, gc_col, beta, gc_row, g_last, decay, _) in enumerate(per):
            idx = cgs[i] * 4 + j
            u_buf[idx] = uw[n][:, :GD_DV]
            wq_buf[idx, 0:c_len, :] = uw[n][:, GD_DV:].astype(BF16)
            wq_buf[idx, c_len:, :] = (q[i] * jnp.exp(gc_col)).astype(BF16)
            attn_buf[idx] = (g_qk[i] * decay).astype(BF16)
            ket_buf[idx] = (k_t[i] * jnp.exp(g_last - gc_row)).astype(BF16)
            gl_buf[idx] = jnp.broadcast_to(jnp.exp(g_last), (SUBLANES, GD_DV))

    def a_body(i, carry):
        phase_a(n_cu * i)
        return carry

    lax.fori_loop(0, n_tot // n_cu, a_body, 0)

    def b_body(i, carry):
        chains = [(s, d, vl) for s in range(n_sb) for d in range(2) for vl in range(2)]
        cg = [s * n_chunks + (i if d == 0 else n_chunks - 1 - i) for s, d, _ in chains]
        idx = [c * 4 + 2 * d + vl for c, (_, d, vl) in zip(cg, chains)]
        s_mat = [s_ref[s * 4 + 2 * d + vl] for s, d, vl in chains]
        ws_qs = [_mm(wq_buf[n], x, "nn", P_APPLY) for n, x in zip(idx, s_mat)]
        v_new = [(u_buf[n] - x[0:c_len]).astype(BF16) for n, x in zip(idx, ws_qs)]
        o = [x[c_len:] + _mm(attn_buf[n], y, "nn", P_APPLY) for n, x, y in zip(idx, ws_qs, v_new)]
        s_new = [x * gl_buf[n, 0:1, :] + _mm(ket_buf[n], y, "nn", P_APPLY) for n, x, y in zip(idx, s_mat, v_new)]
        for n, (s, d, vl) in enumerate(chains):
            rows = pl.ds(pl.multiple_of(cg[n] * c_len, c_len), c_len)
            obuf[d, rows, pl.ds(vl * GD_DV, GD_DV)] = o[n]
            s_ref[s * 4 + 2 * d + vl] = s_new[n]
        return carry

    lax.fori_loop(0, n_chunks, b_body, 0)

    def c_body(cg, carry):
        rows = pl.ds(pl.multiple_of(cg * c_len, c_len), c_len)
        for vl in range(2):
            cols = pl.ds(vl * GD_DV, GD_DV)
            o = obuf[0, rows, cols] + obuf[1, rows, cols]
            o = o * lax.rsqrt(jnp.mean(o * o, -1, keepdims=True) + EPS) * nw_ref[...] * _silu(z_ref[rows, cols])
            o_ref[rows, cols] = o.astype(o_ref.dtype)
        return carry

    lax.fori_loop(0, n_tot, c_body, 0)
    for s in range(n_sb):
        for j in range(4):
            sfin_ref[s, j // 2, j % 2] = s_ref[s * 4 + j]


def _gdn_scan(proj, ab, conv_w, alog_row, dtb_row, nw, s0, row0, n_seq, t_len, n_sb):
    n_kh = 16
    blk_rows = n_sb * t_len
    assert row0 % blk_rows == 0 and n_seq % n_sb == 0 and (n_sb * t_len // CHUNK) % 2 == 0
    blk0 = row0 // blk_rows
    vw = 2 * GD_DV
    n_units = n_sb * (t_len // CHUNK) * 4
    in_specs = [pl.BlockSpec((blk_rows, GD_DK), lambda b, h: (blk0 + b, h)),
                pl.BlockSpec((blk_rows, GD_DK), lambda b, h: (blk0 + b, n_kh + h)),
                pl.BlockSpec((blk_rows, vw), lambda b, h: (blk0 + b, n_kh + h)),
                pl.BlockSpec((blk_rows, vw), lambda b, h: (blk0 + b, 2 * n_kh + h)),
                pl.BlockSpec((blk_rows, 128), lambda b, h: (blk0 + b, 0)),
                pl.BlockSpec((GD_CONV, GD_DK), lambda b, h: (0, h)),
                pl.BlockSpec((GD_CONV, GD_DK), lambda b, h: (0, n_kh + h)),
                pl.BlockSpec((GD_CONV, vw), lambda b, h: (0, n_kh + h)),
                pl.BlockSpec((1, 128), lambda b, h: (0, 0)),
                pl.BlockSpec((1, 128), lambda b, h: (0, 0)),
                pl.BlockSpec((1, GD_DV), lambda b, h: (0, 0))]
    args = [proj, proj, proj, proj, ab, conv_w, conv_w, conv_w, alog_row, dtb_row, nw]
    if s0 is not None:
        in_specs.append(pl.BlockSpec((n_sb, 2, 2, GD_DK, GD_DV), lambda b, h: (b, 0, h, 0, 0)))
        args.append(s0)
    o, sfin = pl.pallas_call(
        functools.partial(_gdn_kernel, t_len=t_len, n_sb=n_sb, has_s0=s0 is not None),
        grid=(n_seq // n_sb, n_kh),
        in_specs=in_specs,
        out_specs=[pl.BlockSpec((blk_rows, vw), lambda b, h: (b, h)),
                   pl.BlockSpec((n_sb, 2, 2, GD_DK, GD_DV), lambda b, h: (b, 0, h, 0, 0))],
        out_shape=[jax.ShapeDtypeStruct((n_seq * t_len, 2 * n_kh * GD_DV), BF16),
                   jax.ShapeDtypeStruct((n_seq, 2, 2 * n_kh, GD_DK, GD_DV), F32)],
        scratch_shapes=[pltpu.VMEM((n_units, 2 * CHUNK, GD_DV), BF16),
                        pltpu.VMEM((n_units, CHUNK, GD_DV), F32),
                        pltpu.VMEM((n_units, CHUNK, CHUNK), BF16),
                        pltpu.VMEM((n_units, GD_DK, CHUNK), BF16),
                        pltpu.VMEM((n_units, SUBLANES, GD_DV), F32),
                        pltpu.VMEM((2, blk_rows, vw), F32),
                        pltpu.VMEM((4 * n_sb, GD_DK, GD_DV), F32)],
        compiler_params=_cparams("arbitrary", "arbitrary"),
        name=f"gdn_scan_t{t_len}",
    )(*args)
    return o, sfin


def _grid_pos_embed(n_tokens, d_model):
    rows = n_tokens // GRID_W
    row = jnp.broadcast_to(jnp.arange(rows, dtype=F32)[:, None], (rows, GRID_W)).reshape(-1)
    col = jnp.broadcast_to(jnp.arange(GRID_W, dtype=F32)[None, :], (rows, GRID_W)).reshape(-1)
    quarter = d_model // 4
    omega = 1.0 / (POS_BASE ** (jnp.arange(quarter, dtype=F32) / quarter))
    ar = row[:, None] * omega
    ac = col[:, None] * omega
    return jnp.concatenate([jnp.sin(ar), jnp.cos(ar), jnp.sin(ac), jnp.cos(ac)], -1)


def _pad_axis(a, axis, size):
    pad = [(0, 0)] * a.ndim
    pad[axis] = (0, size - a.shape[axis])
    return jnp.pad(a, pad)


def kernel(x_prompt, x_sample, state_rwkv, state_gdn, c, c_ctx, norm_mix, norm_ffn, norm_final, w_mod, b_mod, ffn_w1, ffn_w3, ffn_w2, rw_mu, rw_wr, rw_wk, rw_wv, rw_wo, rw_w0, rw_w1, rw_w2, rw_a0, rw_a1, rw_a2, rw_g1, rw_g2, rw_kk, rw_ka, rw_rk, rw_ln_w, rw_ln_b, gd_w_in, gd_conv, gd_a_log, gd_dt_bias, gd_norm, gd_w_out):
    n_p, t_p, d = x_prompt.shape
    n_s, t_s, _ = x_sample.shape
    rows_p = n_p * t_p
    bf = lambda a: a.astype(BF16)

    x = jnp.concatenate([x_prompt.reshape(rows_p, d),
                         (x_sample + _grid_pos_embed(t_s, d)[None]).reshape(n_s * t_s, d)], axis=0)
    regions = [(n_p, t_p, [0] * n_p), (n_s, t_s, list(range(1, n_s + 1)))]
    tm_shift = 256
    tm_big = 512
    tables = _tile_tables(regions, tm_shift)
    grp_big = _tile_tables(regions, tm_big)[0]

    cond = _pad_axis(jnp.concatenate([c_ctx[None], c], axis=0), 0, SUBLANES)
    mod = _modulation(cond, w_mod, b_mod).reshape(w_mod.shape[0], SUBLANES, 6, d)

    row = lambda a: a.reshape(1, -1)
    w_rkv = bf(jnp.stack([rw_wr[0], rw_wk[0], rw_wv[0]]))
    lp = RW_LORA_PAD
    rkv, gate, lw, a = _rwkv_inputs(
        x, mod[0], tables, row(norm_mix[0]), rw_mu[0], w_rkv,
        bf(rw_g1[0]), bf(rw_g2[0]),
        bf(_pad_axis(rw_w1[0], 2, lp)), bf(_pad_axis(rw_w2[0], 1, lp)),
        bf(_pad_axis(rw_a1[0], 2, lp)), bf(_pad_axis(rw_a2[0], 1, lp)),
        rw_w0[0][:, None, :], rw_a0[0][:, None, :], tm_shift)
    scan_args = (rkv, gate, lw, a, row(rw_kk[0]), row(rw_ka[0]), row(rw_rk[0]), row(rw_ln_w[0]), row(rw_ln_b[0]))
    y_p, s_rwkv = _rwkv_scan(*scan_args, None, 0, n_p, t_p, 2)
    s0_rw = state_rwkv[:, 0].reshape(n_s, 2, d, RW_HEAD)
    y_s, _ = _rwkv_scan(*scan_args, s0_rw, rows_p, n_s, t_s, 1)
    y = jnp.concatenate([y_p, y_s], axis=0)
    x = _out_proj(y, bf(rw_wo[0]), x, mod[0], grp_big, 2, tm_big)
    x = _ffn(x, mod[0], grp_big, row(norm_ffn[0]), bf(ffn_w1[0]), bf(ffn_w3[0]), bf(ffn_w2[0]),
             row(norm_final), False, tm_big)

    n_main = 3 * 4096
    proj = _in_proj(x, mod[1], grp_big, row(norm_mix[1]), bf(gd_w_in[0][:, :n_main]), tm_big, 1024)
    ab = _in_proj(x, mod[1], grp_big, row(norm_mix[1]), bf(gd_w_in[0][:, n_main:]), tm_big, 128)
    zeros32 = jnp.zeros((2, 32), F32)
    alog_row = jnp.stack([gd_a_log[0], zeros32], axis=1).reshape(1, 128)
    dtb_row = jnp.stack([gd_dt_bias[0], zeros32], axis=1).reshape(1, 128)
    gd_args = (proj, ab, gd_conv[0], alog_row, dtb_row, row(gd_norm[0]))
    o_p, s_gdn = _gdn_scan(*gd_args, None, 0, n_p, t_p, 2)
    o_s, _ = _gdn_scan(*gd_args, state_gdn[:, 0], rows_p, n_s, t_s, 1)
    o = jnp.concatenate([o_p, o_s], axis=0)
    x = _out_proj(o, bf(gd_w_out[0]), x, mod[1], grp_big, 2, tm_big)
    x = _ffn(x, mod[1], grp_big, row(norm_ffn[1]), bf(ffn_w1[1]), bf(ffn_w3[1]), bf(ffn_w2[1]),
             row(norm_final), True, tm_big)

    y_prompt = x[:rows_p].reshape(n_p, t_p, d)
    y_sample = x[rows_p:].reshape(n_s, t_s, d)
    new_state_rwkv = s_rwkv.reshape(n_p, 1, 2, d // RW_HEAD, RW_HEAD, RW_HEAD)
    new_state_gdn = s_gdn.reshape(n_p, 1, 2, 32, GD_DK, GD_DV)
    return (y_prompt, y_sample, new_state_rwkv, new_state_gdn)
```

```python
import functools

import numpy as np
import jax
import jax.numpy as jnp
from jax import lax
from jax.experimental import pallas as pl
from jax.experimental.pallas import tpu as pltpu

F32 = jnp.float32
BF16 = jnp.bfloat16

EPS = 1e-6
GRID_W = 64
POS_BASE = 10000.0
RW_HEAD = 64
RW_GROUP_HEADS = 4
RW_LANES = RW_HEAD * RW_GROUP_HEADS
RW_LORA_PAD = 128
GD_DK = 128
GD_DV = 128
GD_CONV = 4
CHUNK = 64
RW_SUPER_CHUNKS = 8
SUBLANES = 8
VMEM_LIMIT = 56 * 1024 * 1024


def _cparams(*sem):
    return pltpu.CompilerParams(dimension_semantics=sem, vmem_limit_bytes=VMEM_LIMIT)


def _dot(a, b):
    return jnp.dot(a.astype(BF16), b.astype(BF16), preferred_element_type=F32)


_DIMS = {"nn": (((1,), (0,)), ((), ())),
         "nt": (((1,), (1,)), ((), ())),
         "tn": (((0,), (0,)), ((), ())),
         "bnn": (((2,), (1,)), ((0,), (0,)))}


def _split_bf16(x, pieces):
    out = []
    for _ in range(pieces - 1):
        p = x.astype(BF16)
        out.append(p)
        x = x - p.astype(F32)
    out.append(x.astype(BF16))
    return out


def _mm(a, b, dims="nn", mode="b"):
    dn = _DIMS[dims]
    dg = lambda x, y: lax.dot_general(x, y, dn, preferred_element_type=F32)
    if mode == "b":
        return dg(a.astype(BF16), b.astype(BF16))
    if mode in ("la", "la2"):
        a0 = a.astype(BF16)
        return sum(dg(a0, p) for p in _split_bf16(b, 3 if mode == "la" else 2))
    if mode in ("ra", "ra2"):
        b0 = b.astype(BF16)
        return sum(dg(p, b0) for p in _split_bf16(a, 3 if mode == "ra" else 2))
    raise ValueError(mode)


P_INV = "b"
P_GRAM = "b"
P_APPLY = "b"
P_SUM = "ra2"
P_CUM = "la2"


def _iota(shape, dim):
    return lax.broadcasted_iota(jnp.int32, shape, dim)


def _silu(x):
    return x * jax.nn.sigmoid(x)


def _softplus(x):
    return jnp.maximum(x, 0.0) + jnp.log(1.0 + jnp.exp(-jnp.abs(x)))


def _rms_mod(x, nw, sh, sc):
    y = x * lax.rsqrt(jnp.mean(x * x, -1, keepdims=True) + EPS)
    return (y * nw) * (1.0 + sc) + sh


def _inverse_masks(c_len):
    ij = _iota((c_len, c_len), 0) ^ _iota((c_len, c_len), 1)
    levels = []
    half = 2
    while half < c_len:
        levels.append(((ij >= half) & (ij < 2 * half)).astype(BF16))
        half *= 2
    return (ij == 0).astype(F32), (ij == 1).astype(F32), levels


def _unit_tri_inverse(n_lower, n_upper, masks):
    eye, pair, levels = masks
    mats = [n_lower, n_upper]
    c_len = n_lower.shape[-1]
    d_mats = [eye + m * pair for m in mats]
    n_bf = [m.astype(BF16) for m in mats]
    half = 2
    for lvl in levels:
        l_mats = [m * lvl for m in n_bf]
        if half < SUBLANES:
            dl = [_mm(d, l, "bnn", P_INV) for d, l in zip(d_mats, l_mats)]
            d_mats = [d + _mm(x, d, "bnn", P_INV) for d, x in zip(d_mats, dl)]
        else:
            first = [slice(b0, b0 + half) for b0 in range(0, c_len, 2 * half)]
            second = [slice(b0 + half, b0 + 2 * half) for b0 in range(0, c_len, 2 * half)]
            act = [second, first]
            d_act = [jnp.concatenate([d[:, sl, :] for sl in rows], axis=1) for d, rows in zip(d_mats, act)]
            dl = [_mm(x, l, "bnn", P_INV) for x, l in zip(d_act, l_mats)]
            upd = [x + _mm(y, d, "bnn", P_INV) for x, y, d in zip(d_act, dl, d_mats)]
            out = []
            for g, (d, u) in enumerate(zip(d_mats, upd)):
                parts = []
                for n in range(len(first)):
                    new = u[:, n * half:(n + 1) * half, :]
                    parts += [d[:, first[n], :], new] if g == 0 else [new, d[:, second[n], :]]
                out.append(jnp.concatenate(parts, axis=1))
            d_mats = out
        half *= 2
    return d_mats


def _mod_kernel(c_ref, w_ref, b_ref, o_ref):
    o_ref[0] = _dot(_silu(c_ref[...]), w_ref[0]) + b_ref[0]


def _modulation(cond, w_mod, b_mod):
    depth, d, n = w_mod.shape
    tn = 1024
    return pl.pallas_call(
        _mod_kernel,
        grid=(depth, n // tn),
        in_specs=[pl.BlockSpec((SUBLANES, d), lambda l, j: (0, 0)),
                  pl.BlockSpec((1, d, tn), lambda l, j: (l, 0, j)),
                  pl.BlockSpec((1, 1, tn), lambda l, j: (l, 0, j))],
        out_specs=pl.BlockSpec((1, SUBLANES, tn), lambda l, j: (l, 0, j)),
        out_shape=jax.ShapeDtypeStruct((depth, SUBLANES, n), F32),
        compiler_params=_cparams("arbitrary", "arbitrary"),
        name="modulation",
    )(cond, w_mod, b_mod.reshape(depth, 1, n))


def _shift_mix(x_ref, xp_ref, xn_ref, mod_ref, nw_ref, has_prev, has_next):
    nw = nw_ref[...]
    sh = mod_ref[0, 0:1, :]
    sc = mod_ref[0, 1:2, :]
    h = _rms_mod(x_ref[...], nw, sh, sc)
    tm = h.shape[0]
    hp = _rms_mod(xp_ref[SUBLANES - 1:SUBLANES, :], nw, sh, sc) * has_prev
    hn = _rms_mod(xn_ref[0:1, :], nw, sh, sc) * has_next
    rows = _iota(h.shape, 0)
    prev = jnp.where(rows == 0, hp, pltpu.roll(h, 1, 0))
    nxt = jnp.where(rows == tm - 1, hn, pltpu.roll(h, tm - 1, 0))
    return h, 0.5 * (prev + nxt) - h


def _rkv_kernel(grp_ref, hp_ref, hn_ref, x_ref, xp_ref, xn_ref, mod_ref, nw_ref, mu_ref, w_ref, o_ref):
    i = pl.program_id(1)
    h, xx = _shift_mix(x_ref, xp_ref, xn_ref, mod_ref, nw_ref,
                       hp_ref[i].astype(F32), hn_ref[i].astype(F32))
    o_ref[0] = _dot(h + xx * mu_ref[0], w_ref[0])


def _lora_kernel(grp_ref, hp_ref, hn_ref, x_ref, xp_ref, xn_ref, mod_ref, nw_ref, mu_ref,
                 g1_ref, g2_ref, w1_ref, w2_ref, a1_ref, a2_ref, w0_ref, a0_ref,
                 gate_ref, lw_ref, a_ref):
    i = pl.program_id(0)
    h, xx = _shift_mix(x_ref, xp_ref, xn_ref, mod_ref, nw_ref,
                       hp_ref[i].astype(F32), hn_ref[i].astype(F32))
    xw = h + xx * mu_ref[1:2, :]
    xa = h + xx * mu_ref[4:5, :]
    xg = h + xx * mu_ref[5:6, :]
    gate_ref[...] = _dot(jax.nn.sigmoid(_dot(xg, g1_ref[...])), g2_ref[...])
    for d in range(2):
        lw = jnp.tanh(_dot(xw, w1_ref[d]))
        w_log = -_softplus(-(w0_ref[d] + _dot(lw, w2_ref[d]))) - 0.5
        lw_ref[d] = -jnp.exp(w_log)
        a_ref[d] = jax.nn.sigmoid(a0_ref[d] + _dot(_dot(xa, a1_ref[d]), a2_ref[d]))


def _tile_tables(regions, tm):
    row_grp, row_pos, row_len = [], [], []
    for n_seq, t_len, groups in regions:
        for s in range(n_seq):
            row_grp += [groups[s]] * t_len
            row_pos += list(range(t_len))
            row_len += [t_len] * t_len
    row_grp, row_pos, row_len = (np.asarray(v).reshape(-1, tm) for v in (row_grp, row_pos, row_len))
    assert (row_grp == row_grp[:, :1]).all(), "a row tile must not straddle modulation groups"
    as_i32 = lambda v: jnp.asarray(np.asarray(v, np.int32))
    return (as_i32(row_grp[:, 0]), as_i32(row_pos[:, 0] > 0),
            as_i32(row_pos[:, -1] < row_len[:, -1] - 1))


def _halo_specs(tm, d, m_rows, n_lead):
    blocks = tm // SUBLANES
    last = m_rows // SUBLANES - 1
    if n_lead == 1:
        cur = lambda j, i, *_: (i, 0)
        prev = lambda j, i, *_: (jnp.maximum(i * blocks - 1, 0), 0)
        nxt = lambda j, i, *_: (jnp.minimum((i + 1) * blocks, last), 0)
    else:
        cur = lambda i, *_: (i, 0)
        prev = lambda i, *_: (jnp.maximum(i * blocks - 1, 0), 0)
        nxt = lambda i, *_: (jnp.minimum((i + 1) * blocks, last), 0)
    return [pl.BlockSpec((tm, d), cur), pl.BlockSpec((SUBLANES, d), prev), pl.BlockSpec((SUBLANES, d), nxt)]


def _rwkv_inputs(x, mod, tables, nw, mu, w_rkv, g1, g2, w1, w2, a1, a2, w0, a0, tm):
    m_rows, d = x.shape
    n_tiles = m_rows // tm
    mu_rkv = jnp.stack([mu[0], mu[2], mu[3]])[:, None, :]
    rkv = pl.pallas_call(
        _rkv_kernel,
        grid_spec=pltpu.PrefetchScalarGridSpec(
            num_scalar_prefetch=3, grid=(3, n_tiles),
            in_specs=_halo_specs(tm, d, m_rows, 1) + [
                pl.BlockSpec((1, 6, d), lambda j, i, g, *_: (g[i], 0, 0)),
                pl.BlockSpec((1, d), lambda j, i, *_: (0, 0)),
                pl.BlockSpec((1, 1, d), lambda j, i, *_: (j, 0, 0)),
                pl.BlockSpec((1, d, d), lambda j, i, *_: (j, 0, 0))],
            out_specs=pl.BlockSpec((1, tm, d), lambda j, i, *_: (j, i, 0))),
        out_shape=jax.ShapeDtypeStruct((3, m_rows, d), F32),
        compiler_params=_cparams("arbitrary", "arbitrary"),
        name="rwkv_rkv_proj",
    )(*tables, x, x, x, mod, nw, mu_rkv, w_rkv)

    full = lambda a: pl.BlockSpec(a.shape, lambda i, *_: (0,) * a.ndim)
    small = [g1, g2, w1, w2, a1, a2, w0, a0]
    gate, lw, a = pl.pallas_call(
        _lora_kernel,
        grid_spec=pltpu.PrefetchScalarGridSpec(
            num_scalar_prefetch=3, grid=(n_tiles,),
            in_specs=_halo_specs(tm, d, m_rows, 0) + [
                pl.BlockSpec((1, 6, d), lambda i, g, *_: (g[i], 0, 0)),
                pl.BlockSpec((1, d), lambda i, *_: (0, 0)),
                full(mu)] + [full(s) for s in small],
            out_specs=[pl.BlockSpec((tm, d), lambda i, *_: (i, 0)),
                       pl.BlockSpec((2, tm, d), lambda i, *_: (0, i, 0)),
                       pl.BlockSpec((2, tm, d), lambda i, *_: (0, i, 0))]),
        out_shape=[jax.ShapeDtypeStruct((m_rows, d), F32),
                   jax.ShapeDtypeStruct((2, m_rows, d), F32),
                   jax.ShapeDtypeStruct((2, m_rows, d), F32)],
        compiler_params=_cparams("arbitrary"),
        name="rwkv_lora_proj",
    )(*tables, x, x, x, mod, nw, mu, *small)
    return rkv, gate, lw, a


def _rwkv_scan_kernel(*refs, t_len, n_sb, has_s0):
    (r_ref, k_ref, v_ref, gate_ref, lw_ref, a_ref, kk_ref, ka_ref, rk_ref, lnw_ref, lnb_ref) = refs[:11]
    rest = refs[11:]
    if has_s0:
        s0_ref, rest = rest[0], rest[1:]
    (y_ref, sfin_ref, r2_buf, au_buf, bw_buf, kw_buf, y0_buf, w_buf, ybuf, bonus_buf, s_ref) = rest
    c_len, lanes, heads = CHUNK, RW_LANES, RW_GROUP_HEADS
    n_chunks = t_len // c_len
    sc = min(n_chunks, RW_SUPER_CHUNKS)
    n_super = n_chunks // sc
    n_pp = max(1, 8 // (2 * n_sb))
    stack = heads * c_len

    same_head = (_iota((stack, lanes), 0) // c_len == _iota((stack, lanes), 1) // RW_HEAD).astype(F32)
    ones_blk = same_head
    rep = (_iota((RW_HEAD, lanes), 1) % RW_HEAD == _iota((RW_HEAD, lanes), 0)).astype(F32)
    t_idx = _iota((stack, c_len), 0) % c_len
    s_idx = _iota((stack, c_len), 1)
    tri_t = _iota((c_len, c_len), 0)
    tri_s = _iota((c_len, c_len), 1)

    inv_masks = _inverse_masks(c_len)
    kk_p, ka_p, rk_p = kk_ref[...], ka_ref[...], rk_ref[...]
    lnw, lnb = lnw_ref[...], lnb_ref[...]

    def fold(z):
        z = z * same_head
        return z[0:c_len] + z[c_len:2 * c_len] + z[2 * c_len:3 * c_len] + z[3 * c_len:4 * c_len]

    def tile4(z):
        return jnp.concatenate([z, z, z, z], axis=0) * same_head

    for s in range(n_sb):
        for d in range(2):
            if has_s0:
                s_ref[s * 2 + d] = _mm(s0_ref[s, d], rep, "nn", "ra") * ones_blk
            else:
                s_ref[s * 2 + d] = jnp.zeros((lanes, lanes), F32)
    ybuf[...] = jnp.zeros_like(ybuf)
    bonus_buf[...] = jnp.zeros_like(bonus_buf)

    def chunk_rows(s, d, p):
        c = p if d == 0 else n_chunks - 1 - p
        return pl.ds(pl.multiple_of((s * n_chunks + c) * c_len, c_len), c_len)

    incl_f = [(tri_s <= tri_t).astype(F32), (tri_s >= tri_t).astype(F32)]
    incl4 = [s_idx <= t_idx, s_idx >= t_idx]
    strict4 = [s_idx < t_idx, s_idx > t_idx]

    def phase_a(p0, j0):
        units = [(s, d, q) for s in range(n_sb) for d in range(2) for q in range(n_pp)]
        dirs = [d for _, d, _ in units]
        rows = [chunk_rows(s, d, p0 + q) for s, d, q in units]
        r = [r_ref[rw, :] for rw in rows]
        k = [k_ref[rw, :] for rw in rows]
        v = [v_ref[rw, :] for rw in rows]
        lw = [lw_ref[d, rw, :] for d, rw in zip(dirs, rows)]
        a = [a_ref[d, rw, :] for d, rw in zip(dirs, rows)]
        bonus_old = [bonus_buf[rw, :] for rw in rows]
        kx = [ki * kk_p for ki in k]
        ss = [_mm(x * x, ones_blk, "nn", P_SUM) for x in kx]
        cw = [_mm(incl_f[d], x, "nn", P_CUM) for d, x in zip(dirs, lw)]
        bsum = [_mm(ri * ki * (1.0 + (ai - 1.0) * ka_p) * rk_p, ones_blk, "nn", P_SUM)
                for ri, ki, ai in zip(r, k, a)]
        kkn = [x * lax.rsqrt(y + EPS) for x, y in zip(kx, ss)]
        kd = [ki * (1.0 + (ai - 1.0) * ka_p) for ki, ai in zip(k, a)]
        w_row = [jnp.exp(jnp.sum(x, axis=0, keepdims=True)) for x in lw]
        e_neg = [jnp.exp(-x) for x in cw]
        at = [-kn * jnp.exp(c - l) for kn, c, l in zip(kkn, cw, lw)]
        bt = [kn * ai * e for kn, ai, e in zip(kkn, a, e_neg)]
        kt = [x * e for x, e in zip(kd, e_neg)]
        rt = [ri * jnp.exp(c) for ri, c in zip(r, cw)]
        lhs = [jnp.concatenate([tile4(x), tile4(y)], axis=0) for x, y in zip(at, rt)]
        g_b = [_mm(x, y, "nt", P_GRAM) for x, y in zip(lhs, bt)]
        g_k = [_mm(x, y, "nt", P_GRAM) for x, y in zip(lhs, kt)]
        n_s = [jnp.where(strict4[d], g[0:stack], 0.0) for d, g in zip(dirs, g_b)]
        ak_s = [jnp.where(strict4[d], g[0:stack], 0.0) for d, g in zip(dirs, g_k)]
        rb_s = [jnp.where(incl4[d], g[stack:], 0.0) for d, g in zip(dirs, g_b)]
        rk_s = [jnp.where(incl4[d], g[stack:], 0.0) for d, g in zip(dirs, g_k)]
        sel = [[i for i, di in enumerate(dirs) if di == d] for d in range(2)]
        t_d = _unit_tri_inverse(*[jnp.concatenate([n_s[i].reshape(heads, c_len, c_len) for i in sl], axis=0)
                                  for sl in sel], inv_masks)
        t_s = [None] * len(units)
        for d in range(2):
            for n, i in enumerate(sel[d]):
                t_s[i] = t_d[d][n * heads:(n + 1) * heads].reshape(stack, c_len)
        a2 = [fold(_mm(t, x, "nn", P_APPLY)) for t, x in zip(t_s, at)]
        akv = [fold(_mm(x, vi, "nn", P_APPLY)) for x, vi in zip(ak_s, v)]
        rkv = [_mm(x, vi, "nn", P_APPLY) for x, vi in zip(rk_s, v)]
        u0 = [fold(_mm(t, x, "nn", P_APPLY)) for t, x in zip(t_s, akv)]
        r2 = [x + fold(_mm(rb, y, "nn", P_APPLY)) for x, rb, y in zip(rt, rb_s, a2)]
        y0 = [fold(_mm(rb, x, "nn", P_APPLY) + y) for rb, x, y in zip(rb_s, u0, rkv)]
        for i, (s, d, q) in enumerate(units):
            slot = (s * 2 + d) * sc + j0 + q
            r2_buf[slot] = r2[i].astype(BF16)
            au_buf[slot, :, 0:lanes] = a2[i].astype(BF16)
            au_buf[slot, :, lanes:] = u0[i].astype(BF16)
            bw_buf[slot] = (bt[i] * w_row[i]).astype(BF16)
            kw_buf[slot] = (kt[i] * w_row[i]).astype(BF16)
            y0_buf[slot] = y0[i]
            w_buf[slot] = jnp.broadcast_to(w_row[i], (SUBLANES, lanes))
            bonus_buf[rows[i], :] = bonus_old[i] + bsum[i] * v[i]

    def phase_b(p, j):
        units = [(s, d) for s in range(n_sb) for d in range(2)]
        rows = [chunk_rows(s, d, p) for s, d in units]
        slots = [(s * 2 + d) * sc + j for s, d in units]
        s_mat = [s_ref[s * 2 + d] for s, d in units]
        y_old = [ybuf[rw, :] for rw in rows]
        vv = [v_ref[rw, :] for rw in rows]
        s_bf = [x.astype(BF16) for x in s_mat]
        au_bw = [_mm(au_buf[sl], bw_buf[sl], "tn", P_APPLY) for sl in slots]
        vk = [_mm(x, kw_buf[sl], "tn", P_APPLY) for x, sl in zip(vv, slots)]
        y = [_mm(r2_buf[sl], x, "nt", P_APPLY) + y0_buf[sl] for sl, x in zip(slots, s_bf)]
        s_new = [x * w_buf[sl, 0:1, :] + _mm(xb, m[0:lanes] * ones_blk, "nn", P_APPLY) + (m[lanes:] + z) * ones_blk
                 for x, xb, m, z, sl in zip(s_mat, s_bf, au_bw, vk, slots)]
        for i, (s, d) in enumerate(units):
            ybuf[rows[i], :] = y_old[i] + y[i]
            s_ref[s * 2 + d] = s_new[i]

    def super_body(sp, carry):
        def a_body(j, c2):
            phase_a(sp * sc + j * n_pp, j * n_pp)
            return c2

        def b_body(j, c2):
            phase_b(sp * sc + j, j)
            return c2

        lax.fori_loop(0, sc // n_pp, a_body, 0)
        lax.fori_loop(0, sc, b_body, 0)
        return carry

    lax.fori_loop(0, n_super, super_body, 0)

    n_cc = 4

    def c_body(i, carry):
        rows = [pl.ds(pl.multiple_of((i * n_cc + q) * c_len, c_len), c_len) for q in range(n_cc)]
        y = [ybuf[rw, :] for rw in rows]
        mean = [_mm(x, ones_blk, "nn", P_SUM) * (1.0 / RW_HEAD) for x in y]
        yc = [x - m for x, m in zip(y, mean)]
        var = [_mm(x * x, ones_blk, "nn", P_SUM) * (1.0 / RW_HEAD) for x in yc]
        for rw, x, vr in zip(rows, yc, var):
            yn = x * lax.rsqrt(vr + RW_HEAD * 1e-5) * lnw + lnb
            y_ref[rw, :] = ((yn + bonus_buf[rw, :]) * gate_ref[rw, :]).astype(y_ref.dtype)
        return carry

    lax.fori_loop(0, n_sb * n_chunks // n_cc, c_body, 0)
    rep_t = (_iota((lanes, RW_HEAD), 0) % RW_HEAD == _iota((lanes, RW_HEAD), 1)).astype(F32)
    for s in range(n_sb):
        for d in range(2):
            sfin_ref[s, d] = _mm(s_ref[s * 2 + d], rep_t, "nn", "ra")


def _rwkv_scan(rkv, gate, lw, a, kk, ka, rk, lnw, lnb, s0, row0, n_seq, t_len, n_sb):
    _, m_rows, d = rkv.shape
    n_groups = d // RW_LANES
    blk_rows = n_sb * t_len
    assert row0 % blk_rows == 0 and n_seq % n_sb == 0
    blk0 = row0 // blk_rows
    sc = min(t_len // CHUNK, RW_SUPER_CHUNKS)
    assert (t_len // CHUNK) % sc == 0
    n_slots = n_sb * 2 * sc
    tok = lambda b, g: (blk0 + b, g)
    in_specs = [pl.BlockSpec((None, blk_rows, RW_LANES), lambda b, g, j=j: (j, blk0 + b, g)) for j in range(3)]
    in_specs += [pl.BlockSpec((blk_rows, RW_LANES), tok),
                 pl.BlockSpec((2, blk_rows, RW_LANES), lambda b, g: (0, blk0 + b, g)),
                 pl.BlockSpec((2, blk_rows, RW_LANES), lambda b, g: (0, blk0 + b, g))]
    in_specs += [pl.BlockSpec((1, RW_LANES), lambda b, g: (0, g))] * 5
    args = [rkv, rkv, rkv, gate, lw, a, kk, ka, rk, lnw, lnb]
    if s0 is not None:
        in_specs.append(pl.BlockSpec((n_sb, 2, RW_LANES, RW_HEAD), lambda b, g: (b, 0, g, 0)))
        args.append(s0)
    y, sfin = pl.pallas_call(
        functools.partial(_rwkv_scan_kernel, t_len=t_len, n_sb=n_sb, has_s0=s0 is not None),
        grid=(n_seq // n_sb, n_groups),
        in_specs=in_specs,
        out_specs=[pl.BlockSpec((blk_rows, RW_LANES), lambda b, g: (b, g)),
                   pl.BlockSpec((n_sb, 2, RW_LANES, RW_HEAD), lambda b, g: (b, 0, g, 0))],
        out_shape=[jax.ShapeDtypeStruct((n_seq * t_len, d), BF16),
                   jax.ShapeDtypeStruct((n_seq, 2, d, RW_HEAD), F32)],
        scratch_shapes=[pltpu.VMEM((n_slots, CHUNK, RW_LANES), BF16),
                        pltpu.VMEM((n_slots, CHUNK, 2 * RW_LANES), BF16),
                        pltpu.VMEM((n_slots, CHUNK, RW_LANES), BF16),
                        pltpu.VMEM((n_slots, CHUNK, RW_LANES), BF16),
                        pltpu.VMEM((n_slots, CHUNK, RW_LANES), F32),
                        pltpu.VMEM((n_slots, SUBLANES, RW_LANES), F32),
                        pltpu.VMEM((blk_rows, RW_LANES), F32),
                        pltpu.VMEM((blk_rows, RW_LANES), F32),
                        pltpu.VMEM((2 * n_sb, RW_LANES, RW_LANES), F32)],
        compiler_params=_cparams("arbitrary", "arbitrary"),
        name=f"rwkv_scan_t{t_len}",
    )(*args)
    return y, sfin


def _out_proj_kernel(grp_ref, y_ref, w_ref, x_ref, mod_ref, o_ref, *, gate_row):
    o_ref[...] = x_ref[...] + mod_ref[0, gate_row:gate_row + 1, :] * _dot(y_ref[...], w_ref[...])


def _out_proj(y, w, x, mod, grp, gate_row, tm):
    m_rows, k_dim = y.shape
    d = w.shape[1]
    tn = 1024
    return pl.pallas_call(
        functools.partial(_out_proj_kernel, gate_row=gate_row),
        grid_spec=pltpu.PrefetchScalarGridSpec(
            num_scalar_prefetch=1, grid=(d // tn, m_rows // tm),
            in_specs=[pl.BlockSpec((tm, k_dim), lambda n, i, g: (i, 0)),
                      pl.BlockSpec((k_dim, tn), lambda n, i, g: (0, n)),
                      pl.BlockSpec((tm, tn), lambda n, i, g: (i, n)),
                      pl.BlockSpec((1, 6, tn), lambda n, i, g: (g[i], 0, n))],
            out_specs=pl.BlockSpec((tm, tn), lambda n, i, g: (i, n))),
        out_shape=jax.ShapeDtypeStruct((m_rows, d), F32),
        compiler_params=_cparams("arbitrary", "arbitrary"),
        name="out_proj",
    )(grp, y, w, x, mod)


def _in_proj_kernel(grp_ref, x_ref, mod_ref, nw_ref, w_ref, o_ref):
    h = _rms_mod(x_ref[...], nw_ref[...], mod_ref[0, 0:1, :], mod_ref[0, 1:2, :])
    o_ref[...] = _dot(h, w_ref[...])


def _in_proj(x, mod, grp, nw, w, tm, tn):
    m_rows, d = x.shape
    n = w.shape[1]
    return pl.pallas_call(
        _in_proj_kernel,
        grid_spec=pltpu.PrefetchScalarGridSpec(
            num_scalar_prefetch=1, grid=(n // tn, m_rows // tm),
            in_specs=[pl.BlockSpec((tm, d), lambda n_, i, g: (i, 0)),
                      pl.BlockSpec((1, 6, d), lambda n_, i, g: (g[i], 0, 0)),
                      pl.BlockSpec((1, d), lambda n_, i, g: (0, 0)),
                      pl.BlockSpec((d, tn), lambda n_, i, g: (0, n_))],
            out_specs=pl.BlockSpec((tm, tn), lambda n_, i, g: (i, n_))),
        out_shape=jax.ShapeDtypeStruct((m_rows, n), F32),
        compiler_params=_cparams("arbitrary", "arbitrary"),
        name="gdn_in_proj",
    )(grp, x, mod, nw, w)


def _ffn_kernel(grp_ref, x_ref, mod_ref, nw_ref, w1_ref, w3_ref, w2_ref, fw_ref, o_ref, h_buf, acc,
                *, final_norm):
    f = pl.program_id(1)

    @pl.when(f == 0)
    def _():
        h_buf[...] = _rms_mod(x_ref[...], nw_ref[...], mod_ref[0, 3:4, :], mod_ref[0, 4:5, :]).astype(BF16)
        acc[...] = jnp.zeros_like(acc)

    h = h_buf[...]
    gate = jnp.dot(h, w1_ref[...], preferred_element_type=F32)
    up = jnp.dot(h, w3_ref[...], preferred_element_type=F32)
    acc[...] += _dot(_silu(gate) * up, w2_ref[...])

    @pl.when(f == pl.num_programs(1) - 1)
    def _():
        y = x_ref[...] + mod_ref[0, 5:6, :] * acc[...]
        if final_norm:
            y = y * lax.rsqrt(jnp.mean(y * y, -1, keepdims=True) + EPS) * fw_ref[...]
        o_ref[...] = y


def _ffn(x, mod, grp, nw, w1, w3, w2, fw, final_norm, tm):
    m_rows, d = x.shape
    d_ff = w1.shape[1]
    tf = 512
    return pl.pallas_call(
        functools.partial(_ffn_kernel, final_norm=final_norm),
        grid_spec=pltpu.PrefetchScalarGridSpec(
            num_scalar_prefetch=1, grid=(m_rows // tm, d_ff // tf),
            in_specs=[pl.BlockSpec((tm, d), lambda i, f, g: (i, 0)),
                      pl.BlockSpec((1, 6, d), lambda i, f, g: (g[i], 0, 0)),
                      pl.BlockSpec((1, d), lambda i, f, g: (0, 0)),
                      pl.BlockSpec((d, tf), lambda i, f, g: (0, f)),
                      pl.BlockSpec((d, tf), lambda i, f, g: (0, f)),
                      pl.BlockSpec((tf, d), lambda i, f, g: (f, 0)),
                      pl.BlockSpec((1, d), lambda i, f, g: (0, 0))],
            out_specs=pl.BlockSpec((tm, d), lambda i, f, g: (i, 0)),
            scratch_shapes=[pltpu.VMEM((tm, d), BF16), pltpu.VMEM((tm, d), F32)]),
        out_shape=jax.ShapeDtypeStruct((m_rows, d), F32),
        compiler_params=_cparams("arbitrary", "arbitrary"),
        name="ffn",
    )(grp, x, mod, nw, w1, w3, w2, fw)


def _gdn_kernel(*refs, t_len, n_sb, has_s0):
    (q_ref, k_ref, v_ref, z_ref, ab_ref, cq_ref, ck_ref, cv_ref, alog_ref, dtb_ref, nw_ref) = refs[:11]
    rest = refs[11:]
    if has_s0:
        s0_ref, rest = rest[0], rest[1:]
    o_ref, sfin_ref, wq_buf, u_buf, attn_buf, ket_buf, gl_buf, obuf, s_ref = rest
    c_len = CHUNK
    n_chunks = t_len // c_len
    n_tot = n_sb * n_chunks
    last_row0 = n_sb * t_len - SUBLANES
    kh = pl.program_id(1)
    n_vh = 32
    tri_t = _iota((c_len, c_len), 0)
    tri_s = _iota((c_len, c_len), 1)
    lane_ab = _iota((c_len, 128), 1)
    row_abt = _iota((128, c_len), 0)
    lane_1 = _iota((1, 128), 1)
    lower = (tri_s <= tri_t).astype(F32)
    inv_masks = _inverse_masks(c_len)

    for s in range(n_sb):
        for j in range(4):
            if has_s0:
                s_ref[s * 4 + j] = s0_ref[s, j // 2, j % 2]
            else:
                s_ref[s * 4 + j] = jnp.zeros((GD_DK, GD_DV), F32)

    def conv_silu(ref, w_ref, cg):
        c = cg % n_chunks
        r0 = pl.multiple_of(cg * c_len, c_len)
        main = ref[pl.ds(r0, c_len), :]
        up0 = pl.multiple_of(jnp.maximum(r0 - SUBLANES, 0), SUBLANES)
        dn0 = pl.multiple_of(jnp.minimum(r0 + c_len, last_row0), SUBLANES)
        up = ref[pl.ds(up0, SUBLANES), :] * jnp.where(c > 0, 1.0, 0.0)
        dn = ref[pl.ds(dn0, SUBLANES), :] * jnp.where(c < n_chunks - 1, 1.0, 0.0)
        ext = jnp.concatenate([up, main, dn], axis=0)
        w = w_ref[...]
        acc = ext[SUBLANES - 1:SUBLANES - 1 + c_len] * w[0:1]
        for j in range(1, GD_CONV):
            acc = acc + ext[SUBLANES - 1 + j:SUBLANES - 1 + j + c_len] * w[j:j + 1]
        return _silu(acc)

    def l2n(z):
        return z * lax.rsqrt(jnp.sum(z * z, -1, keepdims=True) + EPS)

    incl_m = [tri_s <= tri_t, tri_s >= tri_t]
    strict_m = [tri_s < tri_t, tri_s > tri_t]
    n_cu = 4

    def phase_a(cg0):
        cgs = [cg0 + i for i in range(n_cu)]
        rows = [pl.ds(pl.multiple_of(cg * c_len, c_len), c_len) for cg in cgs]
        q = [l2n(conv_silu(q_ref, cq_ref, cg)) * (GD_DK ** -0.5) for cg in cgs]
        k = [l2n(conv_silu(k_ref, ck_ref, cg)) for cg in cgs]
        v2 = [conv_silu(v_ref, cv_ref, cg) for cg in cgs]
        ab = [ab_ref[rw, :] for rw in rows]
        k_t = [x.T for x in k]
        g_all = [-jnp.exp(alog_ref[...]) * _softplus(x + dtb_ref[...]) for x in ab]
        beta_all = [jax.nn.sigmoid(x) for x in ab]
        prefix = [_mm(lower, g, "nn", P_CUM) for g in g_all]
        g_kk = [_mm(x, y, "nn", P_GRAM) for x, y in zip(k, k_t)]
        g_qk = [_mm(x, y, "nn", P_GRAM) for x, y in zip(q, k_t)]
        gtot_all = [jnp.sum(g, axis=0, keepdims=True) for g in g_all]
        gc_all = [jnp.where(lane_ab < 64, p, t - p + g) for p, t, g in zip(prefix, gtot_all, g_all)]
        gct_all = [x.T for x in gc_all]
        per = []
        for i in range(n_cu):
            for d in range(2):
                for vl in range(2):
                    col_g = d * 64 + 2 * kh + vl
                    col_b = col_g + n_vh
                    gc_col = jnp.sum(jnp.where(lane_ab == col_g, gc_all[i], 0.0), axis=1, keepdims=True)
                    beta = jnp.sum(jnp.where(lane_ab == col_b, beta_all[i], 0.0), axis=1, keepdims=True)
                    gc_row = jnp.sum(jnp.where(row_abt == col_g, gct_all[i], 0.0), axis=0, keepdims=True)
                    g_last = jnp.sum(jnp.where(lane_1 == col_g, gtot_all[i], 0.0), axis=1, keepdims=True)
                    decay = jnp.where(incl_m[d], jnp.exp(jnp.minimum(gc_col - gc_row, 0.0)), 0.0)
                    a_mat = jnp.where(strict_m[d], beta * g_kk[i] * decay, 0.0)
                    per.append((i, 2 * d + vl, gc_col, beta, gc_row, g_last, decay, a_mat))
        sel = [[n for n, p in enumerate(per) if p[1] // 2 == d] for d in range(2)]
        t_d = _unit_tri_inverse(*[jnp.stack([-per[n][7] for n in sl]) for sl in sel], inv_masks)
        t_all = [None] * len(per)
        for d in range(2):
            for m, n in enumerate(sel[d]):
                t_all[n] = t_d[d][m]
        rhs = [jnp.concatenate([v2[i][:, (j % 2) * GD_DV:(j % 2 + 1) * GD_DV] * beta,
                                k[i] * (beta * jnp.exp(gc_col))], axis=1)
               for i, j, gc_col, beta, _, _, _, _ in per]
        uw = [_mm(t_all[n], x, "nn", P_APPLY) for n, x in enumerate(rhs)]
        for n, (i, j, gc_col, beta, gc_row, g_last, decay, _) in enumerate(per):
            idx = cgs[i] * 4 + j
            u_buf[idx] = uw[n][:, :GD_DV]
            wq_buf[idx, 0:c_len, :] = uw[n][:, GD_DV:].astype(BF16)
            wq_buf[idx, c_len:, :] = (q[i] * jnp.exp(gc_col)).astype(BF16)
            attn_buf[idx] = (g_qk[i] * decay).astype(BF16)
            ket_buf[idx] = (k_t[i] * jnp.exp(g_last - gc_row)).astype(BF16)
            gl_buf[idx] = jnp.broadcast_to(jnp.exp(g_last), (SUBLANES, GD_DV))

    def a_body(i, carry):
        phase_a(n_cu * i)
        return carry

    lax.fori_loop(0, n_tot // n_cu, a_body, 0)

    def b_body(i, carry):
        chains = [(s, d, vl) for s in range(n_sb) for d in range(2) for vl in range(2)]
        cg = [s * n_chunks + (i if d == 0 else n_chunks - 1 - i) for s, d, _ in chains]
        idx = [c * 4 + 2 * d + vl for c, (_, d, vl) in zip(cg, chains)]
        s_mat = [s_ref[s * 4 + 2 * d + vl] for s, d, vl in chains]
        ws_qs = [_mm(wq_buf[n], x, "nn", P_APPLY) for n, x in zip(idx, s_mat)]
        v_new = [(u_buf[n] - x[0:c_len]).astype(BF16) for n, x in zip(idx, ws_qs)]
        o = [x[c_len:] + _mm(attn_buf[n], y, "nn", P_APPLY) for n, x, y in zip(idx, ws_qs, v_new)]
        s_new = [x * gl_buf[n, 0:1, :] + _mm(ket_buf[n], y, "nn", P_APPLY) for n, x, y in zip(idx, s_mat, v_new)]
        for n, (s, d, vl) in enumerate(chains):
            rows = pl.ds(pl.multiple_of(cg[n] * c_len, c_len), c_len)
            obuf[d, rows, pl.ds(vl * GD_DV, GD_DV)] = o[n]
            s_ref[s * 4 + 2 * d + vl] = s_new[n]
        return carry

    lax.fori_loop(0, n_chunks, b_body, 0)

    def c_body(cg, carry):
        rows = pl.ds(pl.multiple_of(cg * c_len, c_len), c_len)
        for vl in range(2):
            cols = pl.ds(vl * GD_DV, GD_DV)
            o = obuf[0, rows, cols] + obuf[1, rows, cols]
            o = o * lax.rsqrt(jnp.mean(o * o, -1, keepdims=True) + EPS) * nw_ref[...] * _silu(z_ref[rows, cols])
            o_ref[rows, cols] = o.astype(o_ref.dtype)
        return carry

    lax.fori_loop(0, n_tot, c_body, 0)
    for s in range(n_sb):
        for j in range(4):
            sfin_ref[s, j // 2, j % 2] = s_ref[s * 4 + j]


def _gdn_scan(proj, ab, conv_w, alog_row, dtb_row, nw, s0, row0, n_seq, t_len, n_sb):
    n_kh = 16
    blk_rows = n_sb * t_len
    assert row0 % blk_rows == 0 and n_seq % n_sb == 0 and (n_sb * t_len // CHUNK) % 2 == 0
    blk0 = row0 // blk_rows
    vw = 2 * GD_DV
    n_units = n_sb * (t_len // CHUNK) * 4
    in_specs = [pl.BlockSpec((blk_rows, GD_DK), lambda b, h: (blk0 + b, h)),
                pl.BlockSpec((blk_rows, GD_DK), lambda b, h: (blk0 + b, n_kh + h)),
                pl.BlockSpec((blk_rows, vw), lambda b, h: (blk0 + b, n_kh + h)),
                pl.BlockSpec((blk_rows, vw), lambda b, h: (blk0 + b, 2 * n_kh + h)),
                pl.BlockSpec((blk_rows, 128), lambda b, h: (blk0 + b, 0)),
                pl.BlockSpec((GD_CONV, GD_DK), lambda b, h: (0, h)),
                pl.BlockSpec((GD_CONV, GD_DK), lambda b, h: (0, n_kh + h)),
                pl.BlockSpec((GD_CONV, vw), lambda b, h: (0, n_kh + h)),
                pl.BlockSpec((1, 128), lambda b, h: (0, 0)),
                pl.BlockSpec((1, 128), lambda b, h: (0, 0)),
                pl.BlockSpec((1, GD_DV), lambda b, h: (0, 0))]
    args = [proj, proj, proj, proj, ab, conv_w, conv_w, conv_w, alog_row, dtb_row, nw]
    if s0 is not None:
        in_specs.append(pl.BlockSpec((n_sb, 2, 2, GD_DK, GD_DV), lambda b, h: (b, 0, h, 0, 0)))
        args.append(s0)
    o, sfin = pl.pallas_call(
        functools.partial(_gdn_kernel, t_len=t_len, n_sb=n_sb, has_s0=s0 is not None),
        grid=(n_seq // n_sb, n_kh),
        in_specs=in_specs,
        out_specs=[pl.BlockSpec((blk_rows, vw), lambda b, h: (b, h)),
                   pl.BlockSpec((n_sb, 2, 2, GD_DK, GD_DV), lambda b, h: (b, 0, h, 0, 0))],
        out_shape=[jax.ShapeDtypeStruct((n_seq * t_len, 2 * n_kh * GD_DV), BF16),
                   jax.ShapeDtypeStruct((n_seq, 2, 2 * n_kh, GD_DK, GD_DV), F32)],
        scratch_shapes=[pltpu.VMEM((n_units, 2 * CHUNK, GD_DV), BF16),
                        pltpu.VMEM((n_units, CHUNK, GD_DV), F32),
                        pltpu.VMEM((n_units, CHUNK, CHUNK), BF16),
                        pltpu.VMEM((n_units, GD_DK, CHUNK), BF16),
                        pltpu.VMEM((n_units, SUBLANES, GD_DV), F32),
                        pltpu.VMEM((2, blk_rows, vw), F32),
                        pltpu.VMEM((4 * n_sb, GD_DK, GD_DV), F32)],
        compiler_params=_cparams("arbitrary", "arbitrary"),
        name=f"gdn_scan_t{t_len}",
    )(*args)
    return o, sfin


def _grid_pos_embed(n_tokens, d_model):
    rows = n_tokens // GRID_W
    row = jnp.broadcast_to(jnp.arange(rows, dtype=F32)[:, None], (rows, GRID_W)).reshape(-1)
    col = jnp.broadcast_to(jnp.arange(GRID_W, dtype=F32)[None, :], (rows, GRID_W)).reshape(-1)
    quarter = d_model // 4
    omega = 1.0 / (POS_BASE ** (jnp.arange(quarter, dtype=F32) / quarter))
    ar = row[:, None] * omega
    ac = col[:, None] * omega
    return jnp.concatenate([jnp.sin(ar), jnp.cos(ar), jnp.sin(ac), jnp.cos(ac)], -1)


def _pad_axis(a, axis, size):
    pad = [(0, 0)] * a.ndim
    pad[axis] = (0, size - a.shape[axis])
    return jnp.pad(a, pad)


def kernel(x_prompt, x_sample, state_rwkv, state_gdn, c, c_ctx, norm_mix, norm_ffn, norm_final, w_mod, b_mod, ffn_w1, ffn_w3, ffn_w2, rw_mu, rw_wr, rw_wk, rw_wv, rw_wo, rw_w0, rw_w1, rw_w2, rw_a0, rw_a1, rw_a2, rw_g1, rw_g2, rw_kk, rw_ka, rw_rk, rw_ln_w, rw_ln_b, gd_w_in, gd_conv, gd_a_log, gd_dt_bias, gd_norm, gd_w_out):
    n_p, t_p, d = x_prompt.shape
    n_s, t_s, _ = x_sample.shape
    rows_p = n_p * t_p
    bf = lambda a: a.astype(BF16)

    x = jnp.concatenate([x_prompt.reshape(rows_p, d),
                         (x_sample + _grid_pos_embed(t_s, d)[None]).reshape(n_s * t_s, d)], axis=0)
    regions = [(n_p, t_p, [0] * n_p), (n_s, t_s, list(range(1, n_s + 1)))]
    tm_shift = 256
    tm_big = 512
    tables = _tile_tables(regions, tm_shift)
    grp_big = _tile_tables(regions, tm_big)[0]

    cond = _pad_axis(jnp.concatenate([c_ctx[None], c], axis=0), 0, SUBLANES)
    mod = _modulation(cond, w_mod, b_mod).reshape(w_mod.shape[0], SUBLANES, 6, d)

    row = lambda a: a.reshape(1, -1)
    w_rkv = bf(jnp.stack([rw_wr[0], rw_wk[0], rw_wv[0]]))
    lp = RW_LORA_PAD
    rkv, gate, lw, a = _rwkv_inputs(
        x, mod[0], tables, row(norm_mix[0]), rw_mu[0], w_rkv,
        bf(rw_g1[0]), bf(rw_g2[0]),
        bf(_pad_axis(rw_w1[0], 2, lp)), bf(_pad_axis(rw_w2[0], 1, lp)),
        bf(_pad_axis(rw_a1[0], 2, lp)), bf(_pad_axis(rw_a2[0], 1, lp)),
        rw_w0[0][:, None, :], rw_a0[0][:, None, :], tm_shift)
    scan_args = (rkv, gate, lw, a, row(rw_kk[0]), row(rw_ka[0]), row(rw_rk[0]), row(rw_ln_w[0]), row(rw_ln_b[0]))
    y_p, s_rwkv = _rwkv_scan(*scan_args, None, 0, n_p, t_p, 2)
    s0_rw = state_rwkv[:, 0].reshape(n_s, 2, d, RW_HEAD)
    y_s, _ = _rwkv_scan(*scan_args, s0_rw, rows_p, n_s, t_s, 1)
    y = jnp.concatenate([y_p, y_s], axis=0)
    x = _out_proj(y, bf(rw_wo[0]), x, mod[0], grp_big, 2, tm_big)
    x = _ffn(x, mod[0], grp_big, row(norm_ffn[0]), bf(ffn_w1[0]), bf(ffn_w3[0]), bf(ffn_w2[0]),
             row(norm_final), False, tm_big)

    n_main = 3 * 4096
    proj = _in_proj(x, mod[1], grp_big, row(norm_mix[1]), bf(gd_w_in[0][:, :n_main]), tm_big, 1024)
    ab = _in_proj(x, mod[1], grp_big, row(norm_mix[1]), bf(gd_w_in[0][:, n_main:]), tm_big, 128)
    zeros32 = jnp.zeros((2, 32), F32)
    alog_row = jnp.stack([gd_a_log[0], zeros32], axis=1).reshape(1, 128)
    dtb_row = jnp.stack([gd_dt_bias[0], zeros32], axis=1).reshape(1, 128)
    gd_args = (proj, ab, gd_conv[0], alog_row, dtb_row, row(gd_norm[0]))
    o_p, s_gdn = _gdn_scan(*gd_args, None, 0, n_p, t_p, 2)
    o_s, _ = _gdn_scan(*gd_args, state_gdn[:, 0], rows_p, n_s, t_s, 1)
    o = jnp.concatenate([o_p, o_s], axis=0)
    x = _out_proj(o, bf(gd_w_out[0]), x, mod[1], grp_big, 2, tm_big)
    x = _ffn(x, mod[1], grp_big, row(norm_ffn[1]), bf(ffn_w1[1]), bf(ffn_w3[1]), bf(ffn_w2[1]),
             row(norm_final), True, tm_big)

    y_prompt = x[:rows_p].reshape(n_p, t_p, d)
    y_sample = x[rows_p:].reshape(n_s, t_s, d)
    new_state_rwkv = s_rwkv.reshape(n_p, 1, 2, d // RW_HEAD, RW_HEAD, RW_HEAD)
    new_state_gdn = s_gdn.reshape(n_p, 1, 2, 32, GD_DK, GD_DV)
    return (y_prompt, y_sample, new_state_rwkv, new_state_gdn)
```

```python
import functools

import numpy as np
import jax
import jax.numpy as jnp
from jax import lax
from jax.experimental import pallas as pl
from jax.experimental.pallas import tpu as pltpu

F32 = jnp.float32
BF16 = jnp.bfloat16

EPS = 1e-6
GRID_W = 64
POS_BASE = 10000.0
RW_HEAD = 64
RW_GROUP_HEADS = 4
RW_LANES = RW_HEAD * RW_GROUP_HEADS
RW_LORA_PAD = 128
GD_DK = 128
GD_DV = 128
GD_CONV = 4
CHUNK = 64
RW_SUPER_CHUNKS = 8
SUBLANES = 8
VMEM_LIMIT = 56 * 1024 * 1024


def _cparams(*sem):
    return pltpu.CompilerParams(dimension_semantics=sem, vmem_limit_bytes=VMEM_LIMIT)


def _dot(a, b):
    return jnp.dot(a.astype(BF16), b.astype(BF16), preferred_element_type=F32)


_DIMS = {"nn": (((1,), (0,)), ((), ())),
         "nt": (((1,), (1,)), ((), ())),
         "tn": (((0,), (0,)), ((), ())),
         "bnn": (((2,), (1,)), ((0,), (0,)))}


def _split_bf16(x, pieces):
    out = []
    for _ in range(pieces - 1):
        p = x.astype(BF16)
        out.append(p)
        x = x - p.astype(F32)
    out.append(x.astype(BF16))
    return out


def _mm(a, b, dims="nn", mode="b"):
    dn = _DIMS[dims]
    dg = lambda x, y: lax.dot_general(x, y, dn, preferred_element_type=F32)
    if mode == "b":
        return dg(a.astype(BF16), b.astype(BF16))
    if mode in ("la", "la2"):
        a0 = a.astype(BF16)
        return sum(dg(a0, p) for p in _split_bf16(b, 3 if mode == "la" else 2))
    if mode in ("ra", "ra2"):
        b0 = b.astype(BF16)
        return sum(dg(p, b0) for p in _split_bf16(a, 3 if mode == "ra" else 2))
    raise ValueError(mode)


P_INV = "b"
P_GRAM = "b"
P_APPLY = "b"
P_SUM = "ra2"
P_CUM = "la2"


def _iota(shape, dim):
    return lax.broadcasted_iota(jnp.int32, shape, dim)


def _silu(x):
    return x * jax.nn.sigmoid(x)


def _softplus(x):
    return jnp.maximum(x, 0.0) + jnp.log(1.0 + jnp.exp(-jnp.abs(x)))


def _rms_mod(x, nw, sh, sc):
    y = x * lax.rsqrt(jnp.mean(x * x, -1, keepdims=True) + EPS)
    return (y * nw) * (1.0 + sc) + sh


def _inverse_masks(c_len):
    ij = _iota((c_len, c_len), 0) ^ _iota((c_len, c_len), 1)
    levels = []
    half = 2
    while half < c_len:
        levels.append(((ij >= half) & (ij < 2 * half)).astype(BF16))
        half *= 2
    return (ij == 0).astype(F32), (ij == 1).astype(F32), levels


def _unit_tri_inverse(n_lower, n_upper, masks):
    eye, pair, levels = masks
    mats = [n_lower, n_upper]
    c_len = n_lower.shape[-1]
    d_mats = [eye + m * pair for m in mats]
    n_bf = [m.astype(BF16) for m in mats]
    half = 2
    for lvl in levels:
        l_mats = [m * lvl for m in n_bf]
        if half < SUBLANES:
            dl = [_mm(d, l, "bnn", P_INV) for d, l in zip(d_mats, l_mats)]
            d_mats = [d + _mm(x, d, "bnn", P_INV) for d, x in zip(d_mats, dl)]
        else:
            first = [slice(b0, b0 + half) for b0 in range(0, c_len, 2 * half)]
            second = [slice(b0 + half, b0 + 2 * half) for b0 in range(0, c_len, 2 * half)]
            act = [second, first]
            d_act = [jnp.concatenate([d[:, sl, :] for sl in rows], axis=1) for d, rows in zip(d_mats, act)]
            dl = [_mm(x, l, "bnn", P_INV) for x, l in zip(d_act, l_mats)]
            upd = [x + _mm(y, d, "bnn", P_INV) for x, y, d in zip(d_act, dl, d_mats)]
            out = []
            for g, (d, u) in enumerate(zip(d_mats, upd)):
                parts = []
                for n in range(len(first)):
                    new = u[:, n * half:(n + 1) * half, :]
                    parts += [d[:, first[n], :], new] if g == 0 else [new, d[:, second[n], :]]
                out.append(jnp.concatenate(parts, axis=1))
            d_mats = out
        half *= 2
    return d_mats


def _mod_kernel(c_ref, w_ref, b_ref, o_ref):
    o_ref[0] = _dot(_silu(c_ref[...]), w_ref[0]) + b_ref[0]


def _modulation(cond, w_mod, b_mod):
    depth, d, n = w_mod.shape
    tn = 1024
    return pl.pallas_call(
        _mod_kernel,
        grid=(depth, n // tn),
        in_specs=[pl.BlockSpec((SUBLANES, d), lambda l, j: (0, 0)),
                  pl.BlockSpec((1, d, tn), lambda l, j: (l, 0, j)),
                  pl.BlockSpec((1, 1, tn), lambda l, j: (l, 0, j))],
        out_specs=pl.BlockSpec((1, SUBLANES, tn), lambda l, j: (l, 0, j)),
        out_shape=jax.ShapeDtypeStruct((depth, SUBLANES, n), F32),
        compiler_params=_cparams("arbitrary", "arbitrary"),
        name="modulation",
    )(cond, w_mod, b_mod.reshape(depth, 1, n))


def _shift_mix(x_ref, xp_ref, xn_ref, mod_ref, nw_ref, has_prev, has_next):
    nw = nw_ref[...]
    sh = mod_ref[0, 0:1, :]
    sc = mod_ref[0, 1:2, :]
    h = _rms_mod(x_ref[...], nw, sh, sc)
    tm = h.shape[0]
    hp = _rms_mod(xp_ref[SUBLANES - 1:SUBLANES, :], nw, sh, sc) * has_prev
    hn = _rms_mod(xn_ref[0:1, :], nw, sh, sc) * has_next
    rows = _iota(h.shape, 0)
    prev = jnp.where(rows == 0, hp, pltpu.roll(h, 1, 0))
    nxt = jnp.where(rows == tm - 1, hn, pltpu.roll(h, tm - 1, 0))
    return h, 0.5 * (prev + nxt) - h


def _rkv_kernel(grp_ref, hp_ref, hn_ref, x_ref, xp_ref, xn_ref, mod_ref, nw_ref, mu_ref, w_ref, o_ref):
    i = pl.program_id(1)
    h, xx = _shift_mix(x_ref, xp_ref, xn_ref, mod_ref, nw_ref,
                       hp_ref[i].astype(F32), hn_ref[i].astype(F32))
    o_ref[0] = _dot(h + xx * mu_ref[0], w_ref[0])


def _lora_kernel(grp_ref, hp_ref, hn_ref, x_ref, xp_ref, xn_ref, mod_ref, nw_ref, mu_ref,
                 g1_ref, g2_ref, w1_ref, w2_ref, a1_ref, a2_ref, w0_ref, a0_ref,
                 gate_ref, lw_ref, a_ref):
    i = pl.program_id(0)
    h, xx = _shift_mix(x_ref, xp_ref, xn_ref, mod_ref, nw_ref,
                       hp_ref[i].astype(F32), hn_ref[i].astype(F32))
    xw = h + xx * mu_ref[1:2, :]
    xa = h + xx * mu_ref[4:5, :]
    xg = h + xx * mu_ref[5:6, :]
    gate_ref[...] = _dot(jax.nn.sigmoid(_dot(xg, g1_ref[...])), g2_ref[...])
    for d in range(2):
        lw = jnp.tanh(_dot(xw, w1_ref[d]))
        w_log = -_softplus(-(w0_ref[d] + _dot(lw, w2_ref[d]))) - 0.5
        lw_ref[d] = -jnp.exp(w_log)
        a_ref[d] = jax.nn.sigmoid(a0_ref[d] + _dot(_dot(xa, a1_ref[d]), a2_ref[d]))


def _tile_tables(regions, tm):
    row_grp, row_pos, row_len = [], [], []
    for n_seq, t_len, groups in regions:
        for s in range(n_seq):
            row_grp += [groups[s]] * t_len
            row_pos += list(range(t_len))
            row_len += [t_len] * t_len
    row_grp, row_pos, row_len = (np.asarray(v).reshape(-1, tm) for v in (row_grp, row_pos, row_len))
    assert (row_grp == row_grp[:, :1]).all(), "a row tile must not straddle modulation groups"
    as_i32 = lambda v: jnp.asarray(np.asarray(v, np.int32))
    return (as_i32(row_grp[:, 0]), as_i32(row_pos[:, 0] > 0),
            as_i32(row_pos[:, -1] < row_len[:, -1] - 1))


def _halo_specs(tm, d, m_rows, n_lead):
    blocks = tm // SUBLANES
    last = m_rows // SUBLANES - 1
    if n_lead == 1:
        cur = lambda j, i, *_: (i, 0)
        prev = lambda j, i, *_: (jnp.maximum(i * blocks - 1, 0), 0)
        nxt = lambda j, i, *_: (jnp.minimum((i + 1) * blocks, last), 0)
    else:
        cur = lambda i, *_: (i, 0)
        prev = lambda i, *_: (jnp.maximum(i * blocks - 1, 0), 0)
        nxt = lambda i, *_: (jnp.minimum((i + 1) * blocks, last), 0)
    return [pl.BlockSpec((tm, d), cur), pl.BlockSpec((SUBLANES, d), prev), pl.BlockSpec((SUBLANES, d), nxt)]


def _rwkv_inputs(x, mod, tables, nw, mu, w_rkv, g1, g2, w1, w2, a1, a2, w0, a0, tm):
    m_rows, d = x.shape
    n_tiles = m_rows // tm
    mu_rkv = jnp.stack([mu[0], mu[2], mu[3]])[:, None, :]
    rkv = pl.pallas_call(
        _rkv_kernel,
        grid_spec=pltpu.PrefetchScalarGridSpec(
            num_scalar_prefetch=3, grid=(3, n_tiles),
            in_specs=_halo_specs(tm, d, m_rows, 1) + [
                pl.BlockSpec((1, 6, d), lambda j, i, g, *_: (g[i], 0, 0)),
                pl.BlockSpec((1, d), lambda j, i, *_: (0, 0)),
                pl.BlockSpec((1, 1, d), lambda j, i, *_: (j, 0, 0)),
                pl.BlockSpec((1, d, d), lambda j, i, *_: (j, 0, 0))],
            out_specs=pl.BlockSpec((1, tm, d), lambda j, i, *_: (j, i, 0))),
        out_shape=jax.ShapeDtypeStruct((3, m_rows, d), F32),
        compiler_params=_cparams("arbitrary", "arbitrary"),
        name="rwkv_rkv_proj",
    )(*tables, x, x, x, mod, nw, mu_rkv, w_rkv)

    full = lambda a: pl.BlockSpec(a.shape, lambda i, *_: (0,) * a.ndim)
    small = [g1, g2, w1, w2, a1, a2, w0, a0]
    gate, lw, a = pl.pallas_call(
        _lora_kernel,
        grid_spec=pltpu.PrefetchScalarGridSpec(
            num_scalar_prefetch=3, grid=(n_tiles,),
            in_specs=_halo_specs(tm, d, m_rows, 0) + [
                pl.BlockSpec((1, 6, d), lambda i, g, *_: (g[i], 0, 0)),
                pl.BlockSpec((1, d), lambda i, *_: (0, 0)),
                full(mu)] + [full(s) for s in small],
            out_specs=[pl.BlockSpec((tm, d), lambda i, *_: (i, 0)),
                       pl.BlockSpec((2, tm, d), lambda i, *_: (0, i, 0)),
                       pl.BlockSpec((2, tm, d), lambda i, *_: (0, i, 0))]),
        out_shape=[jax.ShapeDtypeStruct((m_rows, d), F32),
                   jax.ShapeDtypeStruct((2, m_rows, d), F32),
                   jax.ShapeDtypeStruct((2, m_rows, d), F32)],
        compiler_params=_cparams("arbitrary"),
        name="rwkv_lora_proj",
    )(*tables, x, x, x, mod, nw, mu, *small)
    return rkv, gate, lw, a


def _rwkv_scan_kernel(*refs, t_len, n_sb, has_s0):
    (r_ref, k_ref, v_ref, gate_ref, lw_ref, a_ref, kk_ref, ka_ref, rk_ref, lnw_ref, lnb_ref) = refs[:11]
    rest = refs[11:]
    if has_s0:
        s0_ref, rest = rest[0], rest[1:]
    (y_ref, sfin_ref, r2_buf, a2_buf, uv_buf, bk_buf, y0_buf, w_buf, ybuf, bonus_buf, s_ref) = rest
    c_len, lanes, heads = CHUNK, RW_LANES, RW_GROUP_HEADS
    n_chunks = t_len // c_len
    sc = min(n_chunks, RW_SUPER_CHUNKS)
    n_super = n_chunks // sc
    n_pp = max(1, 8 // (2 * n_sb))
    stack = heads * c_len

    same_head = (_iota((stack, lanes), 0) // c_len == _iota((stack, lanes), 1) // RW_HEAD).astype(F32)
    ones_blk = same_head
    rep = (_iota((RW_HEAD, lanes), 1) % RW_HEAD == _iota((RW_HEAD, lanes), 0)).astype(F32)
    t_idx = _iota((stack, c_len), 0) % c_len
    s_idx = _iota((stack, c_len), 1)
    tri_t = _iota((c_len, c_len), 0)
    tri_s = _iota((c_len, c_len), 1)

    inv_masks = _inverse_masks(c_len)
    kk_p, ka_p, rk_p = kk_ref[...], ka_ref[...], rk_ref[...]
    lnw, lnb = lnw_ref[...], lnb_ref[...]

    def fold(z):
        z = z * same_head
        return z[0:c_len] + z[c_len:2 * c_len] + z[2 * c_len:3 * c_len] + z[3 * c_len:4 * c_len]

    def tile4(z):
        return jnp.concatenate([z, z, z, z], axis=0) * same_head

    for s in range(n_sb):
        for d in range(2):
            if has_s0:
                s_ref[s * 2 + d] = _mm(s0_ref[s, d], rep, "nn", "ra") * ones_blk
            else:
                s_ref[s * 2 + d] = jnp.zeros((lanes, lanes), F32)
    ybuf[...] = jnp.zeros_like(ybuf)
    bonus_buf[...] = jnp.zeros_like(bonus_buf)

    def chunk_rows(s, d, p):
        c = p if d == 0 else n_chunks - 1 - p
        return pl.ds(pl.multiple_of((s * n_chunks + c) * c_len, c_len), c_len)

    incl_f = [(tri_s <= tri_t).astype(F32), (tri_s >= tri_t).astype(F32)]
    incl4 = [s_idx <= t_idx, s_idx >= t_idx]
    strict4 = [s_idx < t_idx, s_idx > t_idx]

    def phase_a(p0, j0):
        units = [(s, d, q) for s in range(n_sb) for d in range(2) for q in range(n_pp)]
        dirs = [d for _, d, _ in units]
        rows = [chunk_rows(s, d, p0 + q) for s, d, q in units]
        r = [r_ref[rw, :] for rw in rows]
        k = [k_ref[rw, :] for rw in rows]
        v = [v_ref[rw, :] for rw in rows]
        lw = [lw_ref[d, rw, :] for d, rw in zip(dirs, rows)]
        a = [a_ref[d, rw, :] for d, rw in zip(dirs, rows)]
        bonus_old = [bonus_buf[rw, :] for rw in rows]
        kx = [ki * kk_p for ki in k]
        ss = [_mm(x * x, ones_blk, "nn", P_SUM) for x in kx]
        cw = [_mm(incl_f[d], x, "nn", P_CUM) for d, x in zip(dirs, lw)]
        bsum = [_mm(ri * ki * (1.0 + (ai - 1.0) * ka_p) * rk_p, ones_blk, "nn", P_SUM)
                for ri, ki, ai in zip(r, k, a)]
        kkn = [x * lax.rsqrt(y + EPS) for x, y in zip(kx, ss)]
        kd = [ki * (1.0 + (ai - 1.0) * ka_p) for ki, ai in zip(k, a)]
        w_row = [jnp.exp(jnp.sum(x, axis=0, keepdims=True)) for x in lw]
        e_neg = [jnp.exp(-x) for x in cw]
        at = [-kn * jnp.exp(c - l) for kn, c, l in zip(kkn, cw, lw)]
        bt = [kn * ai * e for kn, ai, e in zip(kkn, a, e_neg)]
        kt = [x * e for x, e in zip(kd, e_neg)]
        rt = [ri * jnp.exp(c) for ri, c in zip(r, cw)]
        lhs = [jnp.concatenate([tile4(x), tile4(y)], axis=0) for x, y in zip(at, rt)]
        g_b = [_mm(x, y, "nt", P_GRAM) for x, y in zip(lhs, bt)]
        g_k = [_mm(x, y, "nt", P_GRAM) for x, y in zip(lhs, kt)]
        n_s = [jnp.where(strict4[d], g[0:stack], 0.0) for d, g in zip(dirs, g_b)]
        ak_s = [jnp.where(strict4[d], g[0:stack], 0.0) for d, g in zip(dirs, g_k)]
        rb_s = [jnp.where(incl4[d], g[stack:], 0.0) for d, g in zip(dirs, g_b)]
        rk_s = [jnp.where(incl4[d], g[stack:], 0.0) for d, g in zip(dirs, g_k)]
        sel = [[i for i, di in enumerate(dirs) if di == d] for d in range(2)]
        t_d = _unit_tri_inverse(*[jnp.concatenate([n_s[i].reshape(heads, c_len, c_len) for i in sl], axis=0)
                                  for sl in sel], inv_masks)
        t_s = [None] * len(units)
        for d in range(2):
            for n, i in enumerate(sel[d]):
                t_s[i] = t_d[d][n * heads:(n + 1) * heads].reshape(stack, c_len)
        a2 = [fold(_mm(t, x, "nn", P_APPLY)) for t, x in zip(t_s, at)]
        akv = [fold(_mm(x, vi, "nn", P_APPLY)) for x, vi in zip(ak_s, v)]
        rkv = [_mm(x, vi, "nn", P_APPLY) for x, vi in zip(rk_s, v)]
        u0 = [fold(_mm(t, x, "nn", P_APPLY)) for t, x in zip(t_s, akv)]
        r2 = [x + fold(_mm(rb, y, "nn", P_APPLY)) for x, rb, y in zip(rt, rb_s, a2)]
        y0 = [fold(_mm(rb, x, "nn", P_APPLY) + y) for rb, x, y in zip(rb_s, u0, rkv)]
        for i, (s, d, q) in enumerate(units):
            slot = (s * 2 + d) * sc + j0 + q
            r2_buf[slot] = r2[i].astype(BF16)
            a2_buf[slot] = a2[i].astype(BF16)
            uv_buf[slot, 0:c_len, :] = u0[i].astype(BF16)
            uv_buf[slot, c_len:, :] = v[i].astype(BF16)
            bk_buf[slot, 0:c_len, :] = (bt[i] * w_row[i]).astype(BF16)
            bk_buf[slot, c_len:, :] = (kt[i] * w_row[i]).astype(BF16)
            y0_buf[slot] = y0[i]
            w_buf[slot] = jnp.broadcast_to(w_row[i], (SUBLANES, lanes))
            bonus_buf[rows[i], :] = bonus_old[i] + bsum[i] * v[i]

    def phase_b(p, j):
        units = [(s, d) for s in range(n_sb) for d in range(2)]
        rows = [chunk_rows(s, d, p) for s, d in units]
        slots = [(s * 2 + d) * sc + j for s, d in units]
        s_mat = [s_ref[s * 2 + d] for s, d in units]
        y_old = [ybuf[rw, :] for rw in rows]
        s_bf = [x.astype(BF16) for x in s_mat]
        m_s = [_mm(a2_buf[sl], bk_buf[sl, 0:c_len, :], "tn", P_APPLY) * ones_blk for sl in slots]
        s_add = [_mm(uv_buf[sl], bk_buf[sl], "tn", P_APPLY) * ones_blk for sl in slots]
        y = [_mm(r2_buf[sl], x, "nt", P_APPLY) + y0_buf[sl] for sl, x in zip(slots, s_bf)]
        s_new = [x * w_buf[sl, 0:1, :] + _mm(xb, m, "nn", P_APPLY) + z
                 for x, xb, m, z, sl in zip(s_mat, s_bf, m_s, s_add, slots)]
        for i, (s, d) in enumerate(units):
            ybuf[rows[i], :] = y_old[i] + y[i]
            s_ref[s * 2 + d] = s_new[i]

    def super_body(sp, carry):
        def a_body(j, c2):
            phase_a(sp * sc + j * n_pp, j * n_pp)
            return c2

        def b_body(j, c2):
            phase_b(sp * sc + j, j)
            return c2

        lax.fori_loop(0, sc // n_pp, a_body, 0)
        lax.fori_loop(0, sc, b_body, 0)
        return carry

    lax.fori_loop(0, n_super, super_body, 0)

    n_cc = 4

    def c_body(i, carry):
        rows = [pl.ds(pl.multiple_of((i * n_cc + q) * c_len, c_len), c_len) for q in range(n_cc)]
        y = [ybuf[rw, :] for rw in rows]
        mean = [_mm(x, ones_blk, "nn", P_SUM) * (1.0 / RW_HEAD) for x in y]
        yc = [x - m for x, m in zip(y, mean)]
        var = [_mm(x * x, ones_blk, "nn", P_SUM) * (1.0 / RW_HEAD) for x in yc]
        for rw, x, vr in zip(rows, yc, var):
            yn = x * lax.rsqrt(vr + RW_HEAD * 1e-5) * lnw + lnb
            y_ref[rw, :] = ((yn + bonus_buf[rw, :]) * gate_ref[rw, :]).astype(y_ref.dtype)
        return carry

    lax.fori_loop(0, n_sb * n_chunks // n_cc, c_body, 0)
    rep_t = (_iota((lanes, RW_HEAD), 0) % RW_HEAD == _iota((lanes, RW_HEAD), 1)).astype(F32)
    for s in range(n_sb):
        for d in range(2):
            sfin_ref[s, d] = _mm(s_ref[s * 2 + d], rep_t, "nn", "ra")


def _rwkv_scan(rkv, gate, lw, a, kk, ka, rk, lnw, lnb, s0, row0, n_seq, t_len, n_sb):
    _, m_rows, d = rkv.shape
    n_groups = d // RW_LANES
    blk_rows = n_sb * t_len
    assert row0 % blk_rows == 0 and n_seq % n_sb == 0
    blk0 = row0 // blk_rows
    sc = min(t_len // CHUNK, RW_SUPER_CHUNKS)
    assert (t_len // CHUNK) % sc == 0
    n_slots = n_sb * 2 * sc
    tok = lambda b, g: (blk0 + b, g)
    in_specs = [pl.BlockSpec((None, blk_rows, RW_LANES), lambda b, g, j=j: (j, blk0 + b, g)) for j in range(3)]
    in_specs += [pl.BlockSpec((blk_rows, RW_LANES), tok),
                 pl.BlockSpec((2, blk_rows, RW_LANES), lambda b, g: (0, blk0 + b, g)),
                 pl.BlockSpec((2, blk_rows, RW_LANES), lambda b, g: (0, blk0 + b, g))]
    in_specs += [pl.BlockSpec((1, RW_LANES), lambda b, g: (0, g))] * 5
    args = [rkv, rkv, rkv, gate, lw, a, kk, ka, rk, lnw, lnb]
    if s0 is not None:
        in_specs.append(pl.BlockSpec((n_sb, 2, RW_LANES, RW_HEAD), lambda b, g: (b, 0, g, 0)))
        args.append(s0)
    y, sfin = pl.pallas_call(
        functools.partial(_rwkv_scan_kernel, t_len=t_len, n_sb=n_sb, has_s0=s0 is not None),
        grid=(n_seq // n_sb, n_groups),
        in_specs=in_specs,
        out_specs=[pl.BlockSpec((blk_rows, RW_LANES), lambda b, g: (b, g)),
                   pl.BlockSpec((n_sb, 2, RW_LANES, RW_HEAD), lambda b, g: (b, 0, g, 0))],
        out_shape=[jax.ShapeDtypeStruct((n_seq * t_len, d), BF16),
                   jax.ShapeDtypeStruct((n_seq, 2, d, RW_HEAD), F32)],
        scratch_shapes=[pltpu.VMEM((n_slots, CHUNK, RW_LANES), BF16),
                        pltpu.VMEM((n_slots, CHUNK, RW_LANES), BF16),
                        pltpu.VMEM((n_slots, 2 * CHUNK, RW_LANES), BF16),
                        pltpu.VMEM((n_slots, 2 * CHUNK, RW_LANES), BF16),
                        pltpu.VMEM((n_slots, CHUNK, RW_LANES), F32),
                        pltpu.VMEM((n_slots, SUBLANES, RW_LANES), F32),
                        pltpu.VMEM((blk_rows, RW_LANES), F32),
                        pltpu.VMEM((blk_rows, RW_LANES), F32),
                        pltpu.VMEM((2 * n_sb, RW_LANES, RW_LANES), F32)],
        compiler_params=_cparams("arbitrary", "arbitrary"),
        name=f"rwkv_scan_t{t_len}",
    )(*args)
    return y, sfin


def _out_proj_kernel(grp_ref, y_ref, w_ref, x_ref, mod_ref, o_ref, *, gate_row):
    o_ref[...] = x_ref[...] + mod_ref[0, gate_row:gate_row + 1, :] * _dot(y_ref[...], w_ref[...])


def _out_proj(y, w, x, mod, grp, gate_row, tm):
    m_rows, k_dim = y.shape
    d = w.shape[1]
    tn = 1024
    return pl.pallas_call(
        functools.partial(_out_proj_kernel, gate_row=gate_row),
        grid_spec=pltpu.PrefetchScalarGridSpec(
            num_scalar_prefetch=1, grid=(d // tn, m_rows // tm),
            in_specs=[pl.BlockSpec((tm, k_dim), lambda n, i, g: (i, 0)),
                      pl.BlockSpec((k_dim, tn), lambda n, i, g: (0, n)),
                      pl.BlockSpec((tm, tn), lambda n, i, g: (i, n)),
                      pl.BlockSpec((1, 6, tn), lambda n, i, g: (g[i], 0, n))],
            out_specs=pl.BlockSpec((tm, tn), lambda n, i, g: (i, n))),
        out_shape=jax.ShapeDtypeStruct((m_rows, d), F32),
        compiler_params=_cparams("arbitrary", "arbitrary"),
        name="out_proj",
    )(grp, y, w, x, mod)


def _in_proj_kernel(grp_ref, x_ref, mod_ref, nw_ref, w_ref, o_ref):
    h = _rms_mod(x_ref[...], nw_ref[...], mod_ref[0, 0:1, :], mod_ref[0, 1:2, :])
    o_ref[...] = _dot(h, w_ref[...])


def _in_proj(x, mod, grp, nw, w, tm, tn):
    m_rows, d = x.shape
    n = w.shape[1]
    return pl.pallas_call(
        _in_proj_kernel,
        grid_spec=pltpu.PrefetchScalarGridSpec(
            num_scalar_prefetch=1, grid=(n // tn, m_rows // tm),
            in_specs=[pl.BlockSpec((tm, d), lambda n_, i, g: (i, 0)),
                      pl.BlockSpec((1, 6, d), lambda n_, i, g: (g[i], 0, 0)),
                      pl.BlockSpec((1, d), lambda n_, i, g: (0, 0)),
                      pl.BlockSpec((d, tn), lambda n_, i, g: (0, n_))],
            out_specs=pl.BlockSpec((tm, tn), lambda n_, i, g: (i, n_))),
        out_shape=jax.ShapeDtypeStruct((m_rows, n), F32),
        compiler_params=_cparams("arbitrary", "arbitrary"),
        name="gdn_in_proj",
    )(grp, x, mod, nw, w)


def _ffn_kernel(grp_ref, x_ref, mod_ref, nw_ref, w1_ref, w3_ref, w2_ref, fw_ref, o_ref, h_buf, acc,
                *, final_norm):
    f = pl.program_id(1)

    @pl.when(f == 0)
    def _():
        h_buf[...] = _rms_mod(x_ref[...], nw_ref[...], mod_ref[0, 3:4, :], mod_ref[0, 4:5, :]).astype(BF16)
        acc[...] = jnp.zeros_like(acc)

    h = h_buf[...]
    gate = jnp.dot(h, w1_ref[...], preferred_element_type=F32)
    up = jnp.dot(h, w3_ref[...], preferred_element_type=F32)
    acc[...] += _dot(_silu(gate) * up, w2_ref[...])

    @pl.when(f == pl.num_programs(1) - 1)
    def _():
        y = x_ref[...] + mod_ref[0, 5:6, :] * acc[...]
        if final_norm:
            y = y * lax.rsqrt(jnp.mean(y * y, -1, keepdims=True) + EPS) * fw_ref[...]
        o_ref[...] = y


def _ffn(x, mod, grp, nw, w1, w3, w2, fw, final_norm, tm):
    m_rows, d = x.shape
    d_ff = w1.shape[1]
    tf = 512
    return pl.pallas_call(
        functools.partial(_ffn_kernel, final_norm=final_norm),
        grid_spec=pltpu.PrefetchScalarGridSpec(
            num_scalar_prefetch=1, grid=(m_rows // tm, d_ff // tf),
            in_specs=[pl.BlockSpec((tm, d), lambda i, f, g: (i, 0)),
                      pl.BlockSpec((1, 6, d), lambda i, f, g: (g[i], 0, 0)),
                      pl.BlockSpec((1, d), lambda i, f, g: (0, 0)),
                      pl.BlockSpec((d, tf), lambda i, f, g: (0, f)),
                      pl.BlockSpec((d, tf), lambda i, f, g: (0, f)),
                      pl.BlockSpec((tf, d), lambda i, f, g: (f, 0)),
                      pl.BlockSpec((1, d), lambda i, f, g: (0, 0))],
            out_specs=pl.BlockSpec((tm, d), lambda i, f, g: (i, 0)),
            scratch_shapes=[pltpu.VMEM((tm, d), BF16), pltpu.VMEM((tm, d), F32)]),
        out_shape=jax.ShapeDtypeStruct((m_rows, d), F32),
        compiler_params=_cparams("arbitrary", "arbitrary"),
        name="ffn",
    )(grp, x, mod, nw, w1, w3, w2, fw)


def _gdn_kernel(*refs, t_len, n_sb, has_s0):
    (q_ref, k_ref, v_ref, z_ref, ab_ref, cq_ref, ck_ref, cv_ref, alog_ref, dtb_ref, nw_ref) = refs[:11]
    rest = refs[11:]
    if has_s0:
        s0_ref, rest = rest[0], rest[1:]
    o_ref, sfin_ref, wq_buf, u_buf, attn_buf, ket_buf, gl_buf, obuf, s_ref, ext_buf = rest
    c_len = CHUNK
    n_chunks = t_len // c_len
    n_tot = n_sb * n_chunks
    last_row0 = n_sb * t_len - SUBLANES
    kh = pl.program_id(1)
    n_vh = 32
    tri_t = _iota((c_len, c_len), 0)
    tri_s = _iota((c_len, c_len), 1)
    lane_ab = _iota((c_len, 128), 1)
    row_abt = _iota((128, c_len), 0)
    lane_1 = _iota((1, 128), 1)
    lower = (tri_s <= tri_t).astype(F32)
    inv_masks = _inverse_masks(c_len)

    for s in range(n_sb):
        for j in range(4):
            if has_s0:
                s_ref[s * 4 + j] = s0_ref[s, j // 2, j % 2]
            else:
                s_ref[s * 4 + j] = jnp.zeros((GD_DK, GD_DV), F32)

    def conv_silu(ref, w_ref, cg):
        c = cg % n_chunks
        r0 = pl.multiple_of(cg * c_len, c_len)
        main = ref[pl.ds(r0, c_len), :]
        up0 = pl.multiple_of(jnp.maximum(r0 - SUBLANES, 0), SUBLANES)
        dn0 = pl.multiple_of(jnp.minimum(r0 + c_len, last_row0), SUBLANES)
        up = ref[pl.ds(up0, SUBLANES), :] * jnp.where(c > 0, 1.0, 0.0)
        dn = ref[pl.ds(dn0, SUBLANES), :] * jnp.where(c < n_chunks - 1, 1.0, 0.0)
        wd = main.shape[1]
        ext_buf[0:SUBLANES, 0:wd] = up
        ext_buf[SUBLANES:SUBLANES + c_len, 0:wd] = main
        ext_buf[SUBLANES + c_len:2 * SUBLANES + c_len, 0:wd] = dn
        w = w_ref[...]
        acc = main * w[1:2]
        for j in (0, 2, 3):
            acc = acc + ext_buf[SUBLANES - 1 + j:SUBLANES - 1 + j + c_len, 0:wd] * w[j:j + 1]
        return _silu(acc)

    def l2n(z):
        return z * lax.rsqrt(jnp.sum(z * z, -1, keepdims=True) + EPS)

    incl_m = [tri_s <= tri_t, tri_s >= tri_t]
    strict_m = [tri_s < tri_t, tri_s > tri_t]
    n_cu = 4

    def phase_a(cg0):
        cgs = [cg0 + i for i in range(n_cu)]
        rows = [pl.ds(pl.multiple_of(cg * c_len, c_len), c_len) for cg in cgs]
        q = [l2n(conv_silu(q_ref, cq_ref, cg)) * (GD_DK ** -0.5) for cg in cgs]
        k = [l2n(conv_silu(k_ref, ck_ref, cg)) for cg in cgs]
        v2 = [conv_silu(v_ref, cv_ref, cg) for cg in cgs]
        ab = [ab_ref[rw, :] for rw in rows]
        k_t = [x.T for x in k]
        g_all = [-jnp.exp(alog_ref[...]) * _softplus(x + dtb_ref[...]) for x in ab]
        beta_all = [jax.nn.sigmoid(x) for x in ab]
        prefix = [_mm(lower, g, "nn", P_CUM) for g in g_all]
        g_kk = [_mm(x, y, "nn", P_GRAM) for x, y in zip(k, k_t)]
        g_qk = [_mm(x, y, "nn", P_GRAM) for x, y in zip(q, k_t)]
        gtot_all = [jnp.sum(g, axis=0, keepdims=True) for g in g_all]
        gc_all = [jnp.where(lane_ab < 64, p, t - p + g) for p, t, g in zip(prefix, gtot_all, g_all)]
        gct_all = [x.T for x in gc_all]
        per = []
        for i in range(n_cu):
            for d in range(2):
                for vl in range(2):
                    col_g = d * 64 + 2 * kh + vl
                    col_b = col_g + n_vh
                    gc_col = jnp.sum(jnp.where(lane_ab == col_g, gc_all[i], 0.0), axis=1, keepdims=True)
                    beta = jnp.sum(jnp.where(lane_ab == col_b, beta_all[i], 0.0), axis=1, keepdims=True)
                    gc_row = jnp.sum(jnp.where(row_abt == col_g, gct_all[i], 0.0), axis=0, keepdims=True)
                    g_last = jnp.sum(jnp.where(lane_1 == col_g, gtot_all[i], 0.0), axis=1, keepdims=True)
                    decay = jnp.where(incl_m[d], jnp.exp(jnp.minimum(gc_col - gc_row, 0.0)), 0.0)
                    a_mat = jnp.where(strict_m[d], beta * g_kk[i] * decay, 0.0)
                    per.append((i, 2 * d + vl, gc_col, beta, gc_row, g_last, decay, a_mat))
        sel = [[n for n, p in enumerate(per) if p[1] // 2 == d] for d in range(2)]
        t_d = _unit_tri_inverse(*[jnp.stack([-per[n][7] for n in sl]) for sl in sel], inv_masks)
        t_all = [None] * len(per)
        for d in range(2):
            for m, n in enumerate(sel[d]):
                t_all[n] = t_d[d][m]
        rhs = [jnp.concatenate([v2[i][:, (j % 2) * GD_DV:(j % 2 + 1) * GD_DV] * beta,
                                k[i] * (beta * jnp.exp(gc_col))], axis=1)
               for i, j, gc_col, beta, _, _, _, _ in per]
        uw = [_mm(t_all[n], x, "nn", P_APPLY) for n, x in enumerate(rhs)]
        for n, (i, j, gc_col, beta, gc_row, g_last, decay, _) in enumerate(per):
            idx = cgs[i] * 4 + j
            u_buf[idx] = uw[n][:, :GD_DV]
            wq_buf[idx, 0:c_len, :] = uw[n][:, GD_DV:].astype(BF16)
            wq_buf[idx, c_len:, :] = (q[i] * jnp.exp(gc_col)).astype(BF16)
            attn_buf[idx] = (g_qk[i] * decay).astype(BF16)
            ket_buf[idx] = (k_t[i] * jnp.exp(g_last - gc_row)).astype(BF16)
            gl_buf[idx] = jnp.broadcast_to(jnp.exp(g_last), (SUBLANES, GD_DV))

    def a_body(i, carry):
        phase_a(n_cu * i)
        return carry

    lax.fori_loop(0, n_tot // n_cu, a_body, 0)

    def b_body(i, carry):
        chains = [(s, d, vl) for s in range(n_sb) for d in range(2) for vl in range(2)]
        cg = [s * n_chunks + (i if d == 0 else n_chunks - 1 - i) for s, d, _ in chains]
        idx = [c * 4 + 2 * d + vl for c, (_, d, vl) in zip(cg, chains)]
        s_mat = [s_ref[s * 4 + 2 * d + vl] for s, d, vl in chains]
        ws_qs = [_mm(wq_buf[n], x, "nn", P_APPLY) for n, x in zip(idx, s_mat)]
        v_new = [(u_buf[n] - x[0:c_len]).astype(BF16) for n, x in zip(idx, ws_qs)]
        o = [x[c_len:] + _mm(attn_buf[n], y, "nn", P_APPLY) for n, x, y in zip(idx, ws_qs, v_new)]
        s_new = [x * gl_buf[n, 0:1, :] + _mm(ket_buf[n], y, "nn", P_APPLY) for n, x, y in zip(idx, s_mat, v_new)]
        for n, (s, d, vl) in enumerate(chains):
            rows = pl.ds(pl.multiple_of(cg[n] * c_len, c_len), c_len)
            obuf[d, rows, pl.ds(vl * GD_DV, GD_DV)] = o[n]
            s_ref[s * 4 + 2 * d + vl] = s_new[n]
        return carry

    lax.fori_loop(0, n_chunks, b_body, 0)

    def c_body(cg, carry):
        rows = pl.ds(pl.multiple_of(cg * c_len, c_len), c_len)
        for vl in range(2):
            cols = pl.ds(vl * GD_DV, GD_DV)
            o = obuf[0, rows, cols] + obuf[1, rows, cols]
            o = o * lax.rsqrt(jnp.mean(o * o, -1, keepdims=True) + EPS) * nw_ref[...] * _silu(z_ref[rows, cols])
            o_ref[rows, cols] = o.astype(o_ref.dtype)
        return carry

    lax.fori_loop(0, n_tot, c_body, 0)
    for s in range(n_sb):
        for j in range(4):
            sfin_ref[s, j // 2, j % 2] = s_ref[s * 4 + j]


def _gdn_scan(proj, ab, conv_w, alog_row, dtb_row, nw, s0, row0, n_seq, t_len, n_sb):
    n_kh = 16
    blk_rows = n_sb * t_len
    assert row0 % blk_rows == 0 and n_seq % n_sb == 0 and (n_sb * t_len // CHUNK) % 2 == 0
    blk0 = row0 // blk_rows
    vw = 2 * GD_DV
    n_units = n_sb * (t_len // CHUNK) * 4
    in_specs = [pl.BlockSpec((blk_rows, GD_DK), lambda b, h: (blk0 + b, h)),
                pl.BlockSpec((blk_rows, GD_DK), lambda b, h: (blk0 + b, n_kh + h)),
                pl.BlockSpec((blk_rows, vw), lambda b, h: (blk0 + b, n_kh + h)),
                pl.BlockSpec((blk_rows, vw), lambda b, h: (blk0 + b, 2 * n_kh + h)),
                pl.BlockSpec((blk_rows, 128), lambda b, h: (blk0 + b, 0)),
                pl.BlockSpec((GD_CONV, GD_DK), lambda b, h: (0, h)),
                pl.BlockSpec((GD_CONV, GD_DK), lambda b, h: (0, n_kh + h)),
                pl.BlockSpec((GD_CONV, vw), lambda b, h: (0, n_kh + h)),
                pl.BlockSpec((1, 128), lambda b, h: (0, 0)),
                pl.BlockSpec((1, 128), lambda b, h: (0, 0)),
                pl.BlockSpec((1, GD_DV), lambda b, h: (0, 0))]
    args = [proj, proj, proj, proj, ab, conv_w, conv_w, conv_w, alog_row, dtb_row, nw]
    if s0 is not None:
        in_specs.append(pl.BlockSpec((n_sb, 2, 2, GD_DK, GD_DV), lambda b, h: (b, 0, h, 0, 0)))
        args.append(s0)
    o, sfin = pl.pallas_call(
        functools.partial(_gdn_kernel, t_len=t_len, n_sb=n_sb, has_s0=s0 is not None),
        grid=(n_seq // n_sb, n_kh),
        in_specs=in_specs,
        out_specs=[pl.BlockSpec((blk_rows, vw), lambda b, h: (b, h)),
                   pl.BlockSpec((n_sb, 2, 2, GD_DK, GD_DV), lambda b, h: (b, 0, h, 0, 0))],
        out_shape=[jax.ShapeDtypeStruct((n_seq * t_len, 2 * n_kh * GD_DV), BF16),
                   jax.ShapeDtypeStruct((n_seq, 2, 2 * n_kh, GD_DK, GD_DV), F32)],
        scratch_shapes=[pltpu.VMEM((n_units, 2 * CHUNK, GD_DV), BF16),
                        pltpu.VMEM((n_units, CHUNK, GD_DV), F32),
                        pltpu.VMEM((n_units, CHUNK, CHUNK), BF16),
                        pltpu.VMEM((n_units, GD_DK, CHUNK), BF16),
                        pltpu.VMEM((n_units, SUBLANES, GD_DV), F32),
                        pltpu.VMEM((2, blk_rows, vw), F32),
                        pltpu.VMEM((4 * n_sb, GD_DK, GD_DV), F32),
                        pltpu.VMEM((CHUNK + 2 * SUBLANES, vw), F32)],
        compiler_params=_cparams("arbitrary", "arbitrary"),
        name=f"gdn_scan_t{t_len}",
    )(*args)
    return o, sfin


def _grid_pos_embed(n_tokens, d_model):
    rows = n_tokens // GRID_W
    f32 = np.float32
    row = np.broadcast_to(np.arange(rows, dtype=f32)[:, None], (rows, GRID_W)).reshape(-1)
    col = np.broadcast_to(np.arange(GRID_W, dtype=f32)[None, :], (rows, GRID_W)).reshape(-1)
    quarter = d_model // 4
    omega = (f32(1.0) / (f32(POS_BASE) ** (np.arange(quarter, dtype=f32) / f32(quarter)))).astype(f32)
    ar = (row[:, None] * omega).astype(f32)
    ac = (col[:, None] * omega).astype(f32)
    return jnp.asarray(np.concatenate([np.sin(ar), np.cos(ar), np.sin(ac), np.cos(ac)], -1).astype(f32))


def _pad_axis(a, axis, size):
    pad = [(0, 0)] * a.ndim
    pad[axis] = (0, size - a.shape[axis])
    return jnp.pad(a, pad)


def kernel(x_prompt, x_sample, state_rwkv, state_gdn, c, c_ctx, norm_mix, norm_ffn, norm_final, w_mod, b_mod, ffn_w1, ffn_w3, ffn_w2, rw_mu, rw_wr, rw_wk, rw_wv, rw_wo, rw_w0, rw_w1, rw_w2, rw_a0, rw_a1, rw_a2, rw_g1, rw_g2, rw_kk, rw_ka, rw_rk, rw_ln_w, rw_ln_b, gd_w_in, gd_conv, gd_a_log, gd_dt_bias, gd_norm, gd_w_out):
    n_p, t_p, d = x_prompt.shape
    n_s, t_s, _ = x_sample.shape
    bf = lambda a: a.astype(BF16)
    row = lambda a: a.reshape(1, -1)
    tm_shift = 256
    tm_big = 512

    regions = [
        dict(x=x_prompt.reshape(n_p * t_p, d), n_seq=n_p, t_len=t_p, groups=[0] * n_p, n_sb=2,
             s0_rw=None, s0_gd=None),
        dict(x=(x_sample + _grid_pos_embed(t_s, d)[None]).reshape(n_s * t_s, d), n_seq=n_s, t_len=t_s,
             groups=list(range(1, n_s + 1)), n_sb=1,
             s0_rw=state_rwkv[:, 0].reshape(n_s, 2, d, RW_HEAD), s0_gd=state_gdn[:, 0]),
    ]
    for reg in regions:
        layout = [(reg["n_seq"], reg["t_len"], reg["groups"])]
        reg["tables"] = _tile_tables(layout, tm_shift)
        reg["grp"] = _tile_tables(layout, tm_big)[0]

    cond = _pad_axis(jnp.concatenate([c_ctx[None], c], axis=0), 0, SUBLANES)
    mod = _modulation(cond, w_mod, b_mod).reshape(w_mod.shape[0], SUBLANES, 6, d)

    w_rkv = bf(jnp.stack([rw_wr[0], rw_wk[0], rw_wv[0]]))
    lp = RW_LORA_PAD
    lora_w = (bf(rw_g1[0]), bf(rw_g2[0]),
              bf(_pad_axis(rw_w1[0], 2, lp)), bf(_pad_axis(rw_w2[0], 1, lp)),
              bf(_pad_axis(rw_a1[0], 2, lp)), bf(_pad_axis(rw_a2[0], 1, lp)),
              rw_w0[0][:, None, :], rw_a0[0][:, None, :])
    scan_p = (row(rw_kk[0]), row(rw_ka[0]), row(rw_rk[0]), row(rw_ln_w[0]), row(rw_ln_b[0]))
    w_o, ffn0 = bf(rw_wo[0]), (bf(ffn_w1[0]), bf(ffn_w3[0]), bf(ffn_w2[0]))
    for reg in regions:
        rkv, gate, lw, a = _rwkv_inputs(reg["x"], mod[0], reg["tables"], row(norm_mix[0]), rw_mu[0], w_rkv,
                                        *lora_w, tm_shift)
        y, reg["s_rwkv"] = _rwkv_scan(rkv, gate, lw, a, *scan_p, reg["s0_rw"], 0, reg["n_seq"], reg["t_len"],
                                      reg["n_sb"])
        x = _out_proj(y, w_o, reg["x"], mod[0], reg["grp"], 2, tm_big)
        reg["x"] = _ffn(x, mod[0], reg["grp"], row(norm_ffn[0]), *ffn0, row(norm_final), False, tm_big)

    n_main = 3 * 4096
    w_in_main, w_in_ab = bf(gd_w_in[0][:, :n_main]), bf(gd_w_in[0][:, n_main:])
    zeros32 = jnp.zeros((2, 32), F32)
    alog_row = jnp.stack([gd_a_log[0], zeros32], axis=1).reshape(1, 128)
    dtb_row = jnp.stack([gd_dt_bias[0], zeros32], axis=1).reshape(1, 128)
    w_out, ffn1 = bf(gd_w_out[0]), (bf(ffn_w1[1]), bf(ffn_w3[1]), bf(ffn_w2[1]))
    for reg in regions:
        proj = _in_proj(reg["x"], mod[1], reg["grp"], row(norm_mix[1]), w_in_main, tm_big, 1024)
        ab = _in_proj(reg["x"], mod[1], reg["grp"], row(norm_mix[1]), w_in_ab, tm_big, 128)
        o, reg["s_gdn"] = _gdn_scan(proj, ab, gd_conv[0], alog_row, dtb_row, row(gd_norm[0]), reg["s0_gd"], 0,
                                    reg["n_seq"], reg["t_len"], reg["n_sb"])
        x = _out_proj(o, w_out, reg["x"], mod[1], reg["grp"], 2, tm_big)
        reg["x"] = _ffn(x, mod[1], reg["grp"], row(norm_ffn[1]), *ffn1, row(norm_final), True, tm_big)

    y_prompt = regions[0]["x"].reshape(n_p, t_p, d)
    y_sample = regions[1]["x"].reshape(n_s, t_s, d)
    new_state_rwkv = regions[0]["s_rwkv"].reshape(n_p, 1, 2, d // RW_HEAD, RW_HEAD, RW_HEAD)
    new_state_gdn = regions[0]["s_gdn"].reshape(n_p, 1, 2, 32, GD_DK, GD_DV)
    return (y_prompt, y_sample, new_state_rwkv, new_state_gdn)
```

```python
import functools

import numpy as np
import jax
import jax.numpy as jnp
from jax import lax
from jax.experimental import pallas as pl
from jax.experimental.pallas import tpu as pltpu

F32 = jnp.float32
BF16 = jnp.bfloat16

EPS = 1e-6
GRID_W = 64
POS_BASE = 10000.0
RW_HEAD = 64
RW_GROUP_HEADS = 4
RW_LANES = RW_HEAD * RW_GROUP_HEADS
RW_LORA_PAD = 128
GD_DK = 128
GD_DV = 128
GD_CONV = 4
CHUNK = 64
RW_SUPER_CHUNKS = 8
SUBLANES = 8
VMEM_LIMIT = 56 * 1024 * 1024


def _cparams(*sem):
    return pltpu.CompilerParams(dimension_semantics=sem, vmem_limit_bytes=VMEM_LIMIT)


def _dot(a, b):
    return jnp.dot(a.astype(BF16), b.astype(BF16), preferred_element_type=F32)


_DIMS = {"nn": (((1,), (0,)), ((), ())),
         "nt": (((1,), (1,)), ((), ())),
         "tn": (((0,), (0,)), ((), ())),
         "bnn": (((2,), (1,)), ((0,), (0,)))}


def _split_bf16(x, pieces):
    out = []
    for _ in range(pieces - 1):
        p = x.astype(BF16)
        out.append(p)
        x = x - p.astype(F32)
    out.append(x.astype(BF16))
    return out


def _mm(a, b, dims="nn", mode="b"):
    dn = _DIMS[dims]
    dg = lambda x, y: lax.dot_general(x, y, dn, preferred_element_type=F32)
    if mode == "b":
        return dg(a.astype(BF16), b.astype(BF16))
    if mode in ("la", "la2"):
        a0 = a.astype(BF16)
        return sum(dg(a0, p) for p in _split_bf16(b, 3 if mode == "la" else 2))
    if mode in ("ra", "ra2"):
        b0 = b.astype(BF16)
        return sum(dg(p, b0) for p in _split_bf16(a, 3 if mode == "ra" else 2))
    raise ValueError(mode)


P_INV = "b"
P_GRAM = "b"
P_APPLY = "b"
P_SUM = "ra2"
P_CUM = "la2"


def _iota(shape, dim):
    return lax.broadcasted_iota(jnp.int32, shape, dim)


def _silu(x):
    return x * jax.nn.sigmoid(x)


def _softplus(x):
    return jnp.maximum(x, 0.0) + jnp.log(1.0 + jnp.exp(-jnp.abs(x)))


def _rms_mod(x, nw, sh, sc):
    y = x * lax.rsqrt(jnp.mean(x * x, -1, keepdims=True) + EPS)
    return (y * nw) * (1.0 + sc) + sh


def _inverse_masks(c_len):
    ij = _iota((c_len, c_len), 0) ^ _iota((c_len, c_len), 1)
    levels = []
    half = 2
    while half < c_len:
        levels.append(((ij >= half) & (ij < 2 * half)).astype(BF16))
        half *= 2
    return (ij == 0).astype(F32), (ij == 1).astype(F32), levels


def _unit_tri_inverse(n_lower, n_upper, masks):
    eye, pair, levels = masks
    mats = [n_lower, n_upper]
    c_len = n_lower.shape[-1]
    d_mats = [eye + m * pair for m in mats]
    n_bf = [m.astype(BF16) for m in mats]
    half = 2
    for lvl in levels:
        l_mats = [m * lvl for m in n_bf]
        if half < SUBLANES:
            dl = [_mm(d, l, "bnn", P_INV) for d, l in zip(d_mats, l_mats)]
            d_mats = [d + _mm(x, d, "bnn", P_INV) for d, x in zip(d_mats, dl)]
        else:
            first = [slice(b0, b0 + half) for b0 in range(0, c_len, 2 * half)]
            second = [slice(b0 + half, b0 + 2 * half) for b0 in range(0, c_len, 2 * half)]
            act = [second, first]
            d_act = [jnp.concatenate([d[:, sl, :] for sl in rows], axis=1) for d, rows in zip(d_mats, act)]
            dl = [_mm(x, l, "bnn", P_INV) for x, l in zip(d_act, l_mats)]
            upd = [x + _mm(y, d, "bnn", P_INV) for x, y, d in zip(d_act, dl, d_mats)]
            out = []
            for g, (d, u) in enumerate(zip(d_mats, upd)):
                parts = []
                for n in range(len(first)):
                    new = u[:, n * half:(n + 1) * half, :]
                    parts += [d[:, first[n], :], new] if g == 0 else [new, d[:, second[n], :]]
                out.append(jnp.concatenate(parts, axis=1))
            d_mats = out
        half *= 2
    return d_mats


def _mod_kernel(c_ref, w_ref, b_ref, o_ref):
    o_ref[0] = _dot(_silu(c_ref[...]), w_ref[0]) + b_ref[0]


def _modulation(cond, w_mod, b_mod):
    depth, d, n = w_mod.shape
    tn = 1024
    return pl.pallas_call(
        _mod_kernel,
        grid=(depth, n // tn),
        in_specs=[pl.BlockSpec((SUBLANES, d), lambda l, j: (0, 0)),
                  pl.BlockSpec((1, d, tn), lambda l, j: (l, 0, j)),
                  pl.BlockSpec((1, 1, tn), lambda l, j: (l, 0, j))],
        out_specs=pl.BlockSpec((1, SUBLANES, tn), lambda l, j: (l, 0, j)),
        out_shape=jax.ShapeDtypeStruct((depth, SUBLANES, n), F32),
        compiler_params=_cparams("arbitrary", "arbitrary"),
        name="modulation",
    )(cond, w_mod, b_mod.reshape(depth, 1, n))


def _shift_mix(x_ref, xp_ref, xn_ref, mod_ref, nw_ref, has_prev, has_next):
    nw = nw_ref[...]
    sh = mod_ref[0, 0:1, :]
    sc = mod_ref[0, 1:2, :]
    h = _rms_mod(x_ref[...], nw, sh, sc)
    tm = h.shape[0]
    hp = _rms_mod(xp_ref[SUBLANES - 1:SUBLANES, :], nw, sh, sc) * has_prev
    hn = _rms_mod(xn_ref[0:1, :], nw, sh, sc) * has_next
    rows = _iota(h.shape, 0)
    prev = jnp.where(rows == 0, hp, pltpu.roll(h, 1, 0))
    nxt = jnp.where(rows == tm - 1, hn, pltpu.roll(h, tm - 1, 0))
    return h, 0.5 * (prev + nxt) - h


def _rkv_kernel(grp_ref, hp_ref, hn_ref, x_ref, xp_ref, xn_ref, mod_ref, nw_ref, mu_ref, w_ref, o_ref):
    i = pl.program_id(1)
    h, xx = _shift_mix(x_ref, xp_ref, xn_ref, mod_ref, nw_ref,
                       hp_ref[i].astype(F32), hn_ref[i].astype(F32))
    o_ref[0] = _dot(h + xx * mu_ref[0], w_ref[0])


def _lora_kernel(grp_ref, hp_ref, hn_ref, x_ref, xp_ref, xn_ref, mod_ref, nw_ref, mu_ref,
                 g1_ref, g2_ref, w1_ref, w2_ref, a1_ref, a2_ref, w0_ref, a0_ref,
                 gate_ref, lw_ref, a_ref):
    i = pl.program_id(0)
    h, xx = _shift_mix(x_ref, xp_ref, xn_ref, mod_ref, nw_ref,
                       hp_ref[i].astype(F32), hn_ref[i].astype(F32))
    xw = h + xx * mu_ref[1:2, :]
    xa = h + xx * mu_ref[4:5, :]
    xg = h + xx * mu_ref[5:6, :]
    gate_ref[...] = _dot(jax.nn.sigmoid(_dot(xg, g1_ref[...])), g2_ref[...])
    for d in range(2):
        lw = jnp.tanh(_dot(xw, w1_ref[d]))
        w_log = -_softplus(-(w0_ref[d] + _dot(lw, w2_ref[d]))) - 0.5
        lw_ref[d] = -jnp.exp(w_log)
        a_ref[d] = jax.nn.sigmoid(a0_ref[d] + _dot(_dot(xa, a1_ref[d]), a2_ref[d]))


def _tile_tables(regions, tm):
    row_grp, row_pos, row_len = [], [], []
    for n_seq, t_len, groups in regions:
        for s in range(n_seq):
            row_grp += [groups[s]] * t_len
            row_pos += list(range(t_len))
            row_len += [t_len] * t_len
    row_grp, row_pos, row_len = (np.asarray(v).reshape(-1, tm) for v in (row_grp, row_pos, row_len))
    assert (row_grp == row_grp[:, :1]).all(), "a row tile must not straddle modulation groups"
    as_i32 = lambda v: jnp.asarray(np.asarray(v, np.int32))
    return (as_i32(row_grp[:, 0]), as_i32(row_pos[:, 0] > 0),
            as_i32(row_pos[:, -1] < row_len[:, -1] - 1))


def _halo_specs(tm, d, m_rows, n_lead):
    blocks = tm // SUBLANES
    last = m_rows // SUBLANES - 1
    if n_lead == 1:
        cur = lambda j, i, *_: (i, 0)
        prev = lambda j, i, *_: (jnp.maximum(i * blocks - 1, 0), 0)
        nxt = lambda j, i, *_: (jnp.minimum((i + 1) * blocks, last), 0)
    else:
        cur = lambda i, *_: (i, 0)
        prev = lambda i, *_: (jnp.maximum(i * blocks - 1, 0), 0)
        nxt = lambda i, *_: (jnp.minimum((i + 1) * blocks, last), 0)
    return [pl.BlockSpec((tm, d), cur), pl.BlockSpec((SUBLANES, d), prev), pl.BlockSpec((SUBLANES, d), nxt)]


def _rwkv_inputs(x, mod, tables, nw, mu, w_rkv, g1, g2, w1, w2, a1, a2, w0, a0, tm):
    m_rows, d = x.shape
    n_tiles = m_rows // tm
    mu_rkv = jnp.stack([mu[0], mu[2], mu[3]])[:, None, :]
    rkv = pl.pallas_call(
        _rkv_kernel,
        grid_spec=pltpu.PrefetchScalarGridSpec(
            num_scalar_prefetch=3, grid=(3, n_tiles),
            in_specs=_halo_specs(tm, d, m_rows, 1) + [
                pl.BlockSpec((1, 6, d), lambda j, i, g, *_: (g[i], 0, 0)),
                pl.BlockSpec((1, d), lambda j, i, *_: (0, 0)),
                pl.BlockSpec((1, 1, d), lambda j, i, *_: (j, 0, 0)),
                pl.BlockSpec((1, d, d), lambda j, i, *_: (j, 0, 0))],
            out_specs=pl.BlockSpec((1, tm, d), lambda j, i, *_: (j, i, 0))),
        out_shape=jax.ShapeDtypeStruct((3, m_rows, d), F32),
        compiler_params=_cparams("arbitrary", "arbitrary"),
        name="rwkv_rkv_proj",
    )(*tables, x, x, x, mod, nw, mu_rkv, w_rkv)

    full = lambda a: pl.BlockSpec(a.shape, lambda i, *_: (0,) * a.ndim)
    small = [g1, g2, w1, w2, a1, a2, w0, a0]
    gate, lw, a = pl.pallas_call(
        _lora_kernel,
        grid_spec=pltpu.PrefetchScalarGridSpec(
            num_scalar_prefetch=3, grid=(n_tiles,),
            in_specs=_halo_specs(tm, d, m_rows, 0) + [
                pl.BlockSpec((1, 6, d), lambda i, g, *_: (g[i], 0, 0)),
                pl.BlockSpec((1, d), lambda i, *_: (0, 0)),
                full(mu)] + [full(s) for s in small],
            out_specs=[pl.BlockSpec((tm, d), lambda i, *_: (i, 0)),
                       pl.BlockSpec((2, tm, d), lambda i, *_: (0, i, 0)),
                       pl.BlockSpec((2, tm, d), lambda i, *_: (0, i, 0))]),
        out_shape=[jax.ShapeDtypeStruct((m_rows, d), F32),
                   jax.ShapeDtypeStruct((2, m_rows, d), F32),
                   jax.ShapeDtypeStruct((2, m_rows, d), F32)],
        compiler_params=_cparams("arbitrary"),
        name="rwkv_lora_proj",
    )(*tables, x, x, x, mod, nw, mu, *small)
    return rkv, gate, lw, a


def _rwkv_scan_kernel(*refs, t_len, n_sb, has_s0):
    (r_ref, k_ref, v_ref, gate_ref, lw_ref, a_ref, kk_ref, ka_ref, rk_ref, lnw_ref, lnb_ref) = refs[:11]
    rest = refs[11:]
    if has_s0:
        s0_ref, rest = rest[0], rest[1:]
    (y_ref, sfin_ref, r2_buf, a2_buf, uv_buf, bk_buf, y0_buf, w_buf, ybuf, bonus_buf, s_ref) = rest
    c_len, lanes, heads = CHUNK, RW_LANES, RW_GROUP_HEADS
    n_chunks = t_len // c_len
    sc = min(n_chunks, RW_SUPER_CHUNKS)
    n_super = n_chunks // sc
    n_pp = max(1, 8 // (2 * n_sb))
    stack = heads * c_len

    same_head = (_iota((stack, lanes), 0) // c_len == _iota((stack, lanes), 1) // RW_HEAD).astype(F32)
    ones_blk = same_head
    rep = (_iota((RW_HEAD, lanes), 1) % RW_HEAD == _iota((RW_HEAD, lanes), 0)).astype(F32)
    t_idx = _iota((stack, c_len), 0) % c_len
    s_idx = _iota((stack, c_len), 1)
    tri_t = _iota((c_len, c_len), 0)
    tri_s = _iota((c_len, c_len), 1)

    inv_masks = _inverse_masks(c_len)
    kk_p, ka_p, rk_p = kk_ref[...], ka_ref[...], rk_ref[...]
    lnw, lnb = lnw_ref[...], lnb_ref[...]

    def fold(z):
        z = z * same_head
        return z[0:c_len] + z[c_len:2 * c_len] + z[2 * c_len:3 * c_len] + z[3 * c_len:4 * c_len]

    def tile4(z):
        return jnp.concatenate([z, z, z, z], axis=0) * same_head

    for s in range(n_sb):
        for d in range(2):
            if has_s0:
                s_ref[s * 2 + d] = _mm(s0_ref[s, d], rep, "nn", "ra") * ones_blk
            else:
                s_ref[s * 2 + d] = jnp.zeros((lanes, lanes), F32)
    ybuf[...] = jnp.zeros_like(ybuf)
    bonus_buf[...] = jnp.zeros_like(bonus_buf)

    def chunk_rows(s, d, p):
        c = p if d == 0 else n_chunks - 1 - p
        return pl.ds(pl.multiple_of((s * n_chunks + c) * c_len, c_len), c_len)

    incl_f = [(tri_s <= tri_t).astype(F32), (tri_s >= tri_t).astype(F32)]
    incl4 = [s_idx <= t_idx, s_idx >= t_idx]
    strict4 = [s_idx < t_idx, s_idx > t_idx]

    def phase_a(p0, j0):
        units = [(s, d, q) for s in range(n_sb) for d in range(2) for q in range(n_pp)]
        dirs = [d for _, d, _ in units]
        rows = [chunk_rows(s, d, p0 + q) for s, d, q in units]
        r = [r_ref[rw, :] for rw in rows]
        k = [k_ref[rw, :] for rw in rows]
        v = [v_ref[rw, :] for rw in rows]
        lw = [lw_ref[d, rw, :] for d, rw in zip(dirs, rows)]
        a = [a_ref[d, rw, :] for d, rw in zip(dirs, rows)]
        bonus_old = [bonus_buf[rw, :] for rw in rows]
        kx = [ki * kk_p for ki in k]
        ss = [_mm(x * x, ones_blk, "nn", P_SUM) for x in kx]
        cw = [_mm(incl_f[d], x, "nn", P_CUM) for d, x in zip(dirs, lw)]
        bsum = [_mm(ri * ki * (1.0 + (ai - 1.0) * ka_p) * rk_p, ones_blk, "nn", P_SUM)
                for ri, ki, ai in zip(r, k, a)]
        kkn = [x * lax.rsqrt(y + EPS) for x, y in zip(kx, ss)]
        kd = [ki * (1.0 + (ai - 1.0) * ka_p) for ki, ai in zip(k, a)]
        w_row = [jnp.exp(jnp.sum(x, axis=0, keepdims=True)) for x in lw]
        e_neg = [jnp.exp(-x) for x in cw]
        at = [-kn * jnp.exp(c - l) for kn, c, l in zip(kkn, cw, lw)]
        bt = [kn * ai * e for kn, ai, e in zip(kkn, a, e_neg)]
        kt = [x * e for x, e in zip(kd, e_neg)]
        rt = [ri * jnp.exp(c) for ri, c in zip(r, cw)]
        lhs = [jnp.concatenate([tile4(x), tile4(y)], axis=0) for x, y in zip(at, rt)]
        g_b = [_mm(x, y, "nt", P_GRAM) for x, y in zip(lhs, bt)]
        g_k = [_mm(x, y, "nt", P_GRAM) for x, y in zip(lhs, kt)]
        n_s = [jnp.where(strict4[d], g[0:stack], 0.0) for d, g in zip(dirs, g_b)]
        ak_s = [jnp.where(strict4[d], g[0:stack], 0.0) for d, g in zip(dirs, g_k)]
        rb_s = [jnp.where(incl4[d], g[stack:], 0.0) for d, g in zip(dirs, g_b)]
        rk_s = [jnp.where(incl4[d], g[stack:], 0.0) for d, g in zip(dirs, g_k)]
        sel = [[i for i, di in enumerate(dirs) if di == d] for d in range(2)]
        t_d = _unit_tri_inverse(*[jnp.concatenate([n_s[i].reshape(heads, c_len, c_len) for i in sl], axis=0)
                                  for sl in sel], inv_masks)
        t_s = [None] * len(units)
        for d in range(2):
            for n, i in enumerate(sel[d]):
                t_s[i] = t_d[d][n * heads:(n + 1) * heads].reshape(stack, c_len)
        a2 = [fold(_mm(t, x, "nn", P_APPLY)) for t, x in zip(t_s, at)]
        akv = [fold(_mm(x, vi, "nn", P_APPLY)) for x, vi in zip(ak_s, v)]
        rkv = [_mm(x, vi, "nn", P_APPLY) for x, vi in zip(rk_s, v)]
        u0 = [fold(_mm(t, x, "nn", P_APPLY)) for t, x in zip(t_s, akv)]
        r2 = [x + fold(_mm(rb, y, "nn", P_APPLY)) for x, rb, y in zip(rt, rb_s, a2)]
        y0 = [fold(_mm(rb, x, "nn", P_APPLY) + y) for rb, x, y in zip(rb_s, u0, rkv)]
        for i, (s, d, q) in enumerate(units):
            slot = (s * 2 + d) * sc + j0 + q
            r2_buf[slot] = r2[i].astype(BF16)
            a2_buf[slot] = a2[i].astype(BF16)
            uv_buf[slot, 0:c_len, :] = u0[i].astype(BF16)
            uv_buf[slot, c_len:, :] = v[i].astype(BF16)
            bk_buf[slot, 0:c_len, :] = (bt[i] * w_row[i]).astype(BF16)
            bk_buf[slot, c_len:, :] = (kt[i] * w_row[i]).astype(BF16)
            y0_buf[slot] = y0[i]
            w_buf[slot] = jnp.broadcast_to(w_row[i], (SUBLANES, lanes))
            bonus_buf[rows[i], :] = bonus_old[i] + bsum[i] * v[i]

    def phase_b(p, j):
        units = [(s, d) for s in range(n_sb) for d in range(2)]
        rows = [chunk_rows(s, d, p) for s, d in units]
        slots = [(s * 2 + d) * sc + j for s, d in units]
        s_mat = [s_ref[s * 2 + d] for s, d in units]
        y_old = [ybuf[rw, :] for rw in rows]
        s_bf = [x.astype(BF16) for x in s_mat]
        m_s = [_mm(a2_buf[sl], bk_buf[sl, 0:c_len, :], "tn", P_APPLY) * ones_blk for sl in slots]
        s_add = [_mm(uv_buf[sl], bk_buf[sl], "tn", P_APPLY) * ones_blk for sl in slots]
        y = [_mm(r2_buf[sl], x, "nt", P_APPLY) + y0_buf[sl] for sl, x in zip(slots, s_bf)]
        s_new = [x * w_buf[sl, 0:1, :] + _mm(xb, m, "nn", P_APPLY) + z
                 for x, xb, m, z, sl in zip(s_mat, s_bf, m_s, s_add, slots)]
        for i, (s, d) in enumerate(units):
            ybuf[rows[i], :] = y_old[i] + y[i]
            s_ref[s * 2 + d] = s_new[i]

    def super_body(sp, carry):
        def a_body(j, c2):
            phase_a(sp * sc + j * n_pp, j * n_pp)
            return c2

        def b_body(j, c2):
            phase_b(sp * sc + j, j)
            return c2

        lax.fori_loop(0, sc // n_pp, a_body, 0)
        lax.fori_loop(0, sc, b_body, 0)
        return carry

    lax.fori_loop(0, n_super, super_body, 0)

    n_cc = 4

    def c_body(i, carry):
        rows = [pl.ds(pl.multiple_of((i * n_cc + q) * c_len, c_len), c_len) for q in range(n_cc)]
        y = [ybuf[rw, :] for rw in rows]
        mean = [_mm(x, ones_blk, "nn", P_SUM) * (1.0 / RW_HEAD) for x in y]
        yc = [x - m for x, m in zip(y, mean)]
        var = [_mm(x * x, ones_blk, "nn", P_SUM) * (1.0 / RW_HEAD) for x in yc]
        for rw, x, vr in zip(rows, yc, var):
            yn = x * lax.rsqrt(vr + RW_HEAD * 1e-5) * lnw + lnb
            y_ref[rw, :] = ((yn + bonus_buf[rw, :]) * gate_ref[rw, :]).astype(y_ref.dtype)
        return carry

    lax.fori_loop(0, n_sb * n_chunks // n_cc, c_body, 0)
    rep_t = (_iota((lanes, RW_HEAD), 0) % RW_HEAD == _iota((lanes, RW_HEAD), 1)).astype(F32)
    for s in range(n_sb):
        for d in range(2):
            sfin_ref[s, d] = _mm(s_ref[s * 2 + d], rep_t, "nn", "ra")


def _rwkv_scan(rkv, gate, lw, a, kk, ka, rk, lnw, lnb, s0, row0, n_seq, t_len, n_sb):
    _, m_rows, d = rkv.shape
    n_groups = d // RW_LANES
    blk_rows = n_sb * t_len
    assert row0 % blk_rows == 0 and n_seq % n_sb == 0
    blk0 = row0 // blk_rows
    sc = min(t_len // CHUNK, RW_SUPER_CHUNKS)
    assert (t_len // CHUNK) % sc == 0
    n_slots = n_sb * 2 * sc
    tok = lambda b, g: (blk0 + b, g)
    in_specs = [pl.BlockSpec((None, blk_rows, RW_LANES), lambda b, g, j=j: (j, blk0 + b, g)) for j in range(3)]
    in_specs += [pl.BlockSpec((blk_rows, RW_LANES), tok),
                 pl.BlockSpec((2, blk_rows, RW_LANES), lambda b, g: (0, blk0 + b, g)),
                 pl.BlockSpec((2, blk_rows, RW_LANES), lambda b, g: (0, blk0 + b, g))]
    in_specs += [pl.BlockSpec((1, RW_LANES), lambda b, g: (0, g))] * 5
    args = [rkv, rkv, rkv, gate, lw, a, kk, ka, rk, lnw, lnb]
    if s0 is not None:
        in_specs.append(pl.BlockSpec((n_sb, 2, RW_LANES, RW_HEAD), lambda b, g: (b, 0, g, 0)))
        args.append(s0)
    y, sfin = pl.pallas_call(
        functools.partial(_rwkv_scan_kernel, t_len=t_len, n_sb=n_sb, has_s0=s0 is not None),
        grid=(n_seq // n_sb, n_groups),
        in_specs=in_specs,
        out_specs=[pl.BlockSpec((blk_rows, RW_LANES), lambda b, g: (b, g)),
                   pl.BlockSpec((n_sb, 2, RW_LANES, RW_HEAD), lambda b, g: (b, 0, g, 0))],
        out_shape=[jax.ShapeDtypeStruct((n_seq * t_len, d), BF16),
                   jax.ShapeDtypeStruct((n_seq, 2, d, RW_HEAD), F32)],
        scratch_shapes=[pltpu.VMEM((n_slots, CHUNK, RW_LANES), BF16),
                        pltpu.VMEM((n_slots, CHUNK, RW_LANES), BF16),
                        pltpu.VMEM((n_slots, 2 * CHUNK, RW_LANES), BF16),
                        pltpu.VMEM((n_slots, 2 * CHUNK, RW_LANES), BF16),
                        pltpu.VMEM((n_slots, CHUNK, RW_LANES), F32),
                        pltpu.VMEM((n_slots, SUBLANES, RW_LANES), F32),
                        pltpu.VMEM((blk_rows, RW_LANES), F32),
                        pltpu.VMEM((blk_rows, RW_LANES), F32),
                        pltpu.VMEM((2 * n_sb, RW_LANES, RW_LANES), F32)],
        compiler_params=_cparams("arbitrary", "arbitrary"),
        name=f"rwkv_scan_t{t_len}",
    )(*args)
    return y, sfin


def _out_proj_kernel(grp_ref, y_ref, w_ref, x_ref, mod_ref, o_ref, *, gate_row):
    o_ref[...] = x_ref[...] + mod_ref[0, gate_row:gate_row + 1, :] * _dot(y_ref[...], w_ref[...])


def _out_proj(y, w, x, mod, grp, gate_row, tm):
    m_rows, k_dim = y.shape
    d = w.shape[1]
    tn = 1024
    return pl.pallas_call(
        functools.partial(_out_proj_kernel, gate_row=gate_row),
        grid_spec=pltpu.PrefetchScalarGridSpec(
            num_scalar_prefetch=1, grid=(d // tn, m_rows // tm),
            in_specs=[pl.BlockSpec((tm, k_dim), lambda n, i, g: (i, 0)),
                      pl.BlockSpec((k_dim, tn), lambda n, i, g: (0, n)),
                      pl.BlockSpec((tm, tn), lambda n, i, g: (i, n)),
                      pl.BlockSpec((1, 6, tn), lambda n, i, g: (g[i], 0, n))],
            out_specs=pl.BlockSpec((tm, tn), lambda n, i, g: (i, n))),
        out_shape=jax.ShapeDtypeStruct((m_rows, d), F32),
        compiler_params=_cparams("arbitrary", "arbitrary"),
        name="out_proj",
    )(grp, y, w, x, mod)


def _in_proj_kernel(grp_ref, x_ref, mod_ref, nw_ref, w_ref, o_ref, h_buf):
    @pl.when(pl.program_id(1) == 0)
    def _():
        h_buf[...] = _rms_mod(x_ref[...], nw_ref[...], mod_ref[0, 0:1, :], mod_ref[0, 1:2, :]).astype(BF16)

    o_ref[...] = jnp.dot(h_buf[...], w_ref[...], preferred_element_type=F32)


def _in_proj(x, mod, grp, nw, w, tm, tn):
    m_rows, d = x.shape
    n = w.shape[1]
    return pl.pallas_call(
        _in_proj_kernel,
        grid_spec=pltpu.PrefetchScalarGridSpec(
            num_scalar_prefetch=1, grid=(m_rows // tm, n // tn),
            in_specs=[pl.BlockSpec((tm, d), lambda i, n_, g: (i, 0)),
                      pl.BlockSpec((1, 6, d), lambda i, n_, g: (g[i], 0, 0)),
                      pl.BlockSpec((1, d), lambda i, n_, g: (0, 0)),
                      pl.BlockSpec((d, tn), lambda i, n_, g: (0, n_))],
            out_specs=pl.BlockSpec((tm, tn), lambda i, n_, g: (i, n_)),
            scratch_shapes=[pltpu.VMEM((tm, d), BF16)]),
        out_shape=jax.ShapeDtypeStruct((m_rows, n), F32),
        compiler_params=_cparams("arbitrary", "arbitrary"),
        name="gdn_in_proj",
    )(grp, x, mod, nw, w)


def _ffn_kernel(grp_ref, x_ref, mod_ref, nw_ref, w1_ref, w3_ref, w2_ref, fw_ref, o_ref, h_buf, acc,
                *, final_norm):
    f = pl.program_id(1)

    @pl.when(f == 0)
    def _():
        h_buf[...] = _rms_mod(x_ref[...], nw_ref[...], mod_ref[0, 3:4, :], mod_ref[0, 4:5, :]).astype(BF16)
        acc[...] = jnp.zeros_like(acc)

    h = h_buf[...]
    gate = jnp.dot(h, w1_ref[...], preferred_element_type=F32)
    up = jnp.dot(h, w3_ref[...], preferred_element_type=F32)
    acc[...] += _dot(_silu(gate) * up, w2_ref[...])

    @pl.when(f == pl.num_programs(1) - 1)
    def _():
        y = x_ref[...] + mod_ref[0, 5:6, :] * acc[...]
        if final_norm:
            y = y * lax.rsqrt(jnp.mean(y * y, -1, keepdims=True) + EPS) * fw_ref[...]
        o_ref[...] = y


def _ffn(x, mod, grp, nw, w1, w3, w2, fw, final_norm, tm):
    m_rows, d = x.shape
    d_ff = w1.shape[1]
    tf = 512
    return pl.pallas_call(
        functools.partial(_ffn_kernel, final_norm=final_norm),
        grid_spec=pltpu.PrefetchScalarGridSpec(
            num_scalar_prefetch=1, grid=(m_rows // tm, d_ff // tf),
            in_specs=[pl.BlockSpec((tm, d), lambda i, f, g: (i, 0)),
                      pl.BlockSpec((1, 6, d), lambda i, f, g: (g[i], 0, 0)),
                      pl.BlockSpec((1, d), lambda i, f, g: (0, 0)),
                      pl.BlockSpec((d, tf), lambda i, f, g: (0, f)),
                      pl.BlockSpec((d, tf), lambda i, f, g: (0, f)),
                      pl.BlockSpec((tf, d), lambda i, f, g: (f, 0)),
                      pl.BlockSpec((1, d), lambda i, f, g: (0, 0))],
            out_specs=pl.BlockSpec((tm, d), lambda i, f, g: (i, 0)),
            scratch_shapes=[pltpu.VMEM((tm, d), BF16), pltpu.VMEM((tm, d), F32)]),
        out_shape=jax.ShapeDtypeStruct((m_rows, d), F32),
        compiler_params=_cparams("arbitrary", "arbitrary"),
        name="ffn",
    )(grp, x, mod, nw, w1, w3, w2, fw)


def _gdn_kernel(*refs, t_len, n_sb, has_s0):
    (q_ref, k_ref, v_ref, z_ref, ab_ref, cq_ref, ck_ref, cv_ref, alog_ref, dtb_ref, nw_ref) = refs[:11]
    rest = refs[11:]
    if has_s0:
        s0_ref, rest = rest[0], rest[1:]
    o_ref, sfin_ref, wq_buf, u_buf, attn_buf, ket_buf, gl_buf, obuf, s_ref, ext_buf = rest
    c_len = CHUNK
    n_chunks = t_len // c_len
    n_tot = n_sb * n_chunks
    last_row0 = n_sb * t_len - SUBLANES
    kh = pl.program_id(1)
    n_vh = 32
    tri_t = _iota((c_len, c_len), 0)
    tri_s = _iota((c_len, c_len), 1)
    lane_ab = _iota((c_len, 128), 1)
    row_abt = _iota((128, c_len), 0)
    lane_1 = _iota((1, 128), 1)
    lower = (tri_s <= tri_t).astype(F32)
    inv_masks = _inverse_masks(c_len)

    for s in range(n_sb):
        for j in range(4):
            if has_s0:
                s_ref[s * 4 + j] = s0_ref[s, j // 2, j % 2]
            else:
                s_ref[s * 4 + j] = jnp.zeros((GD_DK, GD_DV), F32)

    def conv_silu(ref, w_ref, cg):
        c = cg % n_chunks
        r0 = pl.multiple_of(cg * c_len, c_len)
        main = ref[pl.ds(r0, c_len), :]
        up0 = pl.multiple_of(jnp.maximum(r0 - SUBLANES, 0), SUBLANES)
        dn0 = pl.multiple_of(jnp.minimum(r0 + c_len, last_row0), SUBLANES)
        up = ref[pl.ds(up0, SUBLANES), :] * jnp.where(c > 0, 1.0, 0.0)
        dn = ref[pl.ds(dn0, SUBLANES), :] * jnp.where(c < n_chunks - 1, 1.0, 0.0)
        wd = main.shape[1]
        ext_buf[0:SUBLANES, 0:wd] = up
        ext_buf[SUBLANES:SUBLANES + c_len, 0:wd] = main
        ext_buf[SUBLANES + c_len:2 * SUBLANES + c_len, 0:wd] = dn
        w = w_ref[...]
        acc = main * w[1:2]
        for j in (0, 2, 3):
            acc = acc + ext_buf[SUBLANES - 1 + j:SUBLANES - 1 + j + c_len, 0:wd] * w[j:j + 1]
        return _silu(acc)

    def l2n(z):
        return z * lax.rsqrt(jnp.sum(z * z, -1, keepdims=True) + EPS)

    incl_m = [tri_s <= tri_t, tri_s >= tri_t]
    strict_m = [tri_s < tri_t, tri_s > tri_t]
    n_cu = 8

    def phase_a(cg0):
        cgs = [cg0 + i for i in range(n_cu)]
        rows = [pl.ds(pl.multiple_of(cg * c_len, c_len), c_len) for cg in cgs]
        q = [l2n(conv_silu(q_ref, cq_ref, cg)) * (GD_DK ** -0.5) for cg in cgs]
        k = [l2n(conv_silu(k_ref, ck_ref, cg)) for cg in cgs]
        v2 = [conv_silu(v_ref, cv_ref, cg) for cg in cgs]
        ab = [ab_ref[rw, :] for rw in rows]
        k_t = [x.T for x in k]
        g_all = [-jnp.exp(alog_ref[...]) * _softplus(x + dtb_ref[...]) for x in ab]
        beta_all = [jax.nn.sigmoid(x) for x in ab]
        prefix = [_mm(lower, g, "nn", P_CUM) for g in g_all]
        g_kk = [_mm(x, y, "nn", P_GRAM) for x, y in zip(k, k_t)]
        g_qk = [_mm(x, y, "nn", P_GRAM) for x, y in zip(q, k_t)]
        gtot_all = [jnp.sum(g, axis=0, keepdims=True) for g in g_all]
        gc_all = [jnp.where(lane_ab < 64, p, t - p + g) for p, t, g in zip(prefix, gtot_all, g_all)]
        gct_all = [x.T for x in gc_all]
        per = []
        for i in range(n_cu):
            for d in range(2):
                for vl in range(2):
                    col_g = d * 64 + 2 * kh + vl
                    col_b = col_g + n_vh
                    gc_col = jnp.sum(jnp.where(lane_ab == col_g, gc_all[i], 0.0), axis=1, keepdims=True)
                    beta = jnp.sum(jnp.where(lane_ab == col_b, beta_all[i], 0.0), axis=1, keepdims=True)
                    gc_row = jnp.sum(jnp.where(row_abt == col_g, gct_all[i], 0.0), axis=0, keepdims=True)
                    g_last = jnp.sum(jnp.where(lane_1 == col_g, gtot_all[i], 0.0), axis=1, keepdims=True)
                    decay = jnp.where(incl_m[d], jnp.exp(jnp.minimum(gc_col - gc_row, 0.0)), 0.0)
                    a_mat = jnp.where(strict_m[d], beta * g_kk[i] * decay, 0.0)
                    per.append((i, 2 * d + vl, gc_col, beta, gc_row, g_last, decay, a_mat))
        sel = [[n for n, p in enumerate(per) if p[1] // 2 == d] for d in range(2)]
        t_d = _unit_tri_inverse(*[jnp.stack([-per[n][7] for n in sl]) for sl in sel], inv_masks)
        t_all = [None] * len(per)
        for d in range(2):
            for m, n in enumerate(sel[d]):
                t_all[n] = t_d[d][m]
        rhs = [jnp.concatenate([v2[i][:, (j % 2) * GD_DV:(j % 2 + 1) * GD_DV] * beta,
                                k[i] * (beta * jnp.exp(gc_col))], axis=1)
               for i, j, gc_col, beta, _, _, _, _ in per]
        uw = [_mm(t_all[n], x, "nn", P_APPLY) for n, x in enumerate(rhs)]
        for n, (i, j, gc_col, beta, gc_row, g_last, decay, _) in enumerate(per):
            idx = cgs[i] * 4 + j
            u_buf[idx] = uw[n][:, :GD_DV]
            wq_buf[idx, 0:c_len, :] = uw[n][:, GD_DV:].astype(BF16)
            wq_buf[idx, c_len:, :] = (q[i] * jnp.exp(gc_col)).astype(BF16)
            attn_buf[idx] = (g_qk[i] * decay).astype(BF16)
            ket_buf[idx] = (k_t[i] * jnp.exp(g_last - gc_row)).astype(BF16)
            gl_buf[idx] = jnp.broadcast_to(jnp.exp(g_last), (SUBLANES, GD_DV))

    def a_body(i, carry):
        phase_a(n_cu * i)
        return carry

    lax.fori_loop(0, n_tot // n_cu, a_body, 0)

    def b_body(i, carry):
        chains = [(s, d, vl) for s in range(n_sb) for d in range(2) for vl in range(2)]
        cg = [s * n_chunks + (i if d == 0 else n_chunks - 1 - i) for s, d, _ in chains]
        idx = [c * 4 + 2 * d + vl for c, (_, d, vl) in zip(cg, chains)]
        s_mat = [s_ref[s * 4 + 2 * d + vl] for s, d, vl in chains]
        ws_qs = [_mm(wq_buf[n], x, "nn", P_APPLY) for n, x in zip(idx, s_mat)]
        v_new = [(u_buf[n] - x[0:c_len]).astype(BF16) for n, x in zip(idx, ws_qs)]
        o = [x[c_len:] + _mm(attn_buf[n], y, "nn", P_APPLY) for n, x, y in zip(idx, ws_qs, v_new)]
        s_new = [x * gl_buf[n, 0:1, :] + _mm(ket_buf[n], y, "nn", P_APPLY) for n, x, y in zip(idx, s_mat, v_new)]
        for n, (s, d, vl) in enumerate(chains):
            rows = pl.ds(pl.multiple_of(cg[n] * c_len, c_len), c_len)
            obuf[d, rows, pl.ds(vl * GD_DV, GD_DV)] = o[n]
            s_ref[s * 4 + 2 * d + vl] = s_new[n]
        return carry

    lax.fori_loop(0, n_chunks, b_body, 0)

    n_cc = 4

    def c_body(i, carry):
        tiles = [(pl.ds(pl.multiple_of((i * n_cc + q) * c_len, c_len), c_len), pl.ds(vl * GD_DV, GD_DV))
                 for q in range(n_cc) for vl in range(2)]
        o = [obuf[0, rw, cl] + obuf[1, rw, cl] for rw, cl in tiles]
        ms = [jnp.mean(x * x, -1, keepdims=True) for x in o]
        gate = [_silu(z_ref[rw, cl]) for rw, cl in tiles]
        for (rw, cl), x, m, g in zip(tiles, o, ms, gate):
            o_ref[rw, cl] = (x * lax.rsqrt(m + EPS) * nw_ref[...] * g).astype(o_ref.dtype)
        return carry

    lax.fori_loop(0, n_tot // n_cc, c_body, 0)
    for s in range(n_sb):
        for j in range(4):
            sfin_ref[s, j // 2, j % 2] = s_ref[s * 4 + j]


def _gdn_scan(proj, ab, conv_w, alog_row, dtb_row, nw, s0, row0, n_seq, t_len, n_sb):
    n_kh = 16
    blk_rows = n_sb * t_len
    assert row0 % blk_rows == 0 and n_seq % n_sb == 0 and (n_sb * t_len // CHUNK) % 2 == 0
    blk0 = row0 // blk_rows
    vw = 2 * GD_DV
    n_units = n_sb * (t_len // CHUNK) * 4
    in_specs = [pl.BlockSpec((blk_rows, GD_DK), lambda b, h: (blk0 + b, h)),
                pl.BlockSpec((blk_rows, GD_DK), lambda b, h: (blk0 + b, n_kh + h)),
                pl.BlockSpec((blk_rows, vw), lambda b, h: (blk0 + b, n_kh + h)),
                pl.BlockSpec((blk_rows, vw), lambda b, h: (blk0 + b, 2 * n_kh + h)),
                pl.BlockSpec((blk_rows, 128), lambda b, h: (blk0 + b, 0)),
                pl.BlockSpec((GD_CONV, GD_DK), lambda b, h: (0, h)),
                pl.BlockSpec((GD_CONV, GD_DK), lambda b, h: (0, n_kh + h)),
                pl.BlockSpec((GD_CONV, vw), lambda b, h: (0, n_kh + h)),
                pl.BlockSpec((1, 128), lambda b, h: (0, 0)),
                pl.BlockSpec((1, 128), lambda b, h: (0, 0)),
                pl.BlockSpec((1, GD_DV), lambda b, h: (0, 0))]
    args = [proj, proj, proj, proj, ab, conv_w, conv_w, conv_w, alog_row, dtb_row, nw]
    if s0 is not None:
        in_specs.append(pl.BlockSpec((n_sb, 2, 2, GD_DK, GD_DV), lambda b, h: (b, 0, h, 0, 0)))
        args.append(s0)
    o, sfin = pl.pallas_call(
        functools.partial(_gdn_kernel, t_len=t_len, n_sb=n_sb, has_s0=s0 is not None),
        grid=(n_seq // n_sb, n_kh),
        in_specs=in_specs,
        out_specs=[pl.BlockSpec((blk_rows, vw), lambda b, h: (b, h)),
                   pl.BlockSpec((n_sb, 2, 2, GD_DK, GD_DV), lambda b, h: (b, 0, h, 0, 0))],
        out_shape=[jax.ShapeDtypeStruct((n_seq * t_len, 2 * n_kh * GD_DV), BF16),
                   jax.ShapeDtypeStruct((n_seq, 2, 2 * n_kh, GD_DK, GD_DV), F32)],
        scratch_shapes=[pltpu.VMEM((n_units, 2 * CHUNK, GD_DV), BF16),
                        pltpu.VMEM((n_units, CHUNK, GD_DV), F32),
                        pltpu.VMEM((n_units, CHUNK, CHUNK), BF16),
                        pltpu.VMEM((n_units, GD_DK, CHUNK), BF16),
                        pltpu.VMEM((n_units, SUBLANES, GD_DV), F32),
                        pltpu.VMEM((2, blk_rows, vw), F32),
                        pltpu.VMEM((4 * n_sb, GD_DK, GD_DV), F32),
                        pltpu.VMEM((CHUNK + 2 * SUBLANES, vw), F32)],
        compiler_params=_cparams("arbitrary", "arbitrary"),
        name=f"gdn_scan_t{t_len}",
    )(*args)
    return o, sfin


def _grid_pos_embed(n_tokens, d_model):
    rows = n_tokens // GRID_W
    f32 = np.float32
    row = np.broadcast_to(np.arange(rows, dtype=f32)[:, None], (rows, GRID_W)).reshape(-1)
    col = np.broadcast_to(np.arange(GRID_W, dtype=f32)[None, :], (rows, GRID_W)).reshape(-1)
    quarter = d_model // 4
    omega = (f32(1.0) / (f32(POS_BASE) ** (np.arange(quarter, dtype=f32) / f32(quarter)))).astype(f32)
    ar = (row[:, None] * omega).astype(f32)
    ac = (col[:, None] * omega).astype(f32)
    return jnp.asarray(np.concatenate([np.sin(ar), np.cos(ar), np.sin(ac), np.cos(ac)], -1).astype(f32))


def _pad_axis(a, axis, size):
    pad = [(0, 0)] * a.ndim
    pad[axis] = (0, size - a.shape[axis])
    return jnp.pad(a, pad)


def kernel(x_prompt, x_sample, state_rwkv, state_gdn, c, c_ctx, norm_mix, norm_ffn, norm_final, w_mod, b_mod, ffn_w1, ffn_w3, ffn_w2, rw_mu, rw_wr, rw_wk, rw_wv, rw_wo, rw_w0, rw_w1, rw_w2, rw_a0, rw_a1, rw_a2, rw_g1, rw_g2, rw_kk, rw_ka, rw_rk, rw_ln_w, rw_ln_b, gd_w_in, gd_conv, gd_a_log, gd_dt_bias, gd_norm, gd_w_out):
    n_p, t_p, d = x_prompt.shape
    n_s, t_s, _ = x_sample.shape
    bf = lambda a: a.astype(BF16)
    row = lambda a: a.reshape(1, -1)
    tm_shift = 256
    tm_big = 512

    regions = [
        dict(x=x_prompt.reshape(n_p * t_p, d), n_seq=n_p, t_len=t_p, groups=[0] * n_p, n_sb=2, n_sb_gd=4,
             s0_rw=None, s0_gd=None),
        dict(x=(x_sample + _grid_pos_embed(t_s, d)[None]).reshape(n_s * t_s, d), n_seq=n_s, t_len=t_s,
             groups=list(range(1, n_s + 1)), n_sb=1, n_sb_gd=1,
             s0_rw=state_rwkv[:, 0].reshape(n_s, 2, d, RW_HEAD), s0_gd=state_gdn[:, 0]),
    ]
    for reg in regions:
        layout = [(reg["n_seq"], reg["t_len"], reg["groups"])]
        reg["tables"] = _tile_tables(layout, tm_shift)
        reg["grp"] = _tile_tables(layout, tm_big)[0]

    cond = _pad_axis(jnp.concatenate([c_ctx[None], c], axis=0), 0, SUBLANES)
    mod = _modulation(cond, w_mod, b_mod).reshape(w_mod.shape[0], SUBLANES, 6, d)

    w_rkv = bf(jnp.stack([rw_wr[0], rw_wk[0], rw_wv[0]]))
    lp = RW_LORA_PAD
    lora_w = (bf(rw_g1[0]), bf(rw_g2[0]),
              bf(_pad_axis(rw_w1[0], 2, lp)), bf(_pad_axis(rw_w2[0], 1, lp)),
              bf(_pad_axis(rw_a1[0], 2, lp)), bf(_pad_axis(rw_a2[0], 1, lp)),
              rw_w0[0][:, None, :], rw_a0[0][:, None, :])
    scan_p = (row(rw_kk[0]), row(rw_ka[0]), row(rw_rk[0]), row(rw_ln_w[0]), row(rw_ln_b[0]))
    w_o, ffn0 = bf(rw_wo[0]), (bf(ffn_w1[0]), bf(ffn_w3[0]), bf(ffn_w2[0]))
    for reg in regions:
        rkv, gate, lw, a = _rwkv_inputs(reg["x"], mod[0], reg["tables"], row(norm_mix[0]), rw_mu[0], w_rkv,
                                        *lora_w, tm_shift)
        y, reg["s_rwkv"] = _rwkv_scan(rkv, gate, lw, a, *scan_p, reg["s0_rw"], 0, reg["n_seq"], reg["t_len"],
                                      reg["n_sb"])
        x = _out_proj(y, w_o, reg["x"], mod[0], reg["grp"], 2, tm_big)
        reg["x"] = _ffn(x, mod[0], reg["grp"], row(norm_ffn[0]), *ffn0, row(norm_final), False, tm_big)

    n_main = 3 * 4096
    w_in_main, w_in_ab = bf(gd_w_in[0][:, :n_main]), bf(gd_w_in[0][:, n_main:])
    zeros32 = jnp.zeros((2, 32), F32)
    alog_row = jnp.stack([gd_a_log[0], zeros32], axis=1).reshape(1, 128)
    dtb_row = jnp.stack([gd_dt_bias[0], zeros32], axis=1).reshape(1, 128)
    w_out, ffn1 = bf(gd_w_out[0]), (bf(ffn_w1[1]), bf(ffn_w3[1]), bf(ffn_w2[1]))
    for reg in regions:
        proj = _in_proj(reg["x"], mod[1], reg["grp"], row(norm_mix[1]), w_in_main, tm_big, 1024)
        ab = _in_proj(reg["x"], mod[1], reg["grp"], row(norm_mix[1]), w_in_ab, tm_big, 128)
        o, reg["s_gdn"] = _gdn_scan(proj, ab, gd_conv[0], alog_row, dtb_row, row(gd_norm[0]), reg["s0_gd"], 0,
                                    reg["n_seq"], reg["t_len"], reg["n_sb_gd"])
        x = _out_proj(o, w_out, reg["x"], mod[1], reg["grp"], 2, tm_big)
        reg["x"] = _ffn(x, mod[1], reg["grp"], row(norm_ffn[1]), *ffn1, row(norm_final), True, tm_big)

    y_prompt = regions[0]["x"].reshape(n_p, t_p, d)
    y_sample = regions[1]["x"].reshape(n_s, t_s, d)
    new_state_rwkv = regions[0]["s_rwkv"].reshape(n_p, 1, 2, d // RW_HEAD, RW_HEAD, RW_HEAD)
    new_state_gdn = regions[0]["s_gdn"].reshape(n_p, 1, 2, 32, GD_DK, GD_DV)
    return (y_prompt, y_sample, new_state_rwkv, new_state_gdn)
```

```python
import functools

import numpy as np
import jax
import jax.numpy as jnp
from jax import lax
from jax.experimental import pallas as pl
from jax.experimental.pallas import tpu as pltpu

F32 = jnp.float32
BF16 = jnp.bfloat16

EPS = 1e-6
GRID_W = 64
POS_BASE = 10000.0
RW_HEAD = 64
RW_GROUP_HEADS = 4
RW_LANES = RW_HEAD * RW_GROUP_HEADS
RW_LORA_PAD = 128
GD_DK = 128
GD_DV = 128
GD_CONV = 4
CHUNK = 64
RW_SUPER_CHUNKS = 8
SUBLANES = 8
VMEM_LIMIT = 56 * 1024 * 1024


def _cparams(*sem):
    return pltpu.CompilerParams(dimension_semantics=sem, vmem_limit_bytes=VMEM_LIMIT)


def _dot(a, b):
    return jnp.dot(a.astype(BF16), b.astype(BF16), preferred_element_type=F32)


_DIMS = {"nn": (((1,), (0,)), ((), ())),
         "nt": (((1,), (1,)), ((), ())),
         "tn": (((0,), (0,)), ((), ())),
         "bnn": (((2,), (1,)), ((0,), (0,)))}


def _split_bf16(x, pieces):
    out = []
    for _ in range(pieces - 1):
        p = x.astype(BF16)
        out.append(p)
        x = x - p.astype(F32)
    out.append(x.astype(BF16))
    return out


def _mm(a, b, dims="nn", mode="b"):
    dn = _DIMS[dims]
    dg = lambda x, y: lax.dot_general(x, y, dn, preferred_element_type=F32)
    if mode == "b":
        return dg(a.astype(BF16), b.astype(BF16))
    if mode in ("la", "la2"):
        a0 = a.astype(BF16)
        return sum(dg(a0, p) for p in _split_bf16(b, 3 if mode == "la" else 2))
    if mode in ("ra", "ra2"):
        b0 = b.astype(BF16)
        return sum(dg(p, b0) for p in _split_bf16(a, 3 if mode == "ra" else 2))
    raise ValueError(mode)


P_INV = "b"
P_GRAM = "b"
P_APPLY = "b"
P_SUM = "ra2"
P_CUM = "la2"


def _iota(shape, dim):
    return lax.broadcasted_iota(jnp.int32, shape, dim)


def _silu(x):
    return x * jax.nn.sigmoid(x)


def _softplus(x):
    return jnp.maximum(x, 0.0) + jnp.log(1.0 + jnp.exp(-jnp.abs(x)))


def _rms_mod(x, nw, sh, sc):
    y = x * lax.rsqrt(jnp.mean(x * x, -1, keepdims=True) + EPS)
    return (y * nw) * (1.0 + sc) + sh


def _inverse_masks(c_len):
    ij = _iota((c_len, c_len), 0) ^ _iota((c_len, c_len), 1)
    levels = []
    half = 2
    while half < c_len:
        levels.append(((ij >= half) & (ij < 2 * half)).astype(BF16))
        half *= 2
    return (ij == 0).astype(F32), (ij == 1).astype(F32), levels


def _unit_tri_inverse(n_lower, n_upper, masks):
    eye, pair, levels = masks
    mats = [n_lower, n_upper]
    c_len = n_lower.shape[-1]
    d_mats = [eye + m * pair for m in mats]
    n_bf = [m.astype(BF16) for m in mats]
    half = 2
    for lvl in levels:
        l_mats = [m * lvl for m in n_bf]
        if half < SUBLANES:
            dl = [_mm(d, l, "bnn", P_INV) for d, l in zip(d_mats, l_mats)]
            d_mats = [d + _mm(x, d, "bnn", P_INV) for d, x in zip(d_mats, dl)]
        else:
            first = [slice(b0, b0 + half) for b0 in range(0, c_len, 2 * half)]
            second = [slice(b0 + half, b0 + 2 * half) for b0 in range(0, c_len, 2 * half)]
            act = [second, first]
            d_act = [jnp.concatenate([d[:, sl, :] for sl in rows], axis=1) for d, rows in zip(d_mats, act)]
            dl = [_mm(x, l, "bnn", P_INV) for x, l in zip(d_act, l_mats)]
            upd = [x + _mm(y, d, "bnn", P_INV) for x, y, d in zip(d_act, dl, d_mats)]
            out = []
            for g, (d, u) in enumerate(zip(d_mats, upd)):
                parts = []
                for n in range(len(first)):
                    new = u[:, n * half:(n + 1) * half, :]
                    parts += [d[:, first[n], :], new] if g == 0 else [new, d[:, second[n], :]]
                out.append(jnp.concatenate(parts, axis=1))
            d_mats = out
        half *= 2
    return d_mats


def _mod_kernel(c_ref, w_ref, b_ref, o_ref):
    o_ref[0] = _dot(_silu(c_ref[...]), w_ref[0]) + b_ref[0]


def _modulation(cond, w_mod, b_mod):
    depth, d, n = w_mod.shape
    tn = 1024
    return pl.pallas_call(
        _mod_kernel,
        grid=(depth, n // tn),
        in_specs=[pl.BlockSpec((SUBLANES, d), lambda l, j: (0, 0)),
                  pl.BlockSpec((1, d, tn), lambda l, j: (l, 0, j)),
                  pl.BlockSpec((1, 1, tn), lambda l, j: (l, 0, j))],
        out_specs=pl.BlockSpec((1, SUBLANES, tn), lambda l, j: (l, 0, j)),
        out_shape=jax.ShapeDtypeStruct((depth, SUBLANES, n), F32),
        compiler_params=_cparams("arbitrary", "arbitrary"),
        name="modulation",
    )(cond, w_mod, b_mod.reshape(depth, 1, n))


def _shift_mix(x_ref, xp_ref, xn_ref, mod_ref, nw_ref, has_prev, has_next):
    nw = nw_ref[...]
    sh = mod_ref[0, 0:1, :]
    sc = mod_ref[0, 1:2, :]
    h = _rms_mod(x_ref[...], nw, sh, sc)
    tm = h.shape[0]
    hp = _rms_mod(xp_ref[SUBLANES - 1:SUBLANES, :], nw, sh, sc) * has_prev
    hn = _rms_mod(xn_ref[0:1, :], nw, sh, sc) * has_next
    rows = _iota(h.shape, 0)
    prev = jnp.where(rows == 0, hp, pltpu.roll(h, 1, 0))
    nxt = jnp.where(rows == tm - 1, hn, pltpu.roll(h, tm - 1, 0))
    return h, 0.5 * (prev + nxt) - h


def _rkv_kernel(grp_ref, hp_ref, hn_ref, x_ref, xp_ref, xn_ref, mod_ref, nw_ref, mu_ref, w_ref, o_ref):
    i = pl.program_id(1)
    h, xx = _shift_mix(x_ref, xp_ref, xn_ref, mod_ref, nw_ref,
                       hp_ref[i].astype(F32), hn_ref[i].astype(F32))
    o_ref[0] = _dot(h + xx * mu_ref[0], w_ref[0])


def _lora_kernel(grp_ref, hp_ref, hn_ref, x_ref, xp_ref, xn_ref, mod_ref, nw_ref, mu_ref,
                 g1_ref, g2_ref, w1_ref, w2_ref, a1_ref, a2_ref, w0_ref, a0_ref,
                 gate_ref, lw_ref, a_ref):
    i = pl.program_id(0)
    h, xx = _shift_mix(x_ref, xp_ref, xn_ref, mod_ref, nw_ref,
                       hp_ref[i].astype(F32), hn_ref[i].astype(F32))
    xw = h + xx * mu_ref[1:2, :]
    xa = h + xx * mu_ref[4:5, :]
    xg = h + xx * mu_ref[5:6, :]
    gate_ref[...] = _dot(jax.nn.sigmoid(_dot(xg, g1_ref[...])), g2_ref[...])
    for d in range(2):
        lw = jnp.tanh(_dot(xw, w1_ref[d]))
        w_log = -_softplus(-(w0_ref[d] + _dot(lw, w2_ref[d]))) - 0.5
        lw_ref[d] = -jnp.exp(w_log)
        a_ref[d] = jax.nn.sigmoid(a0_ref[d] + _dot(_dot(xa, a1_ref[d]), a2_ref[d]))


def _tile_tables(regions, tm):
    row_grp, row_pos, row_len = [], [], []
    for n_seq, t_len, groups in regions:
        for s in range(n_seq):
            row_grp += [groups[s]] * t_len
            row_pos += list(range(t_len))
            row_len += [t_len] * t_len
    row_grp, row_pos, row_len = (np.asarray(v).reshape(-1, tm) for v in (row_grp, row_pos, row_len))
    assert (row_grp == row_grp[:, :1]).all(), "a row tile must not straddle modulation groups"
    as_i32 = lambda v: jnp.asarray(np.asarray(v, np.int32))
    return (as_i32(row_grp[:, 0]), as_i32(row_pos[:, 0] > 0),
            as_i32(row_pos[:, -1] < row_len[:, -1] - 1))


def _halo_specs(tm, d, m_rows, n_lead):
    blocks = tm // SUBLANES
    last = m_rows // SUBLANES - 1
    if n_lead == 1:
        cur = lambda j, i, *_: (i, 0)
        prev = lambda j, i, *_: (jnp.maximum(i * blocks - 1, 0), 0)
        nxt = lambda j, i, *_: (jnp.minimum((i + 1) * blocks, last), 0)
    else:
        cur = lambda i, *_: (i, 0)
        prev = lambda i, *_: (jnp.maximum(i * blocks - 1, 0), 0)
        nxt = lambda i, *_: (jnp.minimum((i + 1) * blocks, last), 0)
    return [pl.BlockSpec((tm, d), cur), pl.BlockSpec((SUBLANES, d), prev), pl.BlockSpec((SUBLANES, d), nxt)]


def _rwkv_inputs(x, mod, tables, nw, mu, w_rkv, g1, g2, w1, w2, a1, a2, w0, a0, tm):
    m_rows, d = x.shape
    n_tiles = m_rows // tm
    mu_rkv = jnp.stack([mu[0], mu[2], mu[3]])[:, None, :]
    rkv = pl.pallas_call(
        _rkv_kernel,
        grid_spec=pltpu.PrefetchScalarGridSpec(
            num_scalar_prefetch=3, grid=(3, n_tiles),
            in_specs=_halo_specs(tm, d, m_rows, 1) + [
                pl.BlockSpec((1, 6, d), lambda j, i, g, *_: (g[i], 0, 0)),
                pl.BlockSpec((1, d), lambda j, i, *_: (0, 0)),
                pl.BlockSpec((1, 1, d), lambda j, i, *_: (j, 0, 0)),
                pl.BlockSpec((1, d, d), lambda j, i, *_: (j, 0, 0))],
            out_specs=pl.BlockSpec((1, tm, d), lambda j, i, *_: (j, i, 0))),
        out_shape=jax.ShapeDtypeStruct((3, m_rows, d), F32),
        compiler_params=_cparams("arbitrary", "arbitrary"),
        name="rwkv_rkv_proj",
    )(*tables, x, x, x, mod, nw, mu_rkv, w_rkv)

    full = lambda a: pl.BlockSpec(a.shape, lambda i, *_: (0,) * a.ndim)
    small = [g1, g2, w1, w2, a1, a2, w0, a0]
    gate, lw, a = pl.pallas_call(
        _lora_kernel,
        grid_spec=pltpu.PrefetchScalarGridSpec(
            num_scalar_prefetch=3, grid=(n_tiles,),
            in_specs=_halo_specs(tm, d, m_rows, 0) + [
                pl.BlockSpec((1, 6, d), lambda i, g, *_: (g[i], 0, 0)),
                pl.BlockSpec((1, d), lambda i, *_: (0, 0)),
                full(mu)] + [full(s) for s in small],
            out_specs=[pl.BlockSpec((tm, d), lambda i, *_: (i, 0)),
                       pl.BlockSpec((2, tm, d), lambda i, *_: (0, i, 0)),
                       pl.BlockSpec((2, tm, d), lambda i, *_: (0, i, 0))]),
        out_shape=[jax.ShapeDtypeStruct((m_rows, d), F32),
                   jax.ShapeDtypeStruct((2, m_rows, d), F32),
                   jax.ShapeDtypeStruct((2, m_rows, d), F32)],
        compiler_params=_cparams("arbitrary"),
        name="rwkv_lora_proj",
    )(*tables, x, x, x, mod, nw, mu, *small)
    return rkv, gate, lw, a


def _rwkv_scan_kernel(*refs, t_len, n_sb, has_s0):
    (r_ref, k_ref, v_ref, gate_ref, lw_ref, a_ref, kk_ref, ka_ref, rk_ref, lnw_ref, lnb_ref) = refs[:11]
    rest = refs[11:]
    if has_s0:
        s0_ref, rest = rest[0], rest[1:]
    (y_ref, sfin_ref, r2_buf, a2_buf, uv_buf, bk_buf, y0_buf, w_buf, ybuf, bonus_buf, s_ref) = rest
    c_len, lanes, heads = CHUNK, RW_LANES, RW_GROUP_HEADS
    n_chunks = t_len // c_len
    sc = min(n_chunks, RW_SUPER_CHUNKS)
    n_super = n_chunks // sc
    n_pp = max(1, 8 // (2 * n_sb))
    stack = heads * c_len

    same_head = (_iota((stack, lanes), 0) // c_len == _iota((stack, lanes), 1) // RW_HEAD).astype(F32)
    ones_blk = same_head
    rep = (_iota((RW_HEAD, lanes), 1) % RW_HEAD == _iota((RW_HEAD, lanes), 0)).astype(F32)
    t_idx = _iota((stack, c_len), 0) % c_len
    s_idx = _iota((stack, c_len), 1)
    tri_t = _iota((c_len, c_len), 0)
    tri_s = _iota((c_len, c_len), 1)

    inv_masks = _inverse_masks(c_len)
    kk_p, ka_p, rk_p = kk_ref[...], ka_ref[...], rk_ref[...]
    lnw, lnb = lnw_ref[...], lnb_ref[...]

    def fold(z):
        z = z * same_head
        return z[0:c_len] + z[c_len:2 * c_len] + z[2 * c_len:3 * c_len] + z[3 * c_len:4 * c_len]

    def tile4(z):
        return jnp.concatenate([z, z, z, z], axis=0) * same_head

    for s in range(n_sb):
        for d in range(2):
            if has_s0:
                s_ref[s * 2 + d] = _mm(s0_ref[s, d], rep, "nn", "ra") * ones_blk
            else:
                s_ref[s * 2 + d] = jnp.zeros((lanes, lanes), F32)
    ybuf[...] = jnp.zeros_like(ybuf)
    bonus_buf[...] = jnp.zeros_like(bonus_buf)

    def chunk_rows(s, d, p):
        c = p if d == 0 else n_chunks - 1 - p
        return pl.ds(pl.multiple_of((s * n_chunks + c) * c_len, c_len), c_len)

    incl_f = [(tri_s <= tri_t).astype(F32), (tri_s >= tri_t).astype(F32)]
    strict4 = [s_idx < t_idx, s_idx > t_idx]
    t_idx2 = _iota((stack, 2 * c_len), 0) % c_len
    s_idx2 = _iota((stack, 2 * c_len), 1) % c_len
    right = _iota((stack, 2 * c_len), 1) >= c_len
    strict_right = [right & (s_idx2 < t_idx2), right & (s_idx2 > t_idx2)]
    incl_both = [s_idx2 <= t_idx2, s_idx2 >= t_idx2]

    def phase_a(p0, j0):
        units = [(s, d, q) for s in range(n_sb) for d in range(2) for q in range(n_pp)]
        dirs = [d for _, d, _ in units]
        rows = [chunk_rows(s, d, p0 + q) for s, d, q in units]
        r = [r_ref[rw, :] for rw in rows]
        k = [k_ref[rw, :] for rw in rows]
        v = [v_ref[rw, :] for rw in rows]
        lw = [lw_ref[d, rw, :] for d, rw in zip(dirs, rows)]
        a = [a_ref[d, rw, :] for d, rw in zip(dirs, rows)]
        bonus_old = [bonus_buf[rw, :] for rw in rows]
        kx = [ki * kk_p for ki in k]
        ss = [_mm(x * x, ones_blk, "nn", P_SUM) for x in kx]
        cw = [_mm(incl_f[d], x, "nn", P_CUM) for d, x in zip(dirs, lw)]
        bsum = [_mm(ri * ki * (1.0 + (ai - 1.0) * ka_p) * rk_p, ones_blk, "nn", P_SUM)
                for ri, ki, ai in zip(r, k, a)]
        kkn = [x * lax.rsqrt(y + EPS) for x, y in zip(kx, ss)]
        kd = [ki * (1.0 + (ai - 1.0) * ka_p) for ki, ai in zip(k, a)]
        w_row = [jnp.exp(jnp.sum(x, axis=0, keepdims=True)) for x in lw]
        e_neg = [jnp.exp(-x) for x in cw]
        at = [-kn * jnp.exp(c - l) for kn, c, l in zip(kkn, cw, lw)]
        bt = [kn * ai * e for kn, ai, e in zip(kkn, a, e_neg)]
        kt = [x * e for x, e in zip(kd, e_neg)]
        rt = [ri * jnp.exp(c) for ri, c in zip(r, cw)]
        lhs = [jnp.concatenate([tile4(x), tile4(y)], axis=0) for x, y in zip(at, rt)]
        rhs = [jnp.concatenate([x, y], axis=0) for x, y in zip(bt, kt)]
        gram = [_mm(x, y, "nt", P_GRAM) for x, y in zip(lhs, rhs)]
        n_s = [jnp.where(strict4[d], g[0:stack, 0:c_len], 0.0) for d, g in zip(dirs, gram)]
        ak_m = [jnp.where(strict_right[d], g[0:stack], 0.0) for d, g in zip(dirs, gram)]
        r_m = [jnp.where(incl_both[d], g[stack:], 0.0) for d, g in zip(dirs, gram)]
        sel = [[i for i, di in enumerate(dirs) if di == d] for d in range(2)]
        t_d = _unit_tri_inverse(*[jnp.concatenate([n_s[i].reshape(heads, c_len, c_len) for i in sl], axis=0)
                                  for sl in sel], inv_masks)
        t_s = [None] * len(units)
        for d in range(2):
            for n, i in enumerate(sel[d]):
                t_s[i] = t_d[d][n * heads:(n + 1) * heads].reshape(stack, c_len)
        a2 = [fold(_mm(t, x, "nn", P_APPLY)) for t, x in zip(t_s, at)]
        akv = [fold(_mm(x, jnp.concatenate([vi, vi], axis=0), "nn", P_APPLY)) for x, vi in zip(ak_m, v)]
        u0 = [fold(_mm(t, x, "nn", P_APPLY)) for t, x in zip(t_s, akv)]
        r2 = [x + fold(_mm(rm[:, 0:c_len], y, "nn", P_APPLY)) for x, rm, y in zip(rt, r_m, a2)]
        y0 = [fold(_mm(rm, jnp.concatenate([x, vi], axis=0), "nn", P_APPLY)) for rm, x, vi in zip(r_m, u0, v)]
        for i, (s, d, q) in enumerate(units):
            slot = (s * 2 + d) * sc + j0 + q
            r2_buf[slot] = r2[i].astype(BF16)
            a2_buf[slot] = a2[i].astype(BF16)
            uv_buf[slot, 0:c_len, :] = u0[i].astype(BF16)
            uv_buf[slot, c_len:, :] = v[i].astype(BF16)
            bk_buf[slot, 0:c_len, :] = (bt[i] * w_row[i]).astype(BF16)
            bk_buf[slot, c_len:, :] = (kt[i] * w_row[i]).astype(BF16)
            y0_buf[slot] = y0[i]
            w_buf[slot] = jnp.broadcast_to(w_row[i], (SUBLANES, lanes))
            bonus_buf[rows[i], :] = bonus_old[i] + bsum[i] * v[i]

    def phase_b(p, j):
        units = [(s, d) for s in range(n_sb) for d in range(2)]
        rows = [chunk_rows(s, d, p) for s, d in units]
        slots = [(s * 2 + d) * sc + j for s, d in units]
        s_mat = [s_ref[s * 2 + d] for s, d in units]
        y_old = [ybuf[rw, :] for rw in rows]
        s_bf = [x.astype(BF16) for x in s_mat]
        m_s = [_mm(a2_buf[sl], bk_buf[sl, 0:c_len, :], "tn", P_APPLY) * ones_blk for sl in slots]
        s_add = [_mm(uv_buf[sl], bk_buf[sl], "tn", P_APPLY) * ones_blk for sl in slots]
        y = [_mm(r2_buf[sl], x, "nt", P_APPLY) + y0_buf[sl] for sl, x in zip(slots, s_bf)]
        s_new = [x * w_buf[sl, 0:1, :] + _mm(xb, m, "nn", P_APPLY) + z
                 for x, xb, m, z, sl in zip(s_mat, s_bf, m_s, s_add, slots)]
        for i, (s, d) in enumerate(units):
            ybuf[rows[i], :] = y_old[i] + y[i]
            s_ref[s * 2 + d] = s_new[i]

    def super_body(sp, carry):
        def a_body(j, c2):
            phase_a(sp * sc + j * n_pp, j * n_pp)
            return c2

        def b_body(j, c2):
            phase_b(sp * sc + j, j)
            return c2

        lax.fori_loop(0, sc // n_pp, a_body, 0)
        lax.fori_loop(0, sc, b_body, 0)
        return carry

    lax.fori_loop(0, n_super, super_body, 0)

    n_cc = 4

    def c_body(i, carry):
        rows = [pl.ds(pl.multiple_of((i * n_cc + q) * c_len, c_len), c_len) for q in range(n_cc)]
        y = [ybuf[rw, :] for rw in rows]
        mean = [_mm(x, ones_blk, "nn", P_SUM) * (1.0 / RW_HEAD) for x in y]
        yc = [x - m for x, m in zip(y, mean)]
        var = [_mm(x * x, ones_blk, "nn", P_SUM) * (1.0 / RW_HEAD) for x in yc]
        for rw, x, vr in zip(rows, yc, var):
            yn = x * lax.rsqrt(vr + RW_HEAD * 1e-5) * lnw + lnb
            y_ref[rw, :] = ((yn + bonus_buf[rw, :]) * gate_ref[rw, :]).astype(y_ref.dtype)
        return carry

    lax.fori_loop(0, n_sb * n_chunks // n_cc, c_body, 0)
    rep_t = (_iota((lanes, RW_HEAD), 0) % RW_HEAD == _iota((lanes, RW_HEAD), 1)).astype(F32)
    for s in range(n_sb):
        for d in range(2):
            sfin_ref[s, d] = _mm(s_ref[s * 2 + d], rep_t, "nn", "ra")


def _rwkv_scan(rkv, gate, lw, a, kk, ka, rk, lnw, lnb, s0, row0, n_seq, t_len, n_sb):
    _, m_rows, d = rkv.shape
    n_groups = d // RW_LANES
    blk_rows = n_sb * t_len
    assert row0 % blk_rows == 0 and n_seq % n_sb == 0
    blk0 = row0 // blk_rows
    sc = min(t_len // CHUNK, RW_SUPER_CHUNKS)
    assert (t_len // CHUNK) % sc == 0
    n_slots = n_sb * 2 * sc
    tok = lambda b, g: (blk0 + b, g)
    in_specs = [pl.BlockSpec((None, blk_rows, RW_LANES), lambda b, g, j=j: (j, blk0 + b, g)) for j in range(3)]
    in_specs += [pl.BlockSpec((blk_rows, RW_LANES), tok),
                 pl.BlockSpec((2, blk_rows, RW_LANES), lambda b, g: (0, blk0 + b, g)),
                 pl.BlockSpec((2, blk_rows, RW_LANES), lambda b, g: (0, blk0 + b, g))]
    in_specs += [pl.BlockSpec((1, RW_LANES), lambda b, g: (0, g))] * 5
    args = [rkv, rkv, rkv, gate, lw, a, kk, ka, rk, lnw, lnb]
    if s0 is not None:
        in_specs.append(pl.BlockSpec((n_sb, 2, RW_LANES, RW_HEAD), lambda b, g: (b, 0, g, 0)))
        args.append(s0)
    y, sfin = pl.pallas_call(
        functools.partial(_rwkv_scan_kernel, t_len=t_len, n_sb=n_sb, has_s0=s0 is not None),
        grid=(n_seq // n_sb, n_groups),
        in_specs=in_specs,
        out_specs=[pl.BlockSpec((blk_rows, RW_LANES), lambda b, g: (b, g)),
                   pl.BlockSpec((n_sb, 2, RW_LANES, RW_HEAD), lambda b, g: (b, 0, g, 0))],
        out_shape=[jax.ShapeDtypeStruct((n_seq * t_len, d), BF16),
                   jax.ShapeDtypeStruct((n_seq, 2, d, RW_HEAD), F32)],
        scratch_shapes=[pltpu.VMEM((n_slots, CHUNK, RW_LANES), BF16),
                        pltpu.VMEM((n_slots, CHUNK, RW_LANES), BF16),
                        pltpu.VMEM((n_slots, 2 * CHUNK, RW_LANES), BF16),
                        pltpu.VMEM((n_slots, 2 * CHUNK, RW_LANES), BF16),
                        pltpu.VMEM((n_slots, CHUNK, RW_LANES), F32),
                        pltpu.VMEM((n_slots, SUBLANES, RW_LANES), F32),
                        pltpu.VMEM((blk_rows, RW_LANES), F32),
                        pltpu.VMEM((blk_rows, RW_LANES), F32),
                        pltpu.VMEM((2 * n_sb, RW_LANES, RW_LANES), F32)],
        compiler_params=_cparams("arbitrary", "arbitrary"),
        name=f"rwkv_scan_t{t_len}",
    )(*args)
    return y, sfin


def _out_proj_kernel(grp_ref, y_ref, w_ref, x_ref, mod_ref, o_ref, *, gate_row):
    o_ref[...] = x_ref[...] + mod_ref[0, gate_row:gate_row + 1, :] * _dot(y_ref[...], w_ref[...])


def _out_proj(y, w, x, mod, grp, gate_row, tm):
    m_rows, k_dim = y.shape
    d = w.shape[1]
    tn = 1024
    return pl.pallas_call(
        functools.partial(_out_proj_kernel, gate_row=gate_row),
        grid_spec=pltpu.PrefetchScalarGridSpec(
            num_scalar_prefetch=1, grid=(d // tn, m_rows // tm),
            in_specs=[pl.BlockSpec((tm, k_dim), lambda n, i, g: (i, 0)),
                      pl.BlockSpec((k_dim, tn), lambda n, i, g: (0, n)),
                      pl.BlockSpec((tm, tn), lambda n, i, g: (i, n)),
                      pl.BlockSpec((1, 6, tn), lambda n, i, g: (g[i], 0, n))],
            out_specs=pl.BlockSpec((tm, tn), lambda n, i, g: (i, n))),
        out_shape=jax.ShapeDtypeStruct((m_rows, d), F32),
        compiler_params=_cparams("arbitrary", "arbitrary"),
        name="out_proj",
    )(grp, y, w, x, mod)


def _in_proj_kernel(grp_ref, x_ref, mod_ref, nw_ref, w_ref, o_ref, h_buf):
    @pl.when(pl.program_id(1) == 0)
    def _():
        h_buf[...] = _rms_mod(x_ref[...], nw_ref[...], mod_ref[0, 0:1, :], mod_ref[0, 1:2, :]).astype(BF16)

    o_ref[...] = jnp.dot(h_buf[...], w_ref[...], preferred_element_type=F32)


def _in_proj(x, mod, grp, nw, w, tm, tn):
    m_rows, d = x.shape
    n = w.shape[1]
    return pl.pallas_call(
        _in_proj_kernel,
        grid_spec=pltpu.PrefetchScalarGridSpec(
            num_scalar_prefetch=1, grid=(m_rows // tm, n // tn),
            in_specs=[pl.BlockSpec((tm, d), lambda i, n_, g: (i, 0)),
                      pl.BlockSpec((1, 6, d), lambda i, n_, g: (g[i], 0, 0)),
                      pl.BlockSpec((1, d), lambda i, n_, g: (0, 0)),
                      pl.BlockSpec((d, tn), lambda i, n_, g: (0, n_))],
            out_specs=pl.BlockSpec((tm, tn), lambda i, n_, g: (i, n_)),
            scratch_shapes=[pltpu.VMEM((tm, d), BF16)]),
        out_shape=jax.ShapeDtypeStruct((m_rows, n), F32),
        compiler_params=_cparams("arbitrary", "arbitrary"),
        name="gdn_in_proj",
    )(grp, x, mod, nw, w)


def _ffn_kernel(grp_ref, x_ref, mod_ref, nw_ref, w1_ref, w3_ref, w2_ref, fw_ref, o_ref, h_buf, acc,
                *, final_norm):
    f = pl.program_id(1)

    @pl.when(f == 0)
    def _():
        h_buf[...] = _rms_mod(x_ref[...], nw_ref[...], mod_ref[0, 3:4, :], mod_ref[0, 4:5, :]).astype(BF16)
        acc[...] = jnp.zeros_like(acc)

    h = h_buf[...]
    gate = jnp.dot(h, w1_ref[...], preferred_element_type=F32)
    up = jnp.dot(h, w3_ref[...], preferred_element_type=F32)
    acc[...] += _dot(_silu(gate) * up, w2_ref[...])

    @pl.when(f == pl.num_programs(1) - 1)
    def _():
        y = x_ref[...] + mod_ref[0, 5:6, :] * acc[...]
        if final_norm:
            y = y * lax.rsqrt(jnp.mean(y * y, -1, keepdims=True) + EPS) * fw_ref[...]
        o_ref[...] = y


def _ffn(x, mod, grp, nw, w1, w3, w2, fw, final_norm, tm):
    m_rows, d = x.shape
    d_ff = w1.shape[1]
    tf = 512
    return pl.pallas_call(
        functools.partial(_ffn_kernel, final_norm=final_norm),
        grid_spec=pltpu.PrefetchScalarGridSpec(
            num_scalar_prefetch=1, grid=(m_rows // tm, d_ff // tf),
            in_specs=[pl.BlockSpec((tm, d), lambda i, f, g: (i, 0)),
                      pl.BlockSpec((1, 6, d), lambda i, f, g: (g[i], 0, 0)),
                      pl.BlockSpec((1, d), lambda i, f, g: (0, 0)),
                      pl.BlockSpec((d, tf), lambda i, f, g: (0, f)),
                      pl.BlockSpec((d, tf), lambda i, f, g: (0, f)),
                      pl.BlockSpec((tf, d), lambda i, f, g: (f, 0)),
                      pl.BlockSpec((1, d), lambda i, f, g: (0, 0))],
            out_specs=pl.BlockSpec((tm, d), lambda i, f, g: (i, 0)),
            scratch_shapes=[pltpu.VMEM((tm, d), BF16), pltpu.VMEM((tm, d), F32)]),
        out_shape=jax.ShapeDtypeStruct((m_rows, d), F32),
        compiler_params=_cparams("arbitrary", "arbitrary"),
        name="ffn",
    )(grp, x, mod, nw, w1, w3, w2, fw)


def _gdn_kernel(*refs, t_len, n_sb, has_s0):
    (q_ref, k_ref, v_ref, z_ref, ab_ref, cq_ref, ck_ref, cv_ref, alog_ref, dtb_ref, nw_ref) = refs[:11]
    rest = refs[11:]
    if has_s0:
        s0_ref, rest = rest[0], rest[1:]
    o_ref, sfin_ref, wq_buf, u_buf, attn_buf, ket_buf, gl_buf, obuf, s_ref, ext_buf = rest
    c_len = CHUNK
    n_chunks = t_len // c_len
    n_tot = n_sb * n_chunks
    last_row0 = n_sb * t_len - SUBLANES
    kh = pl.program_id(1)
    n_vh = 32
    tri_t = _iota((c_len, c_len), 0)
    tri_s = _iota((c_len, c_len), 1)
    lane_ab = _iota((c_len, 128), 1)
    row_abt = _iota((128, c_len), 0)
    lane_1 = _iota((1, 128), 1)
    lower = (tri_s <= tri_t).astype(F32)
    inv_masks = _inverse_masks(c_len)

    for s in range(n_sb):
        for j in range(4):
            if has_s0:
                s_ref[s * 4 + j] = s0_ref[s, j // 2, j % 2]
            else:
                s_ref[s * 4 + j] = jnp.zeros((GD_DK, GD_DV), F32)

    def conv_silu(ref, w_ref, cg):
        c = cg % n_chunks
        r0 = pl.multiple_of(cg * c_len, c_len)
        main = ref[pl.ds(r0, c_len), :]
        up0 = pl.multiple_of(jnp.maximum(r0 - SUBLANES, 0), SUBLANES)
        dn0 = pl.multiple_of(jnp.minimum(r0 + c_len, last_row0), SUBLANES)
        up = ref[pl.ds(up0, SUBLANES), :] * jnp.where(c > 0, 1.0, 0.0)
        dn = ref[pl.ds(dn0, SUBLANES), :] * jnp.where(c < n_chunks - 1, 1.0, 0.0)
        wd = main.shape[1]
        ext_buf[0:SUBLANES, 0:wd] = up
        ext_buf[SUBLANES:SUBLANES + c_len, 0:wd] = main
        ext_buf[SUBLANES + c_len:2 * SUBLANES + c_len, 0:wd] = dn
        w = w_ref[...]
        acc = main * w[1:2]
        for j in (0, 2, 3):
            acc = acc + ext_buf[SUBLANES - 1 + j:SUBLANES - 1 + j + c_len, 0:wd] * w[j:j + 1]
        return _silu(acc)

    def l2n(z):
        return z * lax.rsqrt(jnp.sum(z * z, -1, keepdims=True) + EPS)

    incl_m = [tri_s <= tri_t, tri_s >= tri_t]
    strict_m = [tri_s < tri_t, tri_s > tri_t]
    n_cu = 8

    def phase_a(cg0):
        cgs = [cg0 + i for i in range(n_cu)]
        rows = [pl.ds(pl.multiple_of(cg * c_len, c_len), c_len) for cg in cgs]
        q = [l2n(conv_silu(q_ref, cq_ref, cg)) * (GD_DK ** -0.5) for cg in cgs]
        k = [l2n(conv_silu(k_ref, ck_ref, cg)) for cg in cgs]
        v2 = [conv_silu(v_ref, cv_ref, cg) for cg in cgs]
        ab = [ab_ref[rw, :] for rw in rows]
        k_t = [x.T for x in k]
        g_all = [-jnp.exp(alog_ref[...]) * _softplus(x + dtb_ref[...]) for x in ab]
        beta_all = [jax.nn.sigmoid(x) for x in ab]
        prefix = [_mm(lower, g, "nn", P_CUM) for g in g_all]
        g_kk = [_mm(x, y, "nn", P_GRAM) for x, y in zip(k, k_t)]
        g_qk = [_mm(x, y, "nn", P_GRAM) for x, y in zip(q, k_t)]
        gtot_all = [jnp.sum(g, axis=0, keepdims=True) for g in g_all]
        gc_all = [jnp.where(lane_ab < 64, p, t - p + g) for p, t, g in zip(prefix, gtot_all, g_all)]
        gct_all = [x.T for x in gc_all]
        per = []
        for i in range(n_cu):
            for d in range(2):
                for vl in range(2):
                    col_g = d * 64 + 2 * kh + vl
                    col_b = col_g + n_vh
                    gc_col = jnp.sum(jnp.where(lane_ab == col_g, gc_all[i], 0.0), axis=1, keepdims=True)
                    beta = jnp.sum(jnp.where(lane_ab == col_b, beta_all[i], 0.0), axis=1, keepdims=True)
                    gc_row = jnp.sum(jnp.where(row_abt == col_g, gct_all[i], 0.0), axis=0, keepdims=True)
                    g_last = jnp.sum(jnp.where(lane_1 == col_g, gtot_all[i], 0.0), axis=1, keepdims=True)
                    decay = jnp.where(incl_m[d], jnp.exp(jnp.minimum(gc_col - gc_row, 0.0)), 0.0)
                    a_mat = jnp.where(strict_m[d], beta * g_kk[i] * decay, 0.0)
                    per.append((i, 2 * d + vl, gc_col, beta, gc_row, g_last, decay, a_mat))
        sel = [[n for n, p in enumerate(per) if p[1] // 2 == d] for d in range(2)]
        t_d = _unit_tri_inverse(*[jnp.stack([-per[n][7] for n in sl]) for sl in sel], inv_masks)
        t_all = [None] * len(per)
        for d in range(2):
            for m, n in enumerate(sel[d]):
                t_all[n] = t_d[d][m]
        rhs = [jnp.concatenate([v2[i][:, (j % 2) * GD_DV:(j % 2 + 1) * GD_DV] * beta,
                                k[i] * (beta * jnp.exp(gc_col))], axis=1)
               for i, j, gc_col, beta, _, _, _, _ in per]
        uw = [_mm(t_all[n], x, "nn", P_APPLY) for n, x in enumerate(rhs)]
        for n, (i, j, gc_col, beta, gc_row, g_last, decay, _) in enumerate(per):
            idx = cgs[i] * 4 + j
            u_buf[idx] = uw[n][:, :GD_DV]
            wq_buf[idx, 0:c_len, :] = uw[n][:, GD_DV:].astype(BF16)
            wq_buf[idx, c_len:, :] = (q[i] * jnp.exp(gc_col)).astype(BF16)
            attn_buf[idx] = (g_qk[i] * decay).astype(BF16)
            ket_buf[idx] = (k_t[i] * jnp.exp(g_last - gc_row)).astype(BF16)
            gl_buf[idx] = jnp.broadcast_to(jnp.exp(g_last), (SUBLANES, GD_DV))

    def a_body(i, carry):
        phase_a(n_cu * i)
        return carry

    lax.fori_loop(0, n_tot // n_cu, a_body, 0)

    def b_body(i, carry):
        chains = [(s, d, vl) for s in range(n_sb) for d in range(2) for vl in range(2)]
        cg = [s * n_chunks + (i if d == 0 else n_chunks - 1 - i) for s, d, _ in chains]
        idx = [c * 4 + 2 * d + vl for c, (_, d, vl) in zip(cg, chains)]
        s_mat = [s_ref[s * 4 + 2 * d + vl] for s, d, vl in chains]
        ws_qs = [_mm(wq_buf[n], x, "nn", P_APPLY) for n, x in zip(idx, s_mat)]
        v_new = [(u_buf[n] - x[0:c_len]).astype(BF16) for n, x in zip(idx, ws_qs)]
        o = [x[c_len:] + _mm(attn_buf[n], y, "nn", P_APPLY) for n, x, y in zip(idx, ws_qs, v_new)]
        s_new = [x * gl_buf[n, 0:1, :] + _mm(ket_buf[n], y, "nn", P_APPLY) for n, x, y in zip(idx, s_mat, v_new)]
        for n, (s, d, vl) in enumerate(chains):
            rows = pl.ds(pl.multiple_of(cg[n] * c_len, c_len), c_len)
            obuf[d, rows, pl.ds(vl * GD_DV, GD_DV)] = o[n]
            s_ref[s * 4 + 2 * d + vl] = s_new[n]
        return carry

    lax.fori_loop(0, n_chunks, b_body, 0)

    n_cc = 4

    def c_body(i, carry):
        tiles = [(pl.ds(pl.multiple_of((i * n_cc + q) * c_len, c_len), c_len), pl.ds(vl * GD_DV, GD_DV))
                 for q in range(n_cc) for vl in range(2)]
        o = [obuf[0, rw, cl] + obuf[1, rw, cl] for rw, cl in tiles]
        ms = [jnp.mean(x * x, -1, keepdims=True) for x in o]
        gate = [_silu(z_ref[rw, cl]) for rw, cl in tiles]
        for (rw, cl), x, m, g in zip(tiles, o, ms, gate):
            o_ref[rw, cl] = (x * lax.rsqrt(m + EPS) * nw_ref[...] * g).astype(o_ref.dtype)
        return carry

    lax.fori_loop(0, n_tot // n_cc, c_body, 0)
    for s in range(n_sb):
        for j in range(4):
            sfin_ref[s, j // 2, j % 2] = s_ref[s * 4 + j]


def _gdn_scan(proj, ab, conv_w, alog_row, dtb_row, nw, s0, row0, n_seq, t_len, n_sb):
    n_kh = 16
    blk_rows = n_sb * t_len
    assert row0 % blk_rows == 0 and n_seq % n_sb == 0 and (n_sb * t_len // CHUNK) % 2 == 0
    blk0 = row0 // blk_rows
    vw = 2 * GD_DV
    n_units = n_sb * (t_len // CHUNK) * 4
    in_specs = [pl.BlockSpec((blk_rows, GD_DK), lambda b, h: (blk0 + b, h)),
                pl.BlockSpec((blk_rows, GD_DK), lambda b, h: (blk0 + b, n_kh + h)),
                pl.BlockSpec((blk_rows, vw), lambda b, h: (blk0 + b, n_kh + h)),
                pl.BlockSpec((blk_rows, vw), lambda b, h: (blk0 + b, 2 * n_kh + h)),
                pl.BlockSpec((blk_rows, 128), lambda b, h: (blk0 + b, 0)),
                pl.BlockSpec((GD_CONV, GD_DK), lambda b, h: (0, h)),
                pl.BlockSpec((GD_CONV, GD_DK), lambda b, h: (0, n_kh + h)),
                pl.BlockSpec((GD_CONV, vw), lambda b, h: (0, n_kh + h)),
                pl.BlockSpec((1, 128), lambda b, h: (0, 0)),
                pl.BlockSpec((1, 128), lambda b, h: (0, 0)),
                pl.BlockSpec((1, GD_DV), lambda b, h: (0, 0))]
    args = [proj, proj, proj, proj, ab, conv_w, conv_w, conv_w, alog_row, dtb_row, nw]
    if s0 is not None:
        in_specs.append(pl.BlockSpec((n_sb, 2, 2, GD_DK, GD_DV), lambda b, h: (b, 0, h, 0, 0)))
        args.append(s0)
    o, sfin = pl.pallas_call(
        functools.partial(_gdn_kernel, t_len=t_len, n_sb=n_sb, has_s0=s0 is not None),
        grid=(n_seq // n_sb, n_kh),
        in_specs=in_specs,
        out_specs=[pl.BlockSpec((blk_rows, vw), lambda b, h: (b, h)),
                   pl.BlockSpec((n_sb, 2, 2, GD_DK, GD_DV), lambda b, h: (b, 0, h, 0, 0))],
        out_shape=[jax.ShapeDtypeStruct((n_seq * t_len, 2 * n_kh * GD_DV), BF16),
                   jax.ShapeDtypeStruct((n_seq, 2, 2 * n_kh, GD_DK, GD_DV), F32)],
        scratch_shapes=[pltpu.VMEM((n_units, 2 * CHUNK, GD_DV), BF16),
                        pltpu.VMEM((n_units, CHUNK, GD_DV), F32),
                        pltpu.VMEM((n_units, CHUNK, CHUNK), BF16),
                        pltpu.VMEM((n_units, GD_DK, CHUNK), BF16),
                        pltpu.VMEM((n_units, SUBLANES, GD_DV), F32),
                        pltpu.VMEM((2, blk_rows, vw), F32),
                        pltpu.VMEM((4 * n_sb, GD_DK, GD_DV), F32),
                        pltpu.VMEM((CHUNK + 2 * SUBLANES, vw), F32)],
        compiler_params=_cparams("arbitrary", "arbitrary"),
        name=f"gdn_scan_t{t_len}",
    )(*args)
    return o, sfin


def _grid_pos_embed(n_tokens, d_model):
    rows = n_tokens // GRID_W
    f32 = np.float32
    row = np.broadcast_to(np.arange(rows, dtype=f32)[:, None], (rows, GRID_W)).reshape(-1)
    col = np.broadcast_to(np.arange(GRID_W, dtype=f32)[None, :], (rows, GRID_W)).reshape(-1)
    quarter = d_model // 4
    omega = (f32(1.0) / (f32(POS_BASE) ** (np.arange(quarter, dtype=f32) / f32(quarter)))).astype(f32)
    ar = (row[:, None] * omega).astype(f32)
    ac = (col[:, None] * omega).astype(f32)
    return jnp.asarray(np.concatenate([np.sin(ar), np.cos(ar), np.sin(ac), np.cos(ac)], -1).astype(f32))


def _pad_axis(a, axis, size):
    pad = [(0, 0)] * a.ndim
    pad[axis] = (0, size - a.shape[axis])
    return jnp.pad(a, pad)


def kernel(x_prompt, x_sample, state_rwkv, state_gdn, c, c_ctx, norm_mix, norm_ffn, norm_final, w_mod, b_mod, ffn_w1, ffn_w3, ffn_w2, rw_mu, rw_wr, rw_wk, rw_wv, rw_wo, rw_w0, rw_w1, rw_w2, rw_a0, rw_a1, rw_a2, rw_g1, rw_g2, rw_kk, rw_ka, rw_rk, rw_ln_w, rw_ln_b, gd_w_in, gd_conv, gd_a_log, gd_dt_bias, gd_norm, gd_w_out):
    n_p, t_p, d = x_prompt.shape
    n_s, t_s, _ = x_sample.shape
    bf = lambda a: a.astype(BF16)
    row = lambda a: a.reshape(1, -1)
    tm_shift = 256
    tm_big = 512
    tm_wide = 1024

    regions = [
        dict(x=x_prompt.reshape(n_p * t_p, d), n_seq=n_p, t_len=t_p, groups=[0] * n_p, n_sb=4, n_sb_gd=4,
             s0_rw=None, s0_gd=None),
        dict(x=(x_sample + _grid_pos_embed(t_s, d)[None]).reshape(n_s * t_s, d), n_seq=n_s, t_len=t_s,
             groups=list(range(1, n_s + 1)), n_sb=1, n_sb_gd=1,
             s0_rw=state_rwkv[:, 0].reshape(n_s, 2, d, RW_HEAD), s0_gd=state_gdn[:, 0]),
    ]
    for reg in regions:
        layout = [(reg["n_seq"], reg["t_len"], reg["groups"])]
        reg["tables"] = _tile_tables(layout, tm_shift)
        reg["grp"] = _tile_tables(layout, tm_big)[0]
        reg["grp_wide"] = _tile_tables(layout, tm_wide)[0]

    cond = _pad_axis(jnp.concatenate([c_ctx[None], c], axis=0), 0, SUBLANES)
    mod = _modulation(cond, w_mod, b_mod).reshape(w_mod.shape[0], SUBLANES, 6, d)

    w_rkv = bf(jnp.stack([rw_wr[0], rw_wk[0], rw_wv[0]]))
    lp = RW_LORA_PAD
    lora_w = (bf(rw_g1[0]), bf(rw_g2[0]),
              bf(_pad_axis(rw_w1[0], 2, lp)), bf(_pad_axis(rw_w2[0], 1, lp)),
              bf(_pad_axis(rw_a1[0], 2, lp)), bf(_pad_axis(rw_a2[0], 1, lp)),
              rw_w0[0][:, None, :], rw_a0[0][:, None, :])
    scan_p = (row(rw_kk[0]), row(rw_ka[0]), row(rw_rk[0]), row(rw_ln_w[0]), row(rw_ln_b[0]))
    w_o, ffn0 = bf(rw_wo[0]), (bf(ffn_w1[0]), bf(ffn_w3[0]), bf(ffn_w2[0]))
    for reg in regions:
        rkv, gate, lw, a = _rwkv_inputs(reg["x"], mod[0], reg["tables"], row(norm_mix[0]), rw_mu[0], w_rkv,
                                        *lora_w, tm_shift)
        y, reg["s_rwkv"] = _rwkv_scan(rkv, gate, lw, a, *scan_p, reg["s0_rw"], 0, reg["n_seq"], reg["t_len"],
                                      reg["n_sb"])
        x = _out_proj(y, w_o, reg["x"], mod[0], reg["grp"], 2, tm_big)
        reg["x"] = _ffn(x, mod[0], reg["grp"], row(norm_ffn[0]), *ffn0, row(norm_final), False, tm_big)

    n_main = 3 * 4096
    w_in_main, w_in_ab = bf(gd_w_in[0][:, :n_main]), bf(gd_w_in[0][:, n_main:])
    zeros32 = jnp.zeros((2, 32), F32)
    alog_row = jnp.stack([gd_a_log[0], zeros32], axis=1).reshape(1, 128)
    dtb_row = jnp.stack([gd_dt_bias[0], zeros32], axis=1).reshape(1, 128)
    w_out, ffn1 = bf(gd_w_out[0]), (bf(ffn_w1[1]), bf(ffn_w3[1]), bf(ffn_w2[1]))
    for reg in regions:
        proj = _in_proj(reg["x"], mod[1], reg["grp_wide"], row(norm_mix[1]), w_in_main, tm_wide, 1024)
        ab = _in_proj(reg["x"], mod[1], reg["grp_wide"], row(norm_mix[1]), w_in_ab, tm_wide, 128)
        o, reg["s_gdn"] = _gdn_scan(proj, ab, gd_conv[0], alog_row, dtb_row, row(gd_norm[0]), reg["s0_gd"], 0,
                                    reg["n_seq"], reg["t_len"], reg["n_sb_gd"])
        x = _out_proj(o, w_out, reg["x"], mod[1], reg["grp"], 2, tm_big)
        reg["x"] = _ffn(x, mod[1], reg["grp"], row(norm_ffn[1]), *ffn1, row(norm_final), True, tm_big)

    y_prompt = regions[0]["x"].reshape(n_p, t_p, d)
    y_sample = regions[1]["x"].reshape(n_s, t_s, d)
    new_state_rwkv = regions[0]["s_rwkv"].reshape(n_p, 1, 2, d // RW_HEAD, RW_HEAD, RW_HEAD)
    new_state_gdn = regions[0]["s_gdn"].reshape(n_p, 1, 2, 32, GD_DK, GD_DV)
    return (y_prompt, y_sample, new_state_rwkv, new_state_gdn)
```

```python
import functools

import numpy as np
import jax
import jax.numpy as jnp
from jax import lax
from jax.experimental import pallas as pl
from jax.experimental.pallas import tpu as pltpu

F32 = jnp.float32
BF16 = jnp.bfloat16

EPS = 1e-6
GRID_W = 64
POS_BASE = 10000.0
RW_HEAD = 64
RW_GROUP_HEADS = 4
RW_LANES = RW_HEAD * RW_GROUP_HEADS
RW_LORA_PAD = 128
GD_DK = 128
GD_DV = 128
GD_CONV = 4
CHUNK = 64
RW_SUPER_CHUNKS = 8
SUBLANES = 8
VMEM_LIMIT = 56 * 1024 * 1024


def _cparams(*sem):
    return pltpu.CompilerParams(dimension_semantics=sem, vmem_limit_bytes=VMEM_LIMIT)


def _dot(a, b):
    return jnp.dot(a.astype(BF16), b.astype(BF16), preferred_element_type=F32)


_DIMS = {"nn": (((1,), (0,)), ((), ())),
         "nt": (((1,), (1,)), ((), ())),
         "tn": (((0,), (0,)), ((), ())),
         "bnn": (((2,), (1,)), ((0,), (0,)))}


def _split_bf16(x, pieces):
    out = []
    for _ in range(pieces - 1):
        p = x.astype(BF16)
        out.append(p)
        x = x - p.astype(F32)
    out.append(x.astype(BF16))
    return out


def _mm(a, b, dims="nn", mode="b"):
    dn = _DIMS[dims]
    dg = lambda x, y: lax.dot_general(x, y, dn, preferred_element_type=F32)
    if mode == "b":
        return dg(a.astype(BF16), b.astype(BF16))
    if mode in ("la", "la2"):
        a0 = a.astype(BF16)
        return sum(dg(a0, p) for p in _split_bf16(b, 3 if mode == "la" else 2))
    if mode in ("ra", "ra2"):
        b0 = b.astype(BF16)
        return sum(dg(p, b0) for p in _split_bf16(a, 3 if mode == "ra" else 2))
    raise ValueError(mode)


P_INV = "b"
P_GRAM = "b"
P_APPLY = "b"
P_SUM = "ra2"
P_CUM = "la2"


def _iota(shape, dim):
    return lax.broadcasted_iota(jnp.int32, shape, dim)


def _silu(x):
    return x * jax.nn.sigmoid(x)


def _softplus(x):
    return jnp.maximum(x, 0.0) + jnp.log(1.0 + jnp.exp(-jnp.abs(x)))


def _rms_mod(x, nw, sh, sc):
    y = x * lax.rsqrt(jnp.mean(x * x, -1, keepdims=True) + EPS)
    return (y * nw) * (1.0 + sc) + sh


def _inverse_masks(c_len):
    ij = _iota((c_len, c_len), 0) ^ _iota((c_len, c_len), 1)
    levels = []
    half = 2
    while half < c_len:
        levels.append(((ij >= half) & (ij < 2 * half)).astype(BF16))
        half *= 2
    return (ij == 0).astype(F32), (ij == 1).astype(F32), levels


def _unit_tri_inverse(n_lower, n_upper, masks):
    eye, pair, levels = masks
    mats = [n_lower, n_upper]
    c_len = n_lower.shape[-1]
    d_mats = [eye + m * pair for m in mats]
    n_bf = [m.astype(BF16) for m in mats]
    half = 2
    for lvl in levels:
        l_mats = [m * lvl for m in n_bf]
        if half < SUBLANES:
            dl = [_mm(d, l, "bnn", P_INV) for d, l in zip(d_mats, l_mats)]
            d_mats = [d + _mm(x, d, "bnn", P_INV) for d, x in zip(d_mats, dl)]
        else:
            first = [slice(b0, b0 + half) for b0 in range(0, c_len, 2 * half)]
            second = [slice(b0 + half, b0 + 2 * half) for b0 in range(0, c_len, 2 * half)]
            act = [second, first]
            d_act = [jnp.concatenate([d[:, sl, :] for sl in rows], axis=1) for d, rows in zip(d_mats, act)]
            dl = [_mm(x, l, "bnn", P_INV) for x, l in zip(d_act, l_mats)]
            upd = [x + _mm(y, d, "bnn", P_INV) for x, y, d in zip(d_act, dl, d_mats)]
            out = []
            for g, (d, u) in enumerate(zip(d_mats, upd)):
                parts = []
                for n in range(len(first)):
                    new = u[:, n * half:(n + 1) * half, :]
                    parts += [d[:, first[n], :], new] if g == 0 else [new, d[:, second[n], :]]
                out.append(jnp.concatenate(parts, axis=1))
            d_mats = out
        half *= 2
    return d_mats


def _cast_kernel(w_ref, o_ref):
    o_ref[...] = w_ref[...].astype(BF16)


def _layer_bf16(w, layer, col0=0, n_cols=None):
    _, k_dim, n = w.shape
    n_cols = n if n_cols is None else n_cols
    assert col0 % n_cols == 0
    tr = 128
    return pl.pallas_call(
        _cast_kernel,
        grid=(k_dim // tr,),
        in_specs=[pl.BlockSpec((None, tr, n_cols), lambda i: (layer, i, col0 // n_cols))],
        out_specs=pl.BlockSpec((tr, n_cols), lambda i: (i, 0)),
        out_shape=jax.ShapeDtypeStruct((k_dim, n_cols), BF16),
        compiler_params=_cparams("arbitrary"),
        name="weight_to_bf16",
    )(w)


def _mod_kernel(c_ref, w_ref, b_ref, o_ref):
    o_ref[0] = _dot(_silu(c_ref[...]), w_ref[0]) + b_ref[0]


def _modulation(cond, w_mod, b_mod):
    depth, d, n = w_mod.shape
    tn = 1024
    return pl.pallas_call(
        _mod_kernel,
        grid=(depth, n // tn),
        in_specs=[pl.BlockSpec((SUBLANES, d), lambda l, j: (0, 0)),
                  pl.BlockSpec((1, d, tn), lambda l, j: (l, 0, j)),
                  pl.BlockSpec((1, 1, tn), lambda l, j: (l, 0, j))],
        out_specs=pl.BlockSpec((1, SUBLANES, tn), lambda l, j: (l, 0, j)),
        out_shape=jax.ShapeDtypeStruct((depth, SUBLANES, n), F32),
        compiler_params=_cparams("arbitrary", "arbitrary"),
        name="modulation",
    )(cond, w_mod, b_mod.reshape(depth, 1, n))


def _shift_mix(x_ref, xp_ref, xn_ref, mod_ref, nw_ref, has_prev, has_next):
    nw = nw_ref[...]
    sh = mod_ref[0, 0:1, :]
    sc = mod_ref[0, 1:2, :]
    h = _rms_mod(x_ref[...], nw, sh, sc)
    tm = h.shape[0]
    hp = _rms_mod(xp_ref[SUBLANES - 1:SUBLANES, :], nw, sh, sc) * has_prev
    hn = _rms_mod(xn_ref[0:1, :], nw, sh, sc) * has_next
    rows = _iota(h.shape, 0)
    prev = jnp.where(rows == 0, hp, pltpu.roll(h, 1, 0))
    nxt = jnp.where(rows == tm - 1, hn, pltpu.roll(h, tm - 1, 0))
    return h, 0.5 * (prev + nxt) - h


def _rkv_kernel(grp_ref, hp_ref, hn_ref, x_ref, xp_ref, xn_ref, mod_ref, nw_ref, mu_ref, w_ref, o_ref):
    i = pl.program_id(1)
    h, xx = _shift_mix(x_ref, xp_ref, xn_ref, mod_ref, nw_ref,
                       hp_ref[i].astype(F32), hn_ref[i].astype(F32))
    o_ref[0] = _dot(h + xx * mu_ref[0], w_ref[0])


def _lora_kernel(grp_ref, hp_ref, hn_ref, x_ref, xp_ref, xn_ref, mod_ref, nw_ref, mu_ref,
                 g1_ref, g2_ref, w1_ref, w2_ref, a1_ref, a2_ref, w0_ref, a0_ref,
                 gate_ref, lw_ref, a_ref):
    i = pl.program_id(0)
    h, xx = _shift_mix(x_ref, xp_ref, xn_ref, mod_ref, nw_ref,
                       hp_ref[i].astype(F32), hn_ref[i].astype(F32))
    xw = h + xx * mu_ref[1:2, :]
    xa = h + xx * mu_ref[4:5, :]
    xg = h + xx * mu_ref[5:6, :]
    gate_ref[...] = _dot(jax.nn.sigmoid(_dot(xg, g1_ref[...])), g2_ref[...])
    for d in range(2):
        lw = jnp.tanh(_dot(xw, w1_ref[d]))
        w_log = -_softplus(-(w0_ref[d] + _dot(lw, w2_ref[d]))) - 0.5
        lw_ref[d] = -jnp.exp(w_log)
        a_ref[d] = jax.nn.sigmoid(a0_ref[d] + _dot(_dot(xa, a1_ref[d]), a2_ref[d]))


def _tile_tables(regions, tm):
    row_grp, row_pos, row_len = [], [], []
    for n_seq, t_len, groups in regions:
        for s in range(n_seq):
            row_grp += [groups[s]] * t_len
            row_pos += list(range(t_len))
            row_len += [t_len] * t_len
    row_grp, row_pos, row_len = (np.asarray(v).reshape(-1, tm) for v in (row_grp, row_pos, row_len))
    assert (row_grp == row_grp[:, :1]).all(), "a row tile must not straddle modulation groups"
    as_i32 = lambda v: jnp.asarray(np.asarray(v, np.int32))
    return (as_i32(row_grp[:, 0]), as_i32(row_pos[:, 0] > 0),
            as_i32(row_pos[:, -1] < row_len[:, -1] - 1))


def _halo_specs(tm, d, m_rows, n_lead):
    blocks = tm // SUBLANES
    last = m_rows // SUBLANES - 1
    if n_lead == 1:
        cur = lambda j, i, *_: (i, 0)
        prev = lambda j, i, *_: (jnp.maximum(i * blocks - 1, 0), 0)
        nxt = lambda j, i, *_: (jnp.minimum((i + 1) * blocks, last), 0)
    else:
        cur = lambda i, *_: (i, 0)
        prev = lambda i, *_: (jnp.maximum(i * blocks - 1, 0), 0)
        nxt = lambda i, *_: (jnp.minimum((i + 1) * blocks, last), 0)
    return [pl.BlockSpec((tm, d), cur), pl.BlockSpec((SUBLANES, d), prev), pl.BlockSpec((SUBLANES, d), nxt)]


def _rwkv_inputs(x, mod, tables, nw, mu, w_rkv, g1, g2, w1, w2, a1, a2, w0, a0, tm):
    m_rows, d = x.shape
    n_tiles = m_rows // tm
    mu_rkv = jnp.stack([mu[0], mu[2], mu[3]])[:, None, :]
    rkv = pl.pallas_call(
        _rkv_kernel,
        grid_spec=pltpu.PrefetchScalarGridSpec(
            num_scalar_prefetch=3, grid=(3, n_tiles),
            in_specs=_halo_specs(tm, d, m_rows, 1) + [
                pl.BlockSpec((1, 6, d), lambda j, i, g, *_: (g[i], 0, 0)),
                pl.BlockSpec((1, d), lambda j, i, *_: (0, 0)),
                pl.BlockSpec((1, 1, d), lambda j, i, *_: (j, 0, 0)),
                pl.BlockSpec((1, d, d), lambda j, i, *_: (j, 0, 0))],
            out_specs=pl.BlockSpec((1, tm, d), lambda j, i, *_: (j, i, 0))),
        out_shape=jax.ShapeDtypeStruct((3, m_rows, d), F32),
        compiler_params=_cparams("arbitrary", "arbitrary"),
        name="rwkv_rkv_proj",
    )(*tables, x, x, x, mod, nw, mu_rkv, w_rkv)

    full = lambda a: pl.BlockSpec(a.shape, lambda i, *_: (0,) * a.ndim)
    small = [g1, g2, w1, w2, a1, a2, w0, a0]
    gate, lw, a = pl.pallas_call(
        _lora_kernel,
        grid_spec=pltpu.PrefetchScalarGridSpec(
            num_scalar_prefetch=3, grid=(n_tiles,),
            in_specs=_halo_specs(tm, d, m_rows, 0) + [
                pl.BlockSpec((1, 6, d), lambda i, g, *_: (g[i], 0, 0)),
                pl.BlockSpec((1, d), lambda i, *_: (0, 0)),
                full(mu)] + [full(s) for s in small],
            out_specs=[pl.BlockSpec((tm, d), lambda i, *_: (i, 0)),
                       pl.BlockSpec((2, tm, d), lambda i, *_: (0, i, 0)),
                       pl.BlockSpec((2, tm, d), lambda i, *_: (0, i, 0))]),
        out_shape=[jax.ShapeDtypeStruct((m_rows, d), F32),
                   jax.ShapeDtypeStruct((2, m_rows, d), F32),
                   jax.ShapeDtypeStruct((2, m_rows, d), F32)],
        compiler_params=_cparams("arbitrary"),
        name="rwkv_lora_proj",
    )(*tables, x, x, x, mod, nw, mu, *small)
    return rkv, gate, lw, a


def _rwkv_scan_kernel(*refs, t_len, n_sb, has_s0):
    (r_ref, k_ref, v_ref, gate_ref, lw_ref, a_ref, kk_ref, ka_ref, rk_ref, lnw_ref, lnb_ref) = refs[:11]
    rest = refs[11:]
    if has_s0:
        s0_ref, rest = rest[0], rest[1:]
    (y_ref, sfin_ref, r2_buf, a2_buf, uv_buf, bk_buf, y0_buf, w_buf, ybuf, bonus_buf, s_ref) = rest
    c_len, lanes, heads = CHUNK, RW_LANES, RW_GROUP_HEADS
    n_chunks = t_len // c_len
    sc = min(n_chunks, RW_SUPER_CHUNKS)
    n_super = n_chunks // sc
    n_pp = max(1, 8 // (2 * n_sb))
    stack = heads * c_len

    same_head = (_iota((stack, lanes), 0) // c_len == _iota((stack, lanes), 1) // RW_HEAD).astype(F32)
    ones_blk = same_head
    rep = (_iota((RW_HEAD, lanes), 1) % RW_HEAD == _iota((RW_HEAD, lanes), 0)).astype(F32)
    t_idx = _iota((stack, c_len), 0) % c_len
    s_idx = _iota((stack, c_len), 1)
    tri_t = _iota((c_len, c_len), 0)
    tri_s = _iota((c_len, c_len), 1)

    inv_masks = _inverse_masks(c_len)
    kk_p, ka_p, rk_p = kk_ref[...], ka_ref[...], rk_ref[...]
    lnw, lnb = lnw_ref[...], lnb_ref[...]

    def fold(z):
        z = z * same_head
        return z[0:c_len] + z[c_len:2 * c_len] + z[2 * c_len:3 * c_len] + z[3 * c_len:4 * c_len]

    def tile4(z):
        return jnp.concatenate([z, z, z, z], axis=0) * same_head

    for s in range(n_sb):
        for d in range(2):
            if has_s0:
                s_ref[s * 2 + d] = _mm(s0_ref[s, d], rep, "nn", "ra") * ones_blk
            else:
                s_ref[s * 2 + d] = jnp.zeros((lanes, lanes), F32)
    ybuf[...] = jnp.zeros_like(ybuf)
    bonus_buf[...] = jnp.zeros_like(bonus_buf)

    def chunk_rows(s, d, p):
        c = p if d == 0 else n_chunks - 1 - p
        return pl.ds(pl.multiple_of((s * n_chunks + c) * c_len, c_len), c_len)

    incl_f = [(tri_s <= tri_t).astype(F32), (tri_s >= tri_t).astype(F32)]
    strict4 = [s_idx < t_idx, s_idx > t_idx]
    t_idx2 = _iota((stack, 2 * c_len), 0) % c_len
    s_idx2 = _iota((stack, 2 * c_len), 1) % c_len
    right = _iota((stack, 2 * c_len), 1) >= c_len
    strict_right = [right & (s_idx2 < t_idx2), right & (s_idx2 > t_idx2)]
    incl_both = [s_idx2 <= t_idx2, s_idx2 >= t_idx2]

    def phase_a(p0, j0):
        units = [(s, d, q) for s in range(n_sb) for d in range(2) for q in range(n_pp)]
        dirs = [d for _, d, _ in units]
        rows = [chunk_rows(s, d, p0 + q) for s, d, q in units]
        r = [r_ref[rw, :] for rw in rows]
        k = [k_ref[rw, :] for rw in rows]
        v = [v_ref[rw, :] for rw in rows]
        lw = [lw_ref[d, rw, :] for d, rw in zip(dirs, rows)]
        a = [a_ref[d, rw, :] for d, rw in zip(dirs, rows)]
        bonus_old = [bonus_buf[rw, :] for rw in rows]
        kx = [ki * kk_p for ki in k]
        ss = [_mm(x * x, ones_blk, "nn", P_SUM) for x in kx]
        cw = [_mm(incl_f[d], x, "nn", P_CUM) for d, x in zip(dirs, lw)]
        bsum = [_mm(ri * ki * (1.0 + (ai - 1.0) * ka_p) * rk_p, ones_blk, "nn", P_SUM)
                for ri, ki, ai in zip(r, k, a)]
        kkn = [x * lax.rsqrt(y + EPS) for x, y in zip(kx, ss)]
        kd = [ki * (1.0 + (ai - 1.0) * ka_p) for ki, ai in zip(k, a)]
        w_row = [jnp.exp(jnp.sum(x, axis=0, keepdims=True)) for x in lw]
        e_neg = [jnp.exp(-x) for x in cw]
        at = [-kn * jnp.exp(c - l) for kn, c, l in zip(kkn, cw, lw)]
        bt = [kn * ai * e for kn, ai, e in zip(kkn, a, e_neg)]
        kt = [x * e for x, e in zip(kd, e_neg)]
        rt = [ri * jnp.exp(c) for ri, c in zip(r, cw)]
        lhs = [jnp.concatenate([tile4(x), tile4(y)], axis=0) for x, y in zip(at, rt)]
        rhs = [jnp.concatenate([x, y], axis=0) for x, y in zip(bt, kt)]
        gram = [_mm(x, y, "nt", P_GRAM) for x, y in zip(lhs, rhs)]
        n_s = [jnp.where(strict4[d], g[0:stack, 0:c_len], 0.0) for d, g in zip(dirs, gram)]
        ak_m = [jnp.where(strict_right[d], g[0:stack], 0.0) for d, g in zip(dirs, gram)]
        r_m = [jnp.where(incl_both[d], g[stack:], 0.0) for d, g in zip(dirs, gram)]
        sel = [[i for i, di in enumerate(dirs) if di == d] for d in range(2)]
        t_d = _unit_tri_inverse(*[jnp.concatenate([n_s[i].reshape(heads, c_len, c_len) for i in sl], axis=0)
                                  for sl in sel], inv_masks)
        t_s = [None] * len(units)
        for d in range(2):
            for n, i in enumerate(sel[d]):
                t_s[i] = t_d[d][n * heads:(n + 1) * heads].reshape(stack, c_len)
        a2 = [fold(_mm(t, x, "nn", P_APPLY)) for t, x in zip(t_s, at)]
        akv = [fold(_mm(x, jnp.concatenate([vi, vi], axis=0), "nn", P_APPLY)) for x, vi in zip(ak_m, v)]
        u0 = [fold(_mm(t, x, "nn", P_APPLY)) for t, x in zip(t_s, akv)]
        r2 = [x + fold(_mm(rm[:, 0:c_len], y, "nn", P_APPLY)) for x, rm, y in zip(rt, r_m, a2)]
        y0 = [fold(_mm(rm, jnp.concatenate([x, vi], axis=0), "nn", P_APPLY)) for rm, x, vi in zip(r_m, u0, v)]
        for i, (s, d, q) in enumerate(units):
            slot = (s * 2 + d) * sc + j0 + q
            r2_buf[slot] = r2[i].astype(BF16)
            a2_buf[slot] = a2[i].astype(BF16)
            uv_buf[slot, 0:c_len, :] = u0[i].astype(BF16)
            uv_buf[slot, c_len:, :] = v[i].astype(BF16)
            bk_buf[slot, 0:c_len, :] = (bt[i] * w_row[i]).astype(BF16)
            bk_buf[slot, c_len:, :] = (kt[i] * w_row[i]).astype(BF16)
            y0_buf[slot] = y0[i]
            w_buf[slot] = jnp.broadcast_to(w_row[i], (SUBLANES, lanes))
            bonus_buf[rows[i], :] = bonus_old[i] + bsum[i] * v[i]

    def phase_b(p, j):
        units = [(s, d) for s in range(n_sb) for d in range(2)]
        rows = [chunk_rows(s, d, p) for s, d in units]
        slots = [(s * 2 + d) * sc + j for s, d in units]
        s_mat = [s_ref[s * 2 + d] for s, d in units]
        y_old = [ybuf[rw, :] for rw in rows]
        s_bf = [x.astype(BF16) for x in s_mat]
        m_s = [_mm(a2_buf[sl], bk_buf[sl, 0:c_len, :], "tn", P_APPLY) * ones_blk for sl in slots]
        s_add = [_mm(uv_buf[sl], bk_buf[sl], "tn", P_APPLY) * ones_blk for sl in slots]
        y = [_mm(r2_buf[sl], x, "nt", P_APPLY) + y0_buf[sl] for sl, x in zip(slots, s_bf)]
        s_new = [x * w_buf[sl, 0:1, :] + _mm(xb, m, "nn", P_APPLY) + z
                 for x, xb, m, z, sl in zip(s_mat, s_bf, m_s, s_add, slots)]
        for i, (s, d) in enumerate(units):
            ybuf[rows[i], :] = y_old[i] + y[i]
            s_ref[s * 2 + d] = s_new[i]

    def super_body(sp, carry):
        def a_body(j, c2):
            phase_a(sp * sc + j * n_pp, j * n_pp)
            return c2

        def b_body(j, c2):
            phase_b(sp * sc + j, j)
            return c2

        lax.fori_loop(0, sc // n_pp, a_body, 0)
        lax.fori_loop(0, sc, b_body, 0)
        return carry

    lax.fori_loop(0, n_super, super_body, 0)

    n_cc = 4

    def c_body(i, carry):
        rows = [pl.ds(pl.multiple_of((i * n_cc + q) * c_len, c_len), c_len) for q in range(n_cc)]
        y = [ybuf[rw, :] for rw in rows]
        mean = [_mm(x, ones_blk, "nn", P_SUM) * (1.0 / RW_HEAD) for x in y]
        yc = [x - m for x, m in zip(y, mean)]
        var = [_mm(x * x, ones_blk, "nn", P_SUM) * (1.0 / RW_HEAD) for x in yc]
        for rw, x, vr in zip(rows, yc, var):
            yn = x * lax.rsqrt(vr + RW_HEAD * 1e-5) * lnw + lnb
            y_ref[rw, :] = ((yn + bonus_buf[rw, :]) * gate_ref[rw, :]).astype(y_ref.dtype)
        return carry

    lax.fori_loop(0, n_sb * n_chunks // n_cc, c_body, 0)
    rep_t = (_iota((lanes, RW_HEAD), 0) % RW_HEAD == _iota((lanes, RW_HEAD), 1)).astype(F32)
    for s in range(n_sb):
        for d in range(2):
            sfin_ref[s, d] = _mm(s_ref[s * 2 + d], rep_t, "nn", "ra")


def _rwkv_scan(rkv, gate, lw, a, kk, ka, rk, lnw, lnb, s0, row0, n_seq, t_len, n_sb):
    _, m_rows, d = rkv.shape
    n_groups = d // RW_LANES
    blk_rows = n_sb * t_len
    assert row0 % blk_rows == 0 and n_seq % n_sb == 0
    blk0 = row0 // blk_rows
    sc = min(t_len // CHUNK, RW_SUPER_CHUNKS)
    assert (t_len // CHUNK) % sc == 0
    n_slots = n_sb * 2 * sc
    tok = lambda b, g: (blk0 + b, g)
    in_specs = [pl.BlockSpec((None, blk_rows, RW_LANES), lambda b, g, j=j: (j, blk0 + b, g)) for j in range(3)]
    in_specs += [pl.BlockSpec((blk_rows, RW_LANES), tok),
                 pl.BlockSpec((2, blk_rows, RW_LANES), lambda b, g: (0, blk0 + b, g)),
                 pl.BlockSpec((2, blk_rows, RW_LANES), lambda b, g: (0, blk0 + b, g))]
    in_specs += [pl.BlockSpec((1, RW_LANES), lambda b, g: (0, g))] * 5
    args = [rkv, rkv, rkv, gate, lw, a, kk, ka, rk, lnw, lnb]
    if s0 is not None:
        in_specs.append(pl.BlockSpec((n_sb, 2, RW_LANES, RW_HEAD), lambda b, g: (b, 0, g, 0)))
        args.append(s0)
    y, sfin = pl.pallas_call(
        functools.partial(_rwkv_scan_kernel, t_len=t_len, n_sb=n_sb, has_s0=s0 is not None),
        grid=(n_seq // n_sb, n_groups),
        in_specs=in_specs,
        out_specs=[pl.BlockSpec((blk_rows, RW_LANES), lambda b, g: (b, g)),
                   pl.BlockSpec((n_sb, 2, RW_LANES, RW_HEAD), lambda b, g: (b, 0, g, 0))],
        out_shape=[jax.ShapeDtypeStruct((n_seq * t_len, d), BF16),
                   jax.ShapeDtypeStruct((n_seq, 2, d, RW_HEAD), F32)],
        scratch_shapes=[pltpu.VMEM((n_slots, CHUNK, RW_LANES), BF16),
                        pltpu.VMEM((n_slots, CHUNK, RW_LANES), BF16),
                        pltpu.VMEM((n_slots, 2 * CHUNK, RW_LANES), BF16),
                        pltpu.VMEM((n_slots, 2 * CHUNK, RW_LANES), BF16),
                        pltpu.VMEM((n_slots, CHUNK, RW_LANES), F32),
                        pltpu.VMEM((n_slots, SUBLANES, RW_LANES), F32),
                        pltpu.VMEM((blk_rows, RW_LANES), F32),
                        pltpu.VMEM((blk_rows, RW_LANES), F32),
                        pltpu.VMEM((2 * n_sb, RW_LANES, RW_LANES), F32)],
        compiler_params=_cparams("arbitrary", "arbitrary"),
        name=f"rwkv_scan_t{t_len}",
    )(*args)
    return y, sfin


def _out_proj_kernel(grp_ref, y_ref, w_ref, x_ref, mod_ref, o_ref, *, gate_row):
    o_ref[...] = x_ref[...] + mod_ref[0, gate_row:gate_row + 1, :] * _dot(y_ref[...], w_ref[...])


def _out_proj(y, w, x, mod, grp, gate_row, tm):
    m_rows, k_dim = y.shape
    d = w.shape[1]
    tn = 1024
    return pl.pallas_call(
        functools.partial(_out_proj_kernel, gate_row=gate_row),
        grid_spec=pltpu.PrefetchScalarGridSpec(
            num_scalar_prefetch=1, grid=(d // tn, m_rows // tm),
            in_specs=[pl.BlockSpec((tm, k_dim), lambda n, i, g: (i, 0)),
                      pl.BlockSpec((k_dim, tn), lambda n, i, g: (0, n)),
                      pl.BlockSpec((tm, tn), lambda n, i, g: (i, n)),
                      pl.BlockSpec((1, 6, tn), lambda n, i, g: (g[i], 0, n))],
            out_specs=pl.BlockSpec((tm, tn), lambda n, i, g: (i, n))),
        out_shape=jax.ShapeDtypeStruct((m_rows, d), F32),
        compiler_params=_cparams("arbitrary", "arbitrary"),
        name="out_proj",
    )(grp, y, w, x, mod)


def _in_proj_kernel(grp_ref, x_ref, mod_ref, nw_ref, w_ref, o_ref, h_buf):
    @pl.when(pl.program_id(1) == 0)
    def _():
        h_buf[...] = _rms_mod(x_ref[...], nw_ref[...], mod_ref[0, 0:1, :], mod_ref[0, 1:2, :]).astype(BF16)

    o_ref[...] = jnp.dot(h_buf[...], w_ref[...], preferred_element_type=F32)


def _in_proj(x, mod, grp, nw, w, tm, tn):
    m_rows, d = x.shape
    n = w.shape[1]
    return pl.pallas_call(
        _in_proj_kernel,
        grid_spec=pltpu.PrefetchScalarGridSpec(
            num_scalar_prefetch=1, grid=(m_rows // tm, n // tn),
            in_specs=[pl.BlockSpec((tm, d), lambda i, n_, g: (i, 0)),
                      pl.BlockSpec((1, 6, d), lambda i, n_, g: (g[i], 0, 0)),
                      pl.BlockSpec((1, d), lambda i, n_, g: (0, 0)),
                      pl.BlockSpec((d, tn), lambda i, n_, g: (0, n_))],
            out_specs=pl.BlockSpec((tm, tn), lambda i, n_, g: (i, n_)),
            scratch_shapes=[pltpu.VMEM((tm, d), BF16)]),
        out_shape=jax.ShapeDtypeStruct((m_rows, n), F32),
        compiler_params=_cparams("arbitrary", "arbitrary"),
        name="gdn_in_proj",
    )(grp, x, mod, nw, w)


def _ffn_kernel(grp_ref, x_ref, mod_ref, nw_ref, w1_ref, w3_ref, w2_ref, fw_ref, o_ref, h_buf, acc,
                *, final_norm):
    f = pl.program_id(1)

    @pl.when(f == 0)
    def _():
        h_buf[...] = _rms_mod(x_ref[...], nw_ref[...], mod_ref[0, 3:4, :], mod_ref[0, 4:5, :]).astype(BF16)
        acc[...] = jnp.zeros_like(acc)

    h = h_buf[...]
    gate = jnp.dot(h, w1_ref[...], preferred_element_type=F32)
    up = jnp.dot(h, w3_ref[...], preferred_element_type=F32)
    acc[...] += _dot(_silu(gate) * up, w2_ref[...])

    @pl.when(f == pl.num_programs(1) - 1)
    def _():
        y = x_ref[...] + mod_ref[0, 5:6, :] * acc[...]
        if final_norm:
            y = y * lax.rsqrt(jnp.mean(y * y, -1, keepdims=True) + EPS) * fw_ref[...]
        o_ref[...] = y


def _ffn(x, mod, grp, nw, w1, w3, w2, fw, final_norm, tm):
    m_rows, d = x.shape
    d_ff = w1.shape[1]
    tf = 512
    return pl.pallas_call(
        functools.partial(_ffn_kernel, final_norm=final_norm),
        grid_spec=pltpu.PrefetchScalarGridSpec(
            num_scalar_prefetch=1, grid=(m_rows // tm, d_ff // tf),
            in_specs=[pl.BlockSpec((tm, d), lambda i, f, g: (i, 0)),
                      pl.BlockSpec((1, 6, d), lambda i, f, g: (g[i], 0, 0)),
                      pl.BlockSpec((1, d), lambda i, f, g: (0, 0)),
                      pl.BlockSpec((d, tf), lambda i, f, g: (0, f)),
                      pl.BlockSpec((d, tf), lambda i, f, g: (0, f)),
                      pl.BlockSpec((tf, d), lambda i, f, g: (f, 0)),
                      pl.BlockSpec((1, d), lambda i, f, g: (0, 0))],
            out_specs=pl.BlockSpec((tm, d), lambda i, f, g: (i, 0)),
            scratch_shapes=[pltpu.VMEM((tm, d), BF16), pltpu.VMEM((tm, d), F32)]),
        out_shape=jax.ShapeDtypeStruct((m_rows, d), F32),
        compiler_params=_cparams("arbitrary", "arbitrary"),
        name="ffn",
    )(grp, x, mod, nw, w1, w3, w2, fw)


def _gdn_kernel(*refs, t_len, n_sb, has_s0):
    (q_ref, k_ref, v_ref, z_ref, ab_ref, cq_ref, ck_ref, cv_ref, alog_ref, dtb_ref, nw_ref) = refs[:11]
    rest = refs[11:]
    if has_s0:
        s0_ref, rest = rest[0], rest[1:]
    o_ref, sfin_ref, wq_buf, u_buf, attn_buf, ket_buf, gl_buf, obuf, s_ref, ext_buf = rest
    c_len = CHUNK
    n_chunks = t_len // c_len
    n_tot = n_sb * n_chunks
    last_row0 = n_sb * t_len - SUBLANES
    kh = pl.program_id(1)
    n_vh = 32
    tri_t = _iota((c_len, c_len), 0)
    tri_s = _iota((c_len, c_len), 1)
    lane_ab = _iota((c_len, 128), 1)
    row_abt = _iota((128, c_len), 0)
    lane_1 = _iota((1, 128), 1)
    lower = (tri_s <= tri_t).astype(F32)
    inv_masks = _inverse_masks(c_len)

    for s in range(n_sb):
        for j in range(4):
            if has_s0:
                s_ref[s * 4 + j] = s0_ref[s, j // 2, j % 2]
            else:
                s_ref[s * 4 + j] = jnp.zeros((GD_DK, GD_DV), F32)

    def conv_silu(ref, w_ref, cg):
        c = cg % n_chunks
        r0 = pl.multiple_of(cg * c_len, c_len)
        main = ref[pl.ds(r0, c_len), :]
        up0 = pl.multiple_of(jnp.maximum(r0 - SUBLANES, 0), SUBLANES)
        dn0 = pl.multiple_of(jnp.minimum(r0 + c_len, last_row0), SUBLANES)
        up = ref[pl.ds(up0, SUBLANES), :] * jnp.where(c > 0, 1.0, 0.0)
        dn = ref[pl.ds(dn0, SUBLANES), :] * jnp.where(c < n_chunks - 1, 1.0, 0.0)
        wd = main.shape[1]
        ext_buf[0:SUBLANES, 0:wd] = up
        ext_buf[SUBLANES:SUBLANES + c_len, 0:wd] = main
        ext_buf[SUBLANES + c_len:2 * SUBLANES + c_len, 0:wd] = dn
        w = w_ref[...]
        acc = main * w[1:2]
        for j in (0, 2, 3):
            acc = acc + ext_buf[SUBLANES - 1 + j:SUBLANES - 1 + j + c_len, 0:wd] * w[j:j + 1]
        return _silu(acc)

    def l2n(z):
        return z * lax.rsqrt(jnp.sum(z * z, -1, keepdims=True) + EPS)

    incl_m = [tri_s <= tri_t, tri_s >= tri_t]
    strict_m = [tri_s < tri_t, tri_s > tri_t]
    n_cu = 8

    def phase_a(cg0):
        cgs = [cg0 + i for i in range(n_cu)]
        rows = [pl.ds(pl.multiple_of(cg * c_len, c_len), c_len) for cg in cgs]
        q = [l2n(conv_silu(q_ref, cq_ref, cg)) * (GD_DK ** -0.5) for cg in cgs]
        k = [l2n(conv_silu(k_ref, ck_ref, cg)) for cg in cgs]
        v2 = [conv_silu(v_ref, cv_ref, cg) for cg in cgs]
        ab = [ab_ref[rw, :] for rw in rows]
        k_t = [x.T for x in k]
        g_all = [-jnp.exp(alog_ref[...]) * _softplus(x + dtb_ref[...]) for x in ab]
        beta_all = [jax.nn.sigmoid(x) for x in ab]
        prefix = [_mm(lower, g, "nn", P_CUM) for g in g_all]
        g_kk = [_mm(x, y, "nn", P_GRAM) for x, y in zip(k, k_t)]
        g_qk = [_mm(x, y, "nn", P_GRAM) for x, y in zip(q, k_t)]
        gtot_all = [jnp.sum(g, axis=0, keepdims=True) for g in g_all]
        gc_all = [jnp.where(lane_ab < 64, p, t - p + g) for p, t, g in zip(prefix, gtot_all, g_all)]
        gct_all = [x.T for x in gc_all]
        per = []
        for i in range(n_cu):
            for d in range(2):
                for vl in range(2):
                    col_g = d * 64 + 2 * kh + vl
                    col_b = col_g + n_vh
                    gc_col = jnp.sum(jnp.where(lane_ab == col_g, gc_all[i], 0.0), axis=1, keepdims=True)
                    beta = jnp.sum(jnp.where(lane_ab == col_b, beta_all[i], 0.0), axis=1, keepdims=True)
                    gc_row = jnp.sum(jnp.where(row_abt == col_g, gct_all[i], 0.0), axis=0, keepdims=True)
                    g_last = jnp.sum(jnp.where(lane_1 == col_g, gtot_all[i], 0.0), axis=1, keepdims=True)
                    decay = jnp.where(incl_m[d], jnp.exp(jnp.minimum(gc_col - gc_row, 0.0)), 0.0)
                    a_mat = jnp.where(strict_m[d], beta * g_kk[i] * decay, 0.0)
                    per.append((i, 2 * d + vl, gc_col, beta, gc_row, g_last, decay, a_mat))
        sel = [[n for n, p in enumerate(per) if p[1] // 2 == d] for d in range(2)]
        t_d = _unit_tri_inverse(*[jnp.stack([-per[n][7] for n in sl]) for sl in sel], inv_masks)
        t_all = [None] * len(per)
        for d in range(2):
            for m, n in enumerate(sel[d]):
                t_all[n] = t_d[d][m]
        rhs = [jnp.concatenate([v2[i][:, (j % 2) * GD_DV:(j % 2 + 1) * GD_DV] * beta,
                                k[i] * (beta * jnp.exp(gc_col))], axis=1)
               for i, j, gc_col, beta, _, _, _, _ in per]
        uw = [_mm(t_all[n], x, "nn", P_APPLY) for n, x in enumerate(rhs)]
        for n, (i, j, gc_col, beta, gc_row, g_last, decay, _) in enumerate(per):
            idx = cgs[i] * 4 + j
            u_buf[idx] = uw[n][:, :GD_DV]
            wq_buf[idx, 0:c_len, :] = uw[n][:, GD_DV:].astype(BF16)
            wq_buf[idx, c_len:, :] = (q[i] * jnp.exp(gc_col)).astype(BF16)
            attn_buf[idx] = (g_qk[i] * decay).astype(BF16)
            ket_buf[idx] = (k_t[i] * jnp.exp(g_last - gc_row)).astype(BF16)
            gl_buf[idx] = jnp.broadcast_to(jnp.exp(g_last), (SUBLANES, GD_DV))

    def a_body(i, carry):
        phase_a(n_cu * i)
        return carry

    lax.fori_loop(0, n_tot // n_cu, a_body, 0)

    def b_body(i, carry):
        chains = [(s, d, vl) for s in range(n_sb) for d in range(2) for vl in range(2)]
        cg = [s * n_chunks + (i if d == 0 else n_chunks - 1 - i) for s, d, _ in chains]
        idx = [c * 4 + 2 * d + vl for c, (_, d, vl) in zip(cg, chains)]
        s_mat = [s_ref[s * 4 + 2 * d + vl] for s, d, vl in chains]
        ws_qs = [_mm(wq_buf[n], x, "nn", P_APPLY) for n, x in zip(idx, s_mat)]
        v_new = [(u_buf[n] - x[0:c_len]).astype(BF16) for n, x in zip(idx, ws_qs)]
        o = [x[c_len:] + _mm(attn_buf[n], y, "nn", P_APPLY) for n, x, y in zip(idx, ws_qs, v_new)]
        s_new = [x * gl_buf[n, 0:1, :] + _mm(ket_buf[n], y, "nn", P_APPLY) for n, x, y in zip(idx, s_mat, v_new)]
        for n, (s, d, vl) in enumerate(chains):
            rows = pl.ds(pl.multiple_of(cg[n] * c_len, c_len), c_len)
            obuf[d, rows, pl.ds(vl * GD_DV, GD_DV)] = o[n]
            s_ref[s * 4 + 2 * d + vl] = s_new[n]
        return carry

    lax.fori_loop(0, n_chunks, b_body, 0)

    n_cc = 4

    def c_body(i, carry):
        tiles = [(pl.ds(pl.multiple_of((i * n_cc + q) * c_len, c_len), c_len), pl.ds(vl * GD_DV, GD_DV))
                 for q in range(n_cc) for vl in range(2)]
        o = [obuf[0, rw, cl] + obuf[1, rw, cl] for rw, cl in tiles]
        ms = [jnp.mean(x * x, -1, keepdims=True) for x in o]
        gate = [_silu(z_ref[rw, cl]) for rw, cl in tiles]
        for (rw, cl), x, m, g in zip(tiles, o, ms, gate):
            o_ref[rw, cl] = (x * lax.rsqrt(m + EPS) * nw_ref[...] * g).astype(o_ref.dtype)
        return carry

    lax.fori_loop(0, n_tot // n_cc, c_body, 0)
    for s in range(n_sb):
        for j in range(4):
            sfin_ref[s, j // 2, j % 2] = s_ref[s * 4 + j]


def _gdn_scan(proj, ab, conv_w, alog_row, dtb_row, nw, s0, row0, n_seq, t_len, n_sb):
    n_kh = 16
    blk_rows = n_sb * t_len
    assert row0 % blk_rows == 0 and n_seq % n_sb == 0 and (n_sb * t_len // CHUNK) % 2 == 0
    blk0 = row0 // blk_rows
    vw = 2 * GD_DV
    n_units = n_sb * (t_len // CHUNK) * 4
    in_specs = [pl.BlockSpec((blk_rows, GD_DK), lambda b, h: (blk0 + b, h)),
                pl.BlockSpec((blk_rows, GD_DK), lambda b, h: (blk0 + b, n_kh + h)),
                pl.BlockSpec((blk_rows, vw), lambda b, h: (blk0 + b, n_kh + h)),
                pl.BlockSpec((blk_rows, vw), lambda b, h: (blk0 + b, 2 * n_kh + h)),
                pl.BlockSpec((blk_rows, 128), lambda b, h: (blk0 + b, 0)),
                pl.BlockSpec((GD_CONV, GD_DK), lambda b, h: (0, h)),
                pl.BlockSpec((GD_CONV, GD_DK), lambda b, h: (0, n_kh + h)),
                pl.BlockSpec((GD_CONV, vw), lambda b, h: (0, n_kh + h)),
                pl.BlockSpec((1, 128), lambda b, h: (0, 0)),
                pl.BlockSpec((1, 128), lambda b, h: (0, 0)),
                pl.BlockSpec((1, GD_DV), lambda b, h: (0, 0))]
    args = [proj, proj, proj, proj, ab, conv_w, conv_w, conv_w, alog_row, dtb_row, nw]
    if s0 is not None:
        in_specs.append(pl.BlockSpec((n_sb, 2, 2, GD_DK, GD_DV), lambda b, h: (b, 0, h, 0, 0)))
        args.append(s0)
    o, sfin = pl.pallas_call(
        functools.partial(_gdn_kernel, t_len=t_len, n_sb=n_sb, has_s0=s0 is not None),
        grid=(n_seq // n_sb, n_kh),
        in_specs=in_specs,
        out_specs=[pl.BlockSpec((blk_rows, vw), lambda b, h: (b, h)),
                   pl.BlockSpec((n_sb, 2, 2, GD_DK, GD_DV), lambda b, h: (b, 0, h, 0, 0))],
        out_shape=[jax.ShapeDtypeStruct((n_seq * t_len, 2 * n_kh * GD_DV), BF16),
                   jax.ShapeDtypeStruct((n_seq, 2, 2 * n_kh, GD_DK, GD_DV), F32)],
        scratch_shapes=[pltpu.VMEM((n_units, 2 * CHUNK, GD_DV), BF16),
                        pltpu.VMEM((n_units, CHUNK, GD_DV), F32),
                        pltpu.VMEM((n_units, CHUNK, CHUNK), BF16),
                        pltpu.VMEM((n_units, GD_DK, CHUNK), BF16),
                        pltpu.VMEM((n_units, SUBLANES, GD_DV), F32),
                        pltpu.VMEM((2, blk_rows, vw), F32),
                        pltpu.VMEM((4 * n_sb, GD_DK, GD_DV), F32),
                        pltpu.VMEM((CHUNK + 2 * SUBLANES, vw), F32)],
        compiler_params=_cparams("arbitrary", "arbitrary"),
        name=f"gdn_scan_t{t_len}",
    )(*args)
    return o, sfin


def _grid_pos_embed(n_tokens, d_model):
    rows = n_tokens // GRID_W
    f32 = np.float32
    row = np.broadcast_to(np.arange(rows, dtype=f32)[:, None], (rows, GRID_W)).reshape(-1)
    col = np.broadcast_to(np.arange(GRID_W, dtype=f32)[None, :], (rows, GRID_W)).reshape(-1)
    quarter = d_model // 4
    omega = (f32(1.0) / (f32(POS_BASE) ** (np.arange(quarter, dtype=f32) / f32(quarter)))).astype(f32)
    ar = (row[:, None] * omega).astype(f32)
    ac = (col[:, None] * omega).astype(f32)
    return jnp.asarray(np.concatenate([np.sin(ar), np.cos(ar), np.sin(ac), np.cos(ac)], -1).astype(f32))


def _pad_axis(a, axis, size):
    pad = [(0, 0)] * a.ndim
    pad[axis] = (0, size - a.shape[axis])
    return jnp.pad(a, pad)


def kernel(x_prompt, x_sample, state_rwkv, state_gdn, c, c_ctx, norm_mix, norm_ffn, norm_final, w_mod, b_mod, ffn_w1, ffn_w3, ffn_w2, rw_mu, rw_wr, rw_wk, rw_wv, rw_wo, rw_w0, rw_w1, rw_w2, rw_a0, rw_a1, rw_a2, rw_g1, rw_g2, rw_kk, rw_ka, rw_rk, rw_ln_w, rw_ln_b, gd_w_in, gd_conv, gd_a_log, gd_dt_bias, gd_norm, gd_w_out):
    n_p, t_p, d = x_prompt.shape
    n_s, t_s, _ = x_sample.shape
    bf = lambda a: a.astype(BF16)
    row = lambda a: a.reshape(1, -1)
    tm_shift = 256
    tm_big = 512
    tm_wide = 1024

    regions = [
        dict(x=x_prompt.reshape(n_p * t_p, d), n_seq=n_p, t_len=t_p, groups=[0] * n_p, n_sb=4, n_sb_gd=4,
             s0_rw=None, s0_gd=None),
        dict(x=(x_sample + _grid_pos_embed(t_s, d)[None]).reshape(n_s * t_s, d), n_seq=n_s, t_len=t_s,
             groups=list(range(1, n_s + 1)), n_sb=1, n_sb_gd=1,
             s0_rw=state_rwkv[:, 0].reshape(n_s, 2, d, RW_HEAD), s0_gd=state_gdn[:, 0]),
    ]
    for reg in regions:
        layout = [(reg["n_seq"], reg["t_len"], reg["groups"])]
        reg["tables"] = _tile_tables(layout, tm_shift)
        reg["grp"] = _tile_tables(layout, tm_big)[0]
        reg["grp_wide"] = _tile_tables(layout, tm_wide)[0]

    cond = _pad_axis(jnp.concatenate([c_ctx[None], c], axis=0), 0, SUBLANES)
    mod = _modulation(cond, w_mod, b_mod).reshape(w_mod.shape[0], SUBLANES, 6, d)

    w_rkv = bf(jnp.stack([rw_wr[0], rw_wk[0], rw_wv[0]]))
    lp = RW_LORA_PAD
    lora_w = (bf(rw_g1[0]), bf(rw_g2[0]),
              bf(_pad_axis(rw_w1[0], 2, lp)), bf(_pad_axis(rw_w2[0], 1, lp)),
              bf(_pad_axis(rw_a1[0], 2, lp)), bf(_pad_axis(rw_a2[0], 1, lp)),
              rw_w0[0][:, None, :], rw_a0[0][:, None, :])
    scan_p = (row(rw_kk[0]), row(rw_ka[0]), row(rw_rk[0]), row(rw_ln_w[0]), row(rw_ln_b[0]))
    w_o, ffn0 = _layer_bf16(rw_wo, 0), (_layer_bf16(ffn_w1, 0), _layer_bf16(ffn_w3, 0), _layer_bf16(ffn_w2, 0))
    for reg in regions:
        rkv, gate, lw, a = _rwkv_inputs(reg["x"], mod[0], reg["tables"], row(norm_mix[0]), rw_mu[0], w_rkv,
                                        *lora_w, tm_shift)
        y, reg["s_rwkv"] = _rwkv_scan(rkv, gate, lw, a, *scan_p, reg["s0_rw"], 0, reg["n_seq"], reg["t_len"],
                                      reg["n_sb"])
        x = _out_proj(y, w_o, reg["x"], mod[0], reg["grp"], 2, tm_big)
        reg["x"] = _ffn(x, mod[0], reg["grp"], row(norm_ffn[0]), *ffn0, row(norm_final), False, tm_big)

    n_main = 3 * 4096
    w_in_main = _layer_bf16(gd_w_in, 0, 0, n_main)
    w_in_ab = _layer_bf16(gd_w_in, 0, n_main, gd_w_in.shape[2] - n_main)
    zeros32 = jnp.zeros((2, 32), F32)
    alog_row = jnp.stack([gd_a_log[0], zeros32], axis=1).reshape(1, 128)
    dtb_row = jnp.stack([gd_dt_bias[0], zeros32], axis=1).reshape(1, 128)
    w_out, ffn1 = _layer_bf16(gd_w_out, 0), (_layer_bf16(ffn_w1, 1), _layer_bf16(ffn_w3, 1), _layer_bf16(ffn_w2, 1))
    for reg in regions:
        proj = _in_proj(reg["x"], mod[1], reg["grp_wide"], row(norm_mix[1]), w_in_main, tm_wide, 1024)
        ab = _in_proj(reg["x"], mod[1], reg["grp_wide"], row(norm_mix[1]), w_in_ab, tm_wide, 128)
        o, reg["s_gdn"] = _gdn_scan(proj, ab, gd_conv[0], alog_row, dtb_row, row(gd_norm[0]), reg["s0_gd"], 0,
                                    reg["n_seq"], reg["t_len"], reg["n_sb_gd"])
        x = _out_proj(o, w_out, reg["x"], mod[1], reg["grp"], 2, tm_big)
        reg["x"] = _ffn(x, mod[1], reg["grp"], row(norm_ffn[1]), *ffn1, row(norm_final), True, tm_big)

    y_prompt = regions[0]["x"].reshape(n_p, t_p, d)
    y_sample = regions[1]["x"].reshape(n_s, t_s, d)
    new_state_rwkv = regions[0]["s_rwkv"].reshape(n_p, 1, 2, d // RW_HEAD, RW_HEAD, RW_HEAD)
    new_state_gdn = regions[0]["s_gdn"].reshape(n_p, 1, 2, 32, GD_DK, GD_DV)
    return (y_prompt, y_sample, new_state_rwkv, new_state_gdn)
```

```python
import functools

import numpy as np
import jax
import jax.numpy as jnp
from jax import lax
from jax.experimental import pallas as pl
from jax.experimental.pallas import tpu as pltpu

F32 = jnp.float32
BF16 = jnp.bfloat16

EPS = 1e-6
GRID_W = 64
POS_BASE = 10000.0
RW_HEAD = 64
RW_GROUP_HEADS = 4
RW_LANES = RW_HEAD * RW_GROUP_HEADS
RW_LORA_PAD = 128
GD_DK = 128
GD_DV = 128
GD_CONV = 4
CHUNK = 64
RW_SUPER_CHUNKS = 8
SUBLANES = 8
VMEM_LIMIT = 56 * 1024 * 1024
CAST_BLOCK_BYTES = 8 * 1024 * 1024


def _cparams(*sem):
    return pltpu.CompilerParams(dimension_semantics=sem, vmem_limit_bytes=VMEM_LIMIT)


def _dot(a, b):
    return jnp.dot(a.astype(BF16), b.astype(BF16), preferred_element_type=F32)


_DIMS = {"nn": (((1,), (0,)), ((), ())),
         "nt": (((1,), (1,)), ((), ())),
         "tn": (((0,), (0,)), ((), ())),
         "bnn": (((2,), (1,)), ((0,), (0,)))}


def _split_bf16(x, pieces):
    out = []
    for _ in range(pieces - 1):
        p = x.astype(BF16)
        out.append(p)
        x = x - p.astype(F32)
    out.append(x.astype(BF16))
    return out


def _mm(a, b, dims="nn", mode="b"):
    dn = _DIMS[dims]
    dg = lambda x, y: lax.dot_general(x, y, dn, preferred_element_type=F32)
    if mode == "b":
        return dg(a.astype(BF16), b.astype(BF16))
    if mode in ("la", "la2"):
        a0 = a.astype(BF16)
        return sum(dg(a0, p) for p in _split_bf16(b, 3 if mode == "la" else 2))
    if mode in ("ra", "ra2"):
        b0 = b.astype(BF16)
        return sum(dg(p, b0) for p in _split_bf16(a, 3 if mode == "ra" else 2))
    raise ValueError(mode)


P_INV = "b"
P_GRAM = "b"
P_APPLY = "b"
P_SUM = "ra2"
P_CUM = "la2"


def _iota(shape, dim):
    return lax.broadcasted_iota(jnp.int32, shape, dim)


def _silu(x):
    return x * jax.nn.sigmoid(x)


def _softplus(x):
    return jnp.maximum(x, 0.0) + jnp.log(1.0 + jnp.exp(-jnp.abs(x)))


def _rms_mod(x, nw, sh, sc):
    y = x * lax.rsqrt(jnp.mean(x * x, -1, keepdims=True) + EPS)
    return (y * nw) * (1.0 + sc) + sh


def _inverse_masks(c_len):
    ij = _iota((c_len, c_len), 0) ^ _iota((c_len, c_len), 1)
    levels = []
    half = 2
    while half < c_len:
        levels.append(((ij >= half) & (ij < 2 * half)).astype(BF16))
        half *= 2
    return (ij == 0).astype(F32), (ij == 1).astype(F32), levels


def _unit_tri_inverse(n_lower, n_upper, masks):
    eye, pair, levels = masks
    mats = [n_lower, n_upper]
    c_len = n_lower.shape[-1]
    d_mats = [eye + m * pair for m in mats]
    n_bf = [m.astype(BF16) for m in mats]
    half = 2
    for lvl in levels:
        l_mats = [m * lvl for m in n_bf]
        if half < SUBLANES:
            dl = [_mm(d, l, "bnn", P_INV) for d, l in zip(d_mats, l_mats)]
            d_mats = [d + _mm(x, d, "bnn", P_INV) for d, x in zip(d_mats, dl)]
        else:
            first = [slice(b0, b0 + half) for b0 in range(0, c_len, 2 * half)]
            second = [slice(b0 + half, b0 + 2 * half) for b0 in range(0, c_len, 2 * half)]
            act = [second, first]
            d_act = [jnp.concatenate([d[:, sl, :] for sl in rows], axis=1) for d, rows in zip(d_mats, act)]
            dl = [_mm(x, l, "bnn", P_INV) for x, l in zip(d_act, l_mats)]
            upd = [x + _mm(y, d, "bnn", P_INV) for x, y, d in zip(d_act, dl, d_mats)]
            out = []
            for g, (d, u) in enumerate(zip(d_mats, upd)):
                parts = []
                for n in range(len(first)):
                    new = u[:, n * half:(n + 1) * half, :]
                    parts += [d[:, first[n], :], new] if g == 0 else [new, d[:, second[n], :]]
                out.append(jnp.concatenate(parts, axis=1))
            d_mats = out
        half *= 2
    return d_mats


def _cast_kernel(w_ref, o_ref):
    o_ref[...] = w_ref[...].astype(BF16)


def _layer_bf16(w, layer, col0=0, n_cols=None):
    _, k_dim, n = w.shape
    n_cols = n if n_cols is None else n_cols
    assert col0 % n_cols == 0
    tr = 128
    while tr * 2 * n_cols * 4 <= CAST_BLOCK_BYTES and k_dim % (tr * 2) == 0:
        tr *= 2
    return pl.pallas_call(
        _cast_kernel,
        grid=(k_dim // tr,),
        in_specs=[pl.BlockSpec((None, tr, n_cols), lambda i: (layer, i, col0 // n_cols))],
        out_specs=pl.BlockSpec((tr, n_cols), lambda i: (i, 0)),
        out_shape=jax.ShapeDtypeStruct((k_dim, n_cols), BF16),
        compiler_params=_cparams("arbitrary"),
        name="weight_to_bf16",
    )(w)


def _mod_kernel(c_ref, w_ref, b_ref, o_ref):
    o_ref[0] = _dot(_silu(c_ref[...]), w_ref[0]) + b_ref[0]


def _modulation(cond, w_mod, b_mod):
    depth, d, n = w_mod.shape
    tn = 1024
    return pl.pallas_call(
        _mod_kernel,
        grid=(depth, n // tn),
        in_specs=[pl.BlockSpec((SUBLANES, d), lambda l, j: (0, 0)),
                  pl.BlockSpec((1, d, tn), lambda l, j: (l, 0, j)),
                  pl.BlockSpec((1, 1, tn), lambda l, j: (l, 0, j))],
        out_specs=pl.BlockSpec((1, SUBLANES, tn), lambda l, j: (l, 0, j)),
        out_shape=jax.ShapeDtypeStruct((depth, SUBLANES, n), F32),
        compiler_params=_cparams("arbitrary", "arbitrary"),
        name="modulation",
    )(cond, w_mod, b_mod.reshape(depth, 1, n))


def _shift_mix(x_ref, xp_ref, xn_ref, mod_ref, nw_ref, has_prev, has_next):
    nw = nw_ref[...]
    sh = mod_ref[0, 0:1, :]
    sc = mod_ref[0, 1:2, :]
    h = _rms_mod(x_ref[...], nw, sh, sc)
    tm = h.shape[0]
    hp = _rms_mod(xp_ref[SUBLANES - 1:SUBLANES, :], nw, sh, sc) * has_prev
    hn = _rms_mod(xn_ref[0:1, :], nw, sh, sc) * has_next
    rows = _iota(h.shape, 0)
    prev = jnp.where(rows == 0, hp, pltpu.roll(h, 1, 0))
    nxt = jnp.where(rows == tm - 1, hn, pltpu.roll(h, tm - 1, 0))
    return h, 0.5 * (prev + nxt) - h


def _rkv_kernel(grp_ref, hp_ref, hn_ref, x_ref, xp_ref, xn_ref, mod_ref, nw_ref, mu_ref, w_ref, o_ref):
    i = pl.program_id(1)
    h, xx = _shift_mix(x_ref, xp_ref, xn_ref, mod_ref, nw_ref,
                       hp_ref[i].astype(F32), hn_ref[i].astype(F32))
    o_ref[0] = _dot(h + xx * mu_ref[0], w_ref[0])


def _lora_kernel(grp_ref, hp_ref, hn_ref, x_ref, xp_ref, xn_ref, mod_ref, nw_ref, mu_ref,
                 g1_ref, g2_ref, w1_ref, w2_ref, a1_ref, a2_ref, w0_ref, a0_ref,
                 gate_ref, lw_ref, a_ref):
    i = pl.program_id(0)
    h, xx = _shift_mix(x_ref, xp_ref, xn_ref, mod_ref, nw_ref,
                       hp_ref[i].astype(F32), hn_ref[i].astype(F32))
    xw = h + xx * mu_ref[1:2, :]
    xa = h + xx * mu_ref[4:5, :]
    xg = h + xx * mu_ref[5:6, :]
    gate_ref[...] = _dot(jax.nn.sigmoid(_dot(xg, g1_ref[...])), g2_ref[...])
    for d in range(2):
        lw = jnp.tanh(_dot(xw, w1_ref[d]))
        w_log = -_softplus(-(w0_ref[d] + _dot(lw, w2_ref[d]))) - 0.5
        lw_ref[d] = -jnp.exp(w_log)
        a_ref[d] = jax.nn.sigmoid(a0_ref[d] + _dot(_dot(xa, a1_ref[d]), a2_ref[d]))


def _tile_tables(regions, tm):
    row_grp, row_pos, row_len = [], [], []
    for n_seq, t_len, groups in regions:
        for s in range(n_seq):
            row_grp += [groups[s]] * t_len
            row_pos += list(range(t_len))
            row_len += [t_len] * t_len
    row_grp, row_pos, row_len = (np.asarray(v).reshape(-1, tm) for v in (row_grp, row_pos, row_len))
    assert (row_grp == row_grp[:, :1]).all(), "a row tile must not straddle modulation groups"
    as_i32 = lambda v: jnp.asarray(np.asarray(v, np.int32))
    return (as_i32(row_grp[:, 0]), as_i32(row_pos[:, 0] > 0),
            as_i32(row_pos[:, -1] < row_len[:, -1] - 1))


def _halo_specs(tm, d, m_rows, n_lead):
    blocks = tm // SUBLANES
    last = m_rows // SUBLANES - 1
    if n_lead == 1:
        cur = lambda j, i, *_: (i, 0)
        prev = lambda j, i, *_: (jnp.maximum(i * blocks - 1, 0), 0)
        nxt = lambda j, i, *_: (jnp.minimum((i + 1) * blocks, last), 0)
    else:
        cur = lambda i, *_: (i, 0)
        prev = lambda i, *_: (jnp.maximum(i * blocks - 1, 0), 0)
        nxt = lambda i, *_: (jnp.minimum((i + 1) * blocks, last), 0)
    return [pl.BlockSpec((tm, d), cur), pl.BlockSpec((SUBLANES, d), prev), pl.BlockSpec((SUBLANES, d), nxt)]


def _rwkv_inputs(x, mod, tables, nw, mu, w_rkv, g1, g2, w1, w2, a1, a2, w0, a0, tm):
    m_rows, d = x.shape
    n_tiles = m_rows // tm
    mu_rkv = jnp.stack([mu[0], mu[2], mu[3]])[:, None, :]
    rkv = pl.pallas_call(
        _rkv_kernel,
        grid_spec=pltpu.PrefetchScalarGridSpec(
            num_scalar_prefetch=3, grid=(3, n_tiles),
            in_specs=_halo_specs(tm, d, m_rows, 1) + [
                pl.BlockSpec((1, 6, d), lambda j, i, g, *_: (g[i], 0, 0)),
                pl.BlockSpec((1, d), lambda j, i, *_: (0, 0)),
                pl.BlockSpec((1, 1, d), lambda j, i, *_: (j, 0, 0)),
                pl.BlockSpec((1, d, d), lambda j, i, *_: (j, 0, 0))],
            out_specs=pl.BlockSpec((1, tm, d), lambda j, i, *_: (j, i, 0))),
        out_shape=jax.ShapeDtypeStruct((3, m_rows, d), F32),
        compiler_params=_cparams("arbitrary", "arbitrary"),
        name="rwkv_rkv_proj",
    )(*tables, x, x, x, mod, nw, mu_rkv, w_rkv)

    full = lambda a: pl.BlockSpec(a.shape, lambda i, *_: (0,) * a.ndim)
    small = [g1, g2, w1, w2, a1, a2, w0, a0]
    gate, lw, a = pl.pallas_call(
        _lora_kernel,
        grid_spec=pltpu.PrefetchScalarGridSpec(
            num_scalar_prefetch=3, grid=(n_tiles,),
            in_specs=_halo_specs(tm, d, m_rows, 0) + [
                pl.BlockSpec((1, 6, d), lambda i, g, *_: (g[i], 0, 0)),
                pl.BlockSpec((1, d), lambda i, *_: (0, 0)),
                full(mu)] + [full(s) for s in small],
            out_specs=[pl.BlockSpec((tm, d), lambda i, *_: (i, 0)),
                       pl.BlockSpec((2, tm, d), lambda i, *_: (0, i, 0)),
                       pl.BlockSpec((2, tm, d), lambda i, *_: (0, i, 0))]),
        out_shape=[jax.ShapeDtypeStruct((m_rows, d), F32),
                   jax.ShapeDtypeStruct((2, m_rows, d), F32),
                   jax.ShapeDtypeStruct((2, m_rows, d), F32)],
        compiler_params=_cparams("arbitrary"),
        name="rwkv_lora_proj",
    )(*tables, x, x, x, mod, nw, mu, *small)
    return rkv, gate, lw, a


def _rwkv_scan_kernel(*refs, t_len, n_sb, has_s0):
    (r_ref, k_ref, v_ref, gate_ref, lw_ref, a_ref, kk_ref, ka_ref, rk_ref, lnw_ref, lnb_ref) = refs[:11]
    rest = refs[11:]
    if has_s0:
        s0_ref, rest = rest[0], rest[1:]
    (y_ref, sfin_ref, r2_buf, a2_buf, uv_buf, bk_buf, y0_buf, w_buf, ybuf, bonus_buf, s_ref) = rest
    c_len, lanes, heads = CHUNK, RW_LANES, RW_GROUP_HEADS
    n_chunks = t_len // c_len
    sc = min(n_chunks, RW_SUPER_CHUNKS)
    n_super = n_chunks // sc
    n_pp = max(1, 8 // (2 * n_sb))
    stack = heads * c_len

    same_head = (_iota((stack, lanes), 0) // c_len == _iota((stack, lanes), 1) // RW_HEAD).astype(F32)
    ones_blk = same_head
    rep = (_iota((RW_HEAD, lanes), 1) % RW_HEAD == _iota((RW_HEAD, lanes), 0)).astype(F32)
    t_idx = _iota((stack, c_len), 0) % c_len
    s_idx = _iota((stack, c_len), 1)
    tri_t = _iota((c_len, c_len), 0)
    tri_s = _iota((c_len, c_len), 1)

    inv_masks = _inverse_masks(c_len)
    kk_p, ka_p, rk_p = kk_ref[...], ka_ref[...], rk_ref[...]
    lnw, lnb = lnw_ref[...], lnb_ref[...]

    def fold(z):
        z = z * same_head
        return z[0:c_len] + z[c_len:2 * c_len] + z[2 * c_len:3 * c_len] + z[3 * c_len:4 * c_len]

    def tile4(z):
        return jnp.concatenate([z, z, z, z], axis=0) * same_head

    for s in range(n_sb):
        for d in range(2):
            if has_s0:
                s_ref[s * 2 + d] = _mm(s0_ref[s, d], rep, "nn", "ra") * ones_blk
            else:
                s_ref[s * 2 + d] = jnp.zeros((lanes, lanes), F32)
    ybuf[...] = jnp.zeros_like(ybuf)
    bonus_buf[...] = jnp.zeros_like(bonus_buf)

    def chunk_rows(s, d, p):
        c = p if d == 0 else n_chunks - 1 - p
        return pl.ds(pl.multiple_of((s * n_chunks + c) * c_len, c_len), c_len)

    incl_f = [(tri_s <= tri_t).astype(F32), (tri_s >= tri_t).astype(F32)]
    strict4 = [s_idx < t_idx, s_idx > t_idx]
    t_idx2 = _iota((stack, 2 * c_len), 0) % c_len
    s_idx2 = _iota((stack, 2 * c_len), 1) % c_len
    right = _iota((stack, 2 * c_len), 1) >= c_len
    strict_right = [right & (s_idx2 < t_idx2), right & (s_idx2 > t_idx2)]
    incl_both = [s_idx2 <= t_idx2, s_idx2 >= t_idx2]

    def phase_a(p0, j0):
        units = [(s, d, q) for s in range(n_sb) for d in range(2) for q in range(n_pp)]
        dirs = [d for _, d, _ in units]
        rows = [chunk_rows(s, d, p0 + q) for s, d, q in units]
        r = [r_ref[rw, :] for rw in rows]
        k = [k_ref[rw, :] for rw in rows]
        v = [v_ref[rw, :] for rw in rows]
        lw = [lw_ref[d, rw, :] for d, rw in zip(dirs, rows)]
        a = [a_ref[d, rw, :] for d, rw in zip(dirs, rows)]
        bonus_old = [bonus_buf[rw, :] for rw in rows]
        kx = [ki * kk_p for ki in k]
        ss = [_mm(x * x, ones_blk, "nn", P_SUM) for x in kx]
        cw = [_mm(incl_f[d], x, "nn", P_CUM) for d, x in zip(dirs, lw)]
        bsum = [_mm(ri * ki * (1.0 + (ai - 1.0) * ka_p) * rk_p, ones_blk, "nn", P_SUM)
                for ri, ki, ai in zip(r, k, a)]
        kkn = [x * lax.rsqrt(y + EPS) for x, y in zip(kx, ss)]
        kd = [ki * (1.0 + (ai - 1.0) * ka_p) for ki, ai in zip(k, a)]
        w_row = [jnp.exp(jnp.sum(x, axis=0, keepdims=True)) for x in lw]
        e_neg = [jnp.exp(-x) for x in cw]
        at = [-kn * jnp.exp(c - l) for kn, c, l in zip(kkn, cw, lw)]
        bt = [kn * ai * e for kn, ai, e in zip(kkn, a, e_neg)]
        kt = [x * e for x, e in zip(kd, e_neg)]
        rt = [ri * jnp.exp(c) for ri, c in zip(r, cw)]
        lhs = [jnp.concatenate([tile4(x), tile4(y)], axis=0) for x, y in zip(at, rt)]
        rhs = [jnp.concatenate([x, y], axis=0) for x, y in zip(bt, kt)]
        gram = [_mm(x, y, "nt", P_GRAM) for x, y in zip(lhs, rhs)]
        n_s = [jnp.where(strict4[d], g[0:stack, 0:c_len], 0.0) for d, g in zip(dirs, gram)]
        ak_m = [jnp.where(strict_right[d], g[0:stack], 0.0) for d, g in zip(dirs, gram)]
        r_m = [jnp.where(incl_both[d], g[stack:], 0.0) for d, g in zip(dirs, gram)]
        sel = [[i for i, di in enumerate(dirs) if di == d] for d in range(2)]
        t_d = _unit_tri_inverse(*[jnp.concatenate([n_s[i].reshape(heads, c_len, c_len) for i in sl], axis=0)
                                  for sl in sel], inv_masks)
        t_s = [None] * len(units)
        for d in range(2):
            for n, i in enumerate(sel[d]):
                t_s[i] = t_d[d][n * heads:(n + 1) * heads].reshape(stack, c_len)
        a2 = [fold(_mm(t, x, "nn", P_APPLY)) for t, x in zip(t_s, at)]
        akv = [fold(_mm(x, jnp.concatenate([vi, vi], axis=0), "nn", P_APPLY)) for x, vi in zip(ak_m, v)]
        u0 = [fold(_mm(t, x, "nn", P_APPLY)) for t, x in zip(t_s, akv)]
        r2 = [x + fold(_mm(rm[:, 0:c_len], y, "nn", P_APPLY)) for x, rm, y in zip(rt, r_m, a2)]
        y0 = [fold(_mm(rm, jnp.concatenate([x, vi], axis=0), "nn", P_APPLY)) for rm, x, vi in zip(r_m, u0, v)]
        for i, (s, d, q) in enumerate(units):
            slot = (s * 2 + d) * sc + j0 + q
            r2_buf[slot] = r2[i].astype(BF16)
            a2_buf[slot] = a2[i].astype(BF16)
            uv_buf[slot, 0:c_len, :] = u0[i].astype(BF16)
            uv_buf[slot, c_len:, :] = v[i].astype(BF16)
            bk_buf[slot, 0:c_len, :] = (bt[i] * w_row[i]).astype(BF16)
            bk_buf[slot, c_len:, :] = (kt[i] * w_row[i]).astype(BF16)
            y0_buf[slot] = y0[i]
            w_buf[slot] = jnp.broadcast_to(w_row[i], (SUBLANES, lanes))
            bonus_buf[rows[i], :] = bonus_old[i] + bsum[i] * v[i]

    def phase_b(p, j):
        units = [(s, d) for s in range(n_sb) for d in range(2)]
        rows = [chunk_rows(s, d, p) for s, d in units]
        slots = [(s * 2 + d) * sc + j for s, d in units]
        s_mat = [s_ref[s * 2 + d] for s, d in units]
        y_old = [ybuf[rw, :] for rw in rows]
        s_bf = [x.astype(BF16) for x in s_mat]
        m_s = [_mm(a2_buf[sl], bk_buf[sl, 0:c_len, :], "tn", P_APPLY) * ones_blk for sl in slots]
        s_add = [_mm(uv_buf[sl], bk_buf[sl], "tn", P_APPLY) * ones_blk for sl in slots]
        y = [_mm(r2_buf[sl], x, "nt", P_APPLY) + y0_buf[sl] for sl, x in zip(slots, s_bf)]
        s_new = [x * w_buf[sl, 0:1, :] + _mm(xb, m, "nn", P_APPLY) + z
                 for x, xb, m, z, sl in zip(s_mat, s_bf, m_s, s_add, slots)]
        for i, (s, d) in enumerate(units):
            ybuf[rows[i], :] = y_old[i] + y[i]
            s_ref[s * 2 + d] = s_new[i]

    def super_body(sp, carry):
        def a_body(j, c2):
            phase_a(sp * sc + j * n_pp, j * n_pp)
            return c2

        def b_body(j, c2):
            phase_b(sp * sc + j, j)
            return c2

        lax.fori_loop(0, sc // n_pp, a_body, 0)
        lax.fori_loop(0, sc, b_body, 0)
        return carry

    lax.fori_loop(0, n_super, super_body, 0)

    n_cc = 4

    def c_body(i, carry):
        rows = [pl.ds(pl.multiple_of((i * n_cc + q) * c_len, c_len), c_len) for q in range(n_cc)]
        y = [ybuf[rw, :] for rw in rows]
        mean = [_mm(x, ones_blk, "nn", P_SUM) * (1.0 / RW_HEAD) for x in y]
        yc = [x - m for x, m in zip(y, mean)]
        var = [_mm(x * x, ones_blk, "nn", P_SUM) * (1.0 / RW_HEAD) for x in yc]
        for rw, x, vr in zip(rows, yc, var):
            yn = x * lax.rsqrt(vr + RW_HEAD * 1e-5) * lnw + lnb
            y_ref[rw, :] = ((yn + bonus_buf[rw, :]) * gate_ref[rw, :]).astype(y_ref.dtype)
        return carry

    lax.fori_loop(0, n_sb * n_chunks // n_cc, c_body, 0)
    rep_t = (_iota((lanes, RW_HEAD), 0) % RW_HEAD == _iota((lanes, RW_HEAD), 1)).astype(F32)
    for s in range(n_sb):
        for d in range(2):
            sfin_ref[s, d] = _mm(s_ref[s * 2 + d], rep_t, "nn", "ra")


def _rwkv_scan(rkv, gate, lw, a, kk, ka, rk, lnw, lnb, s0, row0, n_seq, t_len, n_sb):
    _, m_rows, d = rkv.shape
    n_groups = d // RW_LANES
    blk_rows = n_sb * t_len
    assert row0 % blk_rows == 0 and n_seq % n_sb == 0
    blk0 = row0 // blk_rows
    sc = min(t_len // CHUNK, RW_SUPER_CHUNKS)
    assert (t_len // CHUNK) % sc == 0
    n_slots = n_sb * 2 * sc
    tok = lambda b, g: (blk0 + b, g)
    in_specs = [pl.BlockSpec((None, blk_rows, RW_LANES), lambda b, g, j=j: (j, blk0 + b, g)) for j in range(3)]
    in_specs += [pl.BlockSpec((blk_rows, RW_LANES), tok),
                 pl.BlockSpec((2, blk_rows, RW_LANES), lambda b, g: (0, blk0 + b, g)),
                 pl.BlockSpec((2, blk_rows, RW_LANES), lambda b, g: (0, blk0 + b, g))]
    in_specs += [pl.BlockSpec((1, RW_LANES), lambda b, g: (0, g))] * 5
    args = [rkv, rkv, rkv, gate, lw, a, kk, ka, rk, lnw, lnb]
    if s0 is not None:
        in_specs.append(pl.BlockSpec((n_sb, 2, RW_LANES, RW_HEAD), lambda b, g: (b, 0, g, 0)))
        args.append(s0)
    y, sfin = pl.pallas_call(
        functools.partial(_rwkv_scan_kernel, t_len=t_len, n_sb=n_sb, has_s0=s0 is not None),
        grid=(n_seq // n_sb, n_groups),
        in_specs=in_specs,
        out_specs=[pl.BlockSpec((blk_rows, RW_LANES), lambda b, g: (b, g)),
                   pl.BlockSpec((n_sb, 2, RW_LANES, RW_HEAD), lambda b, g: (b, 0, g, 0))],
        out_shape=[jax.ShapeDtypeStruct((n_seq * t_len, d), BF16),
                   jax.ShapeDtypeStruct((n_seq, 2, d, RW_HEAD), F32)],
        scratch_shapes=[pltpu.VMEM((n_slots, CHUNK, RW_LANES), BF16),
                        pltpu.VMEM((n_slots, CHUNK, RW_LANES), BF16),
                        pltpu.VMEM((n_slots, 2 * CHUNK, RW_LANES), BF16),
                        pltpu.VMEM((n_slots, 2 * CHUNK, RW_LANES), BF16),
                        pltpu.VMEM((n_slots, CHUNK, RW_LANES), F32),
                        pltpu.VMEM((n_slots, SUBLANES, RW_LANES), F32),
                        pltpu.VMEM((blk_rows, RW_LANES), F32),
                        pltpu.VMEM((blk_rows, RW_LANES), F32),
                        pltpu.VMEM((2 * n_sb, RW_LANES, RW_LANES), F32)],
        compiler_params=_cparams("arbitrary", "arbitrary"),
        name=f"rwkv_scan_t{t_len}",
    )(*args)
    return y, sfin


def _out_proj_kernel(grp_ref, y_ref, w_ref, x_ref, mod_ref, o_ref, *, gate_row):
    o_ref[...] = x_ref[...] + mod_ref[0, gate_row:gate_row + 1, :] * _dot(y_ref[...], w_ref[...])


def _out_proj(y, w, x, mod, grp, gate_row, tm):
    m_rows, k_dim = y.shape
    d = w.shape[1]
    tn = 1024
    return pl.pallas_call(
        functools.partial(_out_proj_kernel, gate_row=gate_row),
        grid_spec=pltpu.PrefetchScalarGridSpec(
            num_scalar_prefetch=1, grid=(d // tn, m_rows // tm),
            in_specs=[pl.BlockSpec((tm, k_dim), lambda n, i, g: (i, 0)),
                      pl.BlockSpec((k_dim, tn), lambda n, i, g: (0, n)),
                      pl.BlockSpec((tm, tn), lambda n, i, g: (i, n)),
                      pl.BlockSpec((1, 6, tn), lambda n, i, g: (g[i], 0, n))],
            out_specs=pl.BlockSpec((tm, tn), lambda n, i, g: (i, n))),
        out_shape=jax.ShapeDtypeStruct((m_rows, d), F32),
        compiler_params=_cparams("arbitrary", "arbitrary"),
        name="out_proj",
    )(grp, y, w, x, mod)


def _in_proj_kernel(grp_ref, x_ref, mod_ref, nw_ref, w_ref, o_ref, h_buf):
    @pl.when(pl.program_id(1) == 0)
    def _():
        h_buf[...] = _rms_mod(x_ref[...], nw_ref[...], mod_ref[0, 0:1, :], mod_ref[0, 1:2, :]).astype(BF16)

    o_ref[...] = jnp.dot(h_buf[...], w_ref[...], preferred_element_type=F32)


def _in_proj(x, mod, grp, nw, w, tm, tn):
    m_rows, d = x.shape
    n = w.shape[1]
    return pl.pallas_call(
        _in_proj_kernel,
        grid_spec=pltpu.PrefetchScalarGridSpec(
            num_scalar_prefetch=1, grid=(m_rows // tm, n // tn),
            in_specs=[pl.BlockSpec((tm, d), lambda i, n_, g: (i, 0)),
                      pl.BlockSpec((1, 6, d), lambda i, n_, g: (g[i], 0, 0)),
                      pl.BlockSpec((1, d), lambda i, n_, g: (0, 0)),
                      pl.BlockSpec((d, tn), lambda i, n_, g: (0, n_))],
            out_specs=pl.BlockSpec((tm, tn), lambda i, n_, g: (i, n_)),
            scratch_shapes=[pltpu.VMEM((tm, d), BF16)]),
        out_shape=jax.ShapeDtypeStruct((m_rows, n), F32),
        compiler_params=_cparams("arbitrary", "arbitrary"),
        name="gdn_in_proj",
    )(grp, x, mod, nw, w)


def _ffn_kernel(grp_ref, x_ref, mod_ref, nw_ref, w1_ref, w3_ref, w2_ref, fw_ref, o_ref, h_buf, acc,
                *, final_norm):
    f = pl.program_id(1)

    @pl.when(f == 0)
    def _():
        h_buf[...] = _rms_mod(x_ref[...], nw_ref[...], mod_ref[0, 3:4, :], mod_ref[0, 4:5, :]).astype(BF16)
        acc[...] = jnp.zeros_like(acc)

    h = h_buf[...]
    gate = jnp.dot(h, w1_ref[...], preferred_element_type=F32)
    up = jnp.dot(h, w3_ref[...], preferred_element_type=F32)
    acc[...] += _dot(_silu(gate) * up, w2_ref[...])

    @pl.when(f == pl.num_programs(1) - 1)
    def _():
        y = x_ref[...] + mod_ref[0, 5:6, :] * acc[...]
        if final_norm:
            y = y * lax.rsqrt(jnp.mean(y * y, -1, keepdims=True) + EPS) * fw_ref[...]
        o_ref[...] = y


def _ffn(x, mod, grp, nw, w1, w3, w2, fw, final_norm, tm):
    m_rows, d = x.shape
    d_ff = w1.shape[1]
    tf = 512
    return pl.pallas_call(
        functools.partial(_ffn_kernel, final_norm=final_norm),
        grid_spec=pltpu.PrefetchScalarGridSpec(
            num_scalar_prefetch=1, grid=(m_rows // tm, d_ff // tf),
            in_specs=[pl.BlockSpec((tm, d), lambda i, f, g: (i, 0)),
                      pl.BlockSpec((1, 6, d), lambda i, f, g: (g[i], 0, 0)),
                      pl.BlockSpec((1, d), lambda i, f, g: (0, 0)),
                      pl.BlockSpec((d, tf), lambda i, f, g: (0, f)),
                      pl.BlockSpec((d, tf), lambda i, f, g: (0, f)),
                      pl.BlockSpec((tf, d), lambda i, f, g: (f, 0)),
                      pl.BlockSpec((1, d), lambda i, f, g: (0, 0))],
            out_specs=pl.BlockSpec((tm, d), lambda i, f, g: (i, 0)),
            scratch_shapes=[pltpu.VMEM((tm, d), BF16), pltpu.VMEM((tm, d), F32)]),
        out_shape=jax.ShapeDtypeStruct((m_rows, d), F32),
        compiler_params=_cparams("arbitrary", "arbitrary"),
        name="ffn",
    )(grp, x, mod, nw, w1, w3, w2, fw)


def _gdn_kernel(*refs, t_len, n_sb, has_s0):
    (q_ref, k_ref, v_ref, z_ref, ab_ref, cq_ref, ck_ref, cv_ref, alog_ref, dtb_ref, nw_ref) = refs[:11]
    rest = refs[11:]
    if has_s0:
        s0_ref, rest = rest[0], rest[1:]
    o_ref, sfin_ref, wq_buf, u_buf, attn_buf, ket_buf, gl_buf, obuf, s_ref, ext_buf = rest
    c_len = CHUNK
    n_chunks = t_len // c_len
    n_tot = n_sb * n_chunks
    last_row0 = n_sb * t_len - SUBLANES
    kh = pl.program_id(1)
    n_vh = 32
    tri_t = _iota((c_len, c_len), 0)
    tri_s = _iota((c_len, c_len), 1)
    lane_ab = _iota((c_len, 128), 1)
    row_abt = _iota((128, c_len), 0)
    lane_1 = _iota((1, 128), 1)
    lower = (tri_s <= tri_t).astype(F32)
    inv_masks = _inverse_masks(c_len)

    for s in range(n_sb):
        for j in range(4):
            if has_s0:
                s_ref[s * 4 + j] = s0_ref[s, j // 2, j % 2]
            else:
                s_ref[s * 4 + j] = jnp.zeros((GD_DK, GD_DV), F32)

    def conv_silu(ref, w_ref, cg):
        c = cg % n_chunks
        r0 = pl.multiple_of(cg * c_len, c_len)
        main = ref[pl.ds(r0, c_len), :]
        up0 = pl.multiple_of(jnp.maximum(r0 - SUBLANES, 0), SUBLANES)
        dn0 = pl.multiple_of(jnp.minimum(r0 + c_len, last_row0), SUBLANES)
        up = ref[pl.ds(up0, SUBLANES), :] * jnp.where(c > 0, 1.0, 0.0)
        dn = ref[pl.ds(dn0, SUBLANES), :] * jnp.where(c < n_chunks - 1, 1.0, 0.0)
        wd = main.shape[1]
        ext_buf[0:SUBLANES, 0:wd] = up
        ext_buf[SUBLANES:SUBLANES + c_len, 0:wd] = main
        ext_buf[SUBLANES + c_len:2 * SUBLANES + c_len, 0:wd] = dn
        w = w_ref[...]
        acc = main * w[1:2]
        for j in (0, 2, 3):
            acc = acc + ext_buf[SUBLANES - 1 + j:SUBLANES - 1 + j + c_len, 0:wd] * w[j:j + 1]
        return _silu(acc)

    def l2n(z):
        return z * lax.rsqrt(jnp.sum(z * z, -1, keepdims=True) + EPS)

    incl_m = [tri_s <= tri_t, tri_s >= tri_t]
    strict_m = [tri_s < tri_t, tri_s > tri_t]
    n_cu = 8

    def phase_a(cg0):
        cgs = [cg0 + i for i in range(n_cu)]
        rows = [pl.ds(pl.multiple_of(cg * c_len, c_len), c_len) for cg in cgs]
        q = [l2n(conv_silu(q_ref, cq_ref, cg)) * (GD_DK ** -0.5) for cg in cgs]
        k = [l2n(conv_silu(k_ref, ck_ref, cg)) for cg in cgs]
        v2 = [conv_silu(v_ref, cv_ref, cg) for cg in cgs]
        ab = [ab_ref[rw, :] for rw in rows]
        k_t = [x.T for x in k]
        g_all = [-jnp.exp(alog_ref[...]) * _softplus(x + dtb_ref[...]) for x in ab]
        beta_all = [jax.nn.sigmoid(x) for x in ab]
        prefix = [_mm(lower, g, "nn", P_CUM) for g in g_all]
        g_kk = [_mm(x, y, "nn", P_GRAM) for x, y in zip(k, k_t)]
        g_qk = [_mm(x, y, "nn", P_GRAM) for x, y in zip(q, k_t)]
        gtot_all = [jnp.sum(g, axis=0, keepdims=True) for g in g_all]
        gc_all = [jnp.where(lane_ab < 64, p, t - p + g) for p, t, g in zip(prefix, gtot_all, g_all)]
        gct_all = [x.T for x in gc_all]
        per = []
        for i in range(n_cu):
            for d in range(2):
                for vl in range(2):
                    col_g = d * 64 + 2 * kh + vl
                    col_b = col_g + n_vh
                    gc_col = jnp.sum(jnp.where(lane_ab == col_g, gc_all[i], 0.0), axis=1, keepdims=True)
                    beta = jnp.sum(jnp.where(lane_ab == col_b, beta_all[i], 0.0), axis=1, keepdims=True)
                    gc_row = jnp.sum(jnp.where(row_abt == col_g, gct_all[i], 0.0), axis=0, keepdims=True)
                    g_last = jnp.sum(jnp.where(lane_1 == col_g, gtot_all[i], 0.0), axis=1, keepdims=True)
                    decay = jnp.where(incl_m[d], jnp.exp(jnp.minimum(gc_col - gc_row, 0.0)), 0.0)
                    a_mat = jnp.where(strict_m[d], beta * g_kk[i] * decay, 0.0)
                    per.append((i, 2 * d + vl, gc_col, beta, gc_row, g_last, decay, a_mat))
        sel = [[n for n, p in enumerate(per) if p[1] // 2 == d] for d in range(2)]
        t_d = _unit_tri_inverse(*[jnp.stack([-per[n][7] for n in sl]) for sl in sel], inv_masks)
        t_all = [None] * len(per)
        for d in range(2):
            for m, n in enumerate(sel[d]):
                t_all[n] = t_d[d][m]
        rhs = [jnp.concatenate([v2[i][:, (j % 2) * GD_DV:(j % 2 + 1) * GD_DV] * beta,
                                k[i] * (beta * jnp.exp(gc_col))], axis=1)
               for i, j, gc_col, beta, _, _, _, _ in per]
        uw = [_mm(t_all[n], x, "nn", P_APPLY) for n, x in enumerate(rhs)]
        for n, (i, j, gc_col, beta, gc_row, g_last, decay, _) in enumerate(per):
            idx = cgs[i] * 4 + j
            u_buf[idx] = uw[n][:, :GD_DV]
            wq_buf[idx, 0:c_len, :] = uw[n][:, GD_DV:].astype(BF16)
            wq_buf[idx, c_len:, :] = (q[i] * jnp.exp(gc_col)).astype(BF16)
            attn_buf[idx] = (g_qk[i] * decay).astype(BF16)
            ket_buf[idx] = (k_t[i] * jnp.exp(g_last - gc_row)).astype(BF16)
            gl_buf[idx] = jnp.broadcast_to(jnp.exp(g_last), (SUBLANES, GD_DV))

    def a_body(i, carry):
        phase_a(n_cu * i)
        return carry

    lax.fori_loop(0, n_tot // n_cu, a_body, 0)

    def b_body(i, carry):
        chains = [(s, d, vl) for s in range(n_sb) for d in range(2) for vl in range(2)]
        cg = [s * n_chunks + (i if d == 0 else n_chunks - 1 - i) for s, d, _ in chains]
        idx = [c * 4 + 2 * d + vl for c, (_, d, vl) in zip(cg, chains)]
        s_mat = [s_ref[s * 4 + 2 * d + vl] for s, d, vl in chains]
        ws_qs = [_mm(wq_buf[n], x, "nn", P_APPLY) for n, x in zip(idx, s_mat)]
        v_new = [(u_buf[n] - x[0:c_len]).astype(BF16) for n, x in zip(idx, ws_qs)]
        o = [x[c_len:] + _mm(attn_buf[n], y, "nn", P_APPLY) for n, x, y in zip(idx, ws_qs, v_new)]
        s_new = [x * gl_buf[n, 0:1, :] + _mm(ket_buf[n], y, "nn", P_APPLY) for n, x, y in zip(idx, s_mat, v_new)]
        for n, (s, d, vl) in enumerate(chains):
            rows = pl.ds(pl.multiple_of(cg[n] * c_len, c_len), c_len)
            obuf[d, rows, pl.ds(vl * GD_DV, GD_DV)] = o[n]
            s_ref[s * 4 + 2 * d + vl] = s_new[n]
        return carry

    lax.fori_loop(0, n_chunks, b_body, 0)

    n_cc = 4

    def c_body(i, carry):
        tiles = [(pl.ds(pl.multiple_of((i * n_cc + q) * c_len, c_len), c_len), pl.ds(vl * GD_DV, GD_DV))
                 for q in range(n_cc) for vl in range(2)]
        o = [obuf[0, rw, cl] + obuf[1, rw, cl] for rw, cl in tiles]
        ms = [jnp.mean(x * x, -1, keepdims=True) for x in o]
        gate = [_silu(z_ref[rw, cl]) for rw, cl in tiles]
        for (rw, cl), x, m, g in zip(tiles, o, ms, gate):
            o_ref[rw, cl] = (x * lax.rsqrt(m + EPS) * nw_ref[...] * g).astype(o_ref.dtype)
        return carry

    lax.fori_loop(0, n_tot // n_cc, c_body, 0)
    for s in range(n_sb):
        for j in range(4):
            sfin_ref[s, j // 2, j % 2] = s_ref[s * 4 + j]


def _gdn_scan(proj, ab, conv_w, alog_row, dtb_row, nw, s0, row0, n_seq, t_len, n_sb):
    n_kh = 16
    blk_rows = n_sb * t_len
    assert row0 % blk_rows == 0 and n_seq % n_sb == 0 and (n_sb * t_len // CHUNK) % 2 == 0
    blk0 = row0 // blk_rows
    vw = 2 * GD_DV
    n_units = n_sb * (t_len // CHUNK) * 4
    in_specs = [pl.BlockSpec((blk_rows, GD_DK), lambda b, h: (blk0 + b, h)),
                pl.BlockSpec((blk_rows, GD_DK), lambda b, h: (blk0 + b, n_kh + h)),
                pl.BlockSpec((blk_rows, vw), lambda b, h: (blk0 + b, n_kh + h)),
                pl.BlockSpec((blk_rows, vw), lambda b, h: (blk0 + b, 2 * n_kh + h)),
                pl.BlockSpec((blk_rows, 128), lambda b, h: (blk0 + b, 0)),
                pl.BlockSpec((GD_CONV, GD_DK), lambda b, h: (0, h)),
                pl.BlockSpec((GD_CONV, GD_DK), lambda b, h: (0, n_kh + h)),
                pl.BlockSpec((GD_CONV, vw), lambda b, h: (0, n_kh + h)),
                pl.BlockSpec((1, 128), lambda b, h: (0, 0)),
                pl.BlockSpec((1, 128), lambda b, h: (0, 0)),
                pl.BlockSpec((1, GD_DV), lambda b, h: (0, 0))]
    args = [proj, proj, proj, proj, ab, conv_w, conv_w, conv_w, alog_row, dtb_row, nw]
    if s0 is not None:
        in_specs.append(pl.BlockSpec((n_sb, 2, 2, GD_DK, GD_DV), lambda b, h: (b, 0, h, 0, 0)))
        args.append(s0)
    o, sfin = pl.pallas_call(
        functools.partial(_gdn_kernel, t_len=t_len, n_sb=n_sb, has_s0=s0 is not None),
        grid=(n_seq // n_sb, n_kh),
        in_specs=in_specs,
        out_specs=[pl.BlockSpec((blk_rows, vw), lambda b, h: (b, h)),
                   pl.BlockSpec((n_sb, 2, 2, GD_DK, GD_DV), lambda b, h: (b, 0, h, 0, 0))],
        out_shape=[jax.ShapeDtypeStruct((n_seq * t_len, 2 * n_kh * GD_DV), BF16),
                   jax.ShapeDtypeStruct((n_seq, 2, 2 * n_kh, GD_DK, GD_DV), F32)],
        scratch_shapes=[pltpu.VMEM((n_units, 2 * CHUNK, GD_DV), BF16),
                        pltpu.VMEM((n_units, CHUNK, GD_DV), F32),
                        pltpu.VMEM((n_units, CHUNK, CHUNK), BF16),
                        pltpu.VMEM((n_units, GD_DK, CHUNK), BF16),
                        pltpu.VMEM((n_units, SUBLANES, GD_DV), F32),
                        pltpu.VMEM((2, blk_rows, vw), F32),
                        pltpu.VMEM((4 * n_sb, GD_DK, GD_DV), F32),
                        pltpu.VMEM((CHUNK + 2 * SUBLANES, vw), F32)],
        compiler_params=_cparams("arbitrary", "arbitrary"),
        name=f"gdn_scan_t{t_len}",
    )(*args)
    return o, sfin


def _grid_pos_embed(n_tokens, d_model):
    rows = n_tokens // GRID_W
    f32 = np.float32
    row = np.broadcast_to(np.arange(rows, dtype=f32)[:, None], (rows, GRID_W)).reshape(-1)
    col = np.broadcast_to(np.arange(GRID_W, dtype=f32)[None, :], (rows, GRID_W)).reshape(-1)
    quarter = d_model // 4
    omega = (f32(1.0) / (f32(POS_BASE) ** (np.arange(quarter, dtype=f32) / f32(quarter)))).astype(f32)
    ar = (row[:, None] * omega).astype(f32)
    ac = (col[:, None] * omega).astype(f32)
    return jnp.asarray(np.concatenate([np.sin(ar), np.cos(ar), np.sin(ac), np.cos(ac)], -1).astype(f32))


def _pad_axis(a, axis, size):
    pad = [(0, 0)] * a.ndim
    pad[axis] = (0, size - a.shape[axis])
    return jnp.pad(a, pad)


def kernel(x_prompt, x_sample, state_rwkv, state_gdn, c, c_ctx, norm_mix, norm_ffn, norm_final, w_mod, b_mod, ffn_w1, ffn_w3, ffn_w2, rw_mu, rw_wr, rw_wk, rw_wv, rw_wo, rw_w0, rw_w1, rw_w2, rw_a0, rw_a1, rw_a2, rw_g1, rw_g2, rw_kk, rw_ka, rw_rk, rw_ln_w, rw_ln_b, gd_w_in, gd_conv, gd_a_log, gd_dt_bias, gd_norm, gd_w_out):
    n_p, t_p, d = x_prompt.shape
    n_s, t_s, _ = x_sample.shape
    bf = lambda a: a.astype(BF16)
    row = lambda a: a.reshape(1, -1)
    tm_shift = 256
    tm_big = 512
    tm_wide = 1024

    regions = [
        dict(x=x_prompt.reshape(n_p * t_p, d), n_seq=n_p, t_len=t_p, groups=[0] * n_p, n_sb=4, n_sb_gd=8,
             s0_rw=None, s0_gd=None),
        dict(x=(x_sample + _grid_pos_embed(t_s, d)[None]).reshape(n_s * t_s, d), n_seq=n_s, t_len=t_s,
             groups=list(range(1, n_s + 1)), n_sb=1, n_sb_gd=1,
             s0_rw=state_rwkv[:, 0].reshape(n_s, 2, d, RW_HEAD), s0_gd=state_gdn[:, 0]),
    ]
    for reg in regions:
        layout = [(reg["n_seq"], reg["t_len"], reg["groups"])]
        reg["tables"] = _tile_tables(layout, tm_shift)
        reg["grp"] = _tile_tables(layout, tm_big)[0]
        reg["grp_wide"] = _tile_tables(layout, tm_wide)[0]

    cond = _pad_axis(jnp.concatenate([c_ctx[None], c], axis=0), 0, SUBLANES)
    mod = _modulation(cond, w_mod, b_mod).reshape(w_mod.shape[0], SUBLANES, 6, d)

    w_rkv = bf(jnp.stack([rw_wr[0], rw_wk[0], rw_wv[0]]))
    lp = RW_LORA_PAD
    lora_w = (bf(rw_g1[0]), bf(rw_g2[0]),
              bf(_pad_axis(rw_w1[0], 2, lp)), bf(_pad_axis(rw_w2[0], 1, lp)),
              bf(_pad_axis(rw_a1[0], 2, lp)), bf(_pad_axis(rw_a2[0], 1, lp)),
              rw_w0[0][:, None, :], rw_a0[0][:, None, :])
    scan_p = (row(rw_kk[0]), row(rw_ka[0]), row(rw_rk[0]), row(rw_ln_w[0]), row(rw_ln_b[0]))
    w_o, ffn0 = _layer_bf16(rw_wo, 0), (_layer_bf16(ffn_w1, 0), _layer_bf16(ffn_w3, 0), _layer_bf16(ffn_w2, 0))
    for reg in regions:
        rkv, gate, lw, a = _rwkv_inputs(reg["x"], mod[0], reg["tables"], row(norm_mix[0]), rw_mu[0], w_rkv,
                                        *lora_w, tm_shift)
        y, reg["s_rwkv"] = _rwkv_scan(rkv, gate, lw, a, *scan_p, reg["s0_rw"], 0, reg["n_seq"], reg["t_len"],
                                      reg["n_sb"])
        x = _out_proj(y, w_o, reg["x"], mod[0], reg["grp"], 2, tm_big)
        reg["x"] = _ffn(x, mod[0], reg["grp"], row(norm_ffn[0]), *ffn0, row(norm_final), False, tm_big)

    n_main = 3 * 4096
    w_in_main = _layer_bf16(gd_w_in, 0, 0, n_main)
    w_in_ab = _layer_bf16(gd_w_in, 0, n_main, gd_w_in.shape[2] - n_main)
    zeros32 = jnp.zeros((2, 32), F32)
    alog_row = jnp.stack([gd_a_log[0], zeros32], axis=1).reshape(1, 128)
    dtb_row = jnp.stack([gd_dt_bias[0], zeros32], axis=1).reshape(1, 128)
    w_out, ffn1 = _layer_bf16(gd_w_out, 0), (_layer_bf16(ffn_w1, 1), _layer_bf16(ffn_w3, 1), _layer_bf16(ffn_w2, 1))
    for reg in regions:
        proj = _in_proj(reg["x"], mod[1], reg["grp_wide"], row(norm_mix[1]), w_in_main, tm_wide, 1024)
        ab = _in_proj(reg["x"], mod[1], reg["grp_wide"], row(norm_mix[1]), w_in_ab, tm_wide, 128)
        o, reg["s_gdn"] = _gdn_scan(proj, ab, gd_conv[0], alog_row, dtb_row, row(gd_norm[0]), reg["s0_gd"], 0,
                                    reg["n_seq"], reg["t_len"], reg["n_sb_gd"])
        x = _out_proj(o, w_out, reg["x"], mod[1], reg["grp"], 2, tm_big)
        reg["x"] = _ffn(x, mod[1], reg["grp"], row(norm_ffn[1]), *ffn1, row(norm_final), True, tm_big)

    y_prompt = regions[0]["x"].reshape(n_p, t_p, d)
    y_sample = regions[1]["x"].reshape(n_s, t_s, d)
    new_state_rwkv = regions[0]["s_rwkv"].reshape(n_p, 1, 2, d // RW_HEAD, RW_HEAD, RW_HEAD)
    new_state_gdn = regions[0]["s_gdn"].reshape(n_p, 1, 2, 32, GD_DK, GD_DV)
    return (y_prompt, y_sample, new_state_rwkv, new_state_gdn)
```

```python
import functools

import numpy as np
import jax
import jax.numpy as jnp
from jax import lax
from jax.experimental import pallas as pl
from jax.experimental.pallas import tpu as pltpu

F32 = jnp.float32
BF16 = jnp.bfloat16

EPS = 1e-6
GRID_W = 64
POS_BASE = 10000.0
RW_HEAD = 64
RW_GROUP_HEADS = 4
RW_LANES = RW_HEAD * RW_GROUP_HEADS
RW_LORA_PAD = 128
GD_DK = 128
GD_DV = 128
GD_CONV = 4
CHUNK = 64
RW_SUPER_CHUNKS = 8
SUBLANES = 8
VMEM_LIMIT = 56 * 1024 * 1024
CAST_BLOCK_BYTES = 8 * 1024 * 1024


def _cparams(*sem):
    return pltpu.CompilerParams(dimension_semantics=sem, vmem_limit_bytes=VMEM_LIMIT)


def _dot(a, b):
    return jnp.dot(a.astype(BF16), b.astype(BF16), preferred_element_type=F32)


_DIMS = {"nn": (((1,), (0,)), ((), ())),
         "nt": (((1,), (1,)), ((), ())),
         "tn": (((0,), (0,)), ((), ())),
         "bnn": (((2,), (1,)), ((0,), (0,)))}


def _split_bf16(x, pieces):
    out = []
    for _ in range(pieces - 1):
        p = x.astype(BF16)
        out.append(p)
        x = x - p.astype(F32)
    out.append(x.astype(BF16))
    return out


def _mm(a, b, dims="nn", mode="b"):
    dn = _DIMS[dims]
    dg = lambda x, y: lax.dot_general(x, y, dn, preferred_element_type=F32)
    if mode == "b":
        return dg(a.astype(BF16), b.astype(BF16))
    if mode in ("la", "la2"):
        a0 = a.astype(BF16)
        return sum(dg(a0, p) for p in _split_bf16(b, 3 if mode == "la" else 2))
    if mode in ("ra", "ra2"):
        b0 = b.astype(BF16)
        return sum(dg(p, b0) for p in _split_bf16(a, 3 if mode == "ra" else 2))
    raise ValueError(mode)


P_INV = "b"
P_GRAM = "b"
P_APPLY = "b"
P_SUM = "ra2"
P_CUM = "la2"


def _iota(shape, dim):
    return lax.broadcasted_iota(jnp.int32, shape, dim)


def _silu(x):
    return x * jax.nn.sigmoid(x)


def _softplus(x):
    return jnp.maximum(x, 0.0) + jnp.log(1.0 + jnp.exp(-jnp.abs(x)))


def _rms_mod(x, nw, sh, sc):
    y = x * lax.rsqrt(jnp.mean(x * x, -1, keepdims=True) + EPS)
    return (y * nw) * (1.0 + sc) + sh


def _inverse_masks(c_len):
    ij = _iota((c_len, c_len), 0) ^ _iota((c_len, c_len), 1)
    levels = []
    half = 2
    while half < c_len:
        levels.append(((ij >= half) & (ij < 2 * half)).astype(BF16))
        half *= 2
    return (ij == 0).astype(F32), (ij == 1).astype(F32), levels


def _unit_tri_inverse(n_lower, n_upper, masks):
    eye, pair, levels = masks
    mats = [n_lower, n_upper]
    c_len = n_lower.shape[-1]
    d_mats = [eye + m * pair for m in mats]
    n_bf = [m.astype(BF16) for m in mats]
    half = 2
    for lvl in levels:
        l_mats = [m * lvl for m in n_bf]
        if half < SUBLANES:
            dl = [_mm(d, l, "bnn", P_INV) for d, l in zip(d_mats, l_mats)]
            d_mats = [d + _mm(x, d, "bnn", P_INV) for d, x in zip(d_mats, dl)]
        else:
            first = [slice(b0, b0 + half) for b0 in range(0, c_len, 2 * half)]
            second = [slice(b0 + half, b0 + 2 * half) for b0 in range(0, c_len, 2 * half)]
            act = [second, first]
            d_act = [jnp.concatenate([d[:, sl, :] for sl in rows], axis=1) for d, rows in zip(d_mats, act)]
            dl = [_mm(x, l, "bnn", P_INV) for x, l in zip(d_act, l_mats)]
            upd = [x + _mm(y, d, "bnn", P_INV) for x, y, d in zip(d_act, dl, d_mats)]
            out = []
            for g, (d, u) in enumerate(zip(d_mats, upd)):
                parts = []
                for n in range(len(first)):
                    new = u[:, n * half:(n + 1) * half, :]
                    parts += [d[:, first[n], :], new] if g == 0 else [new, d[:, second[n], :]]
                out.append(jnp.concatenate(parts, axis=1))
            d_mats = out
        half *= 2
    return d_mats


def _cast_kernel(w_ref, o_ref):
    o_ref[...] = w_ref[...].astype(BF16)


def _layer_bf16(w, layer, col0=0, n_cols=None):
    _, k_dim, n = w.shape
    n_cols = n if n_cols is None else n_cols
    assert col0 % n_cols == 0
    tr = 128
    while tr * 2 * n_cols * 4 <= CAST_BLOCK_BYTES and k_dim % (tr * 2) == 0:
        tr *= 2
    return pl.pallas_call(
        _cast_kernel,
        grid=(k_dim // tr,),
        in_specs=[pl.BlockSpec((None, tr, n_cols), lambda i: (layer, i, col0 // n_cols))],
        out_specs=pl.BlockSpec((tr, n_cols), lambda i: (i, 0)),
        out_shape=jax.ShapeDtypeStruct((k_dim, n_cols), BF16),
        compiler_params=_cparams("arbitrary"),
        name="weight_to_bf16",
    )(w)


def _mod_kernel(c_ref, w_ref, b_ref, o_ref):
    o_ref[0] = _dot(_silu(c_ref[...]), w_ref[0]) + b_ref[0]


def _modulation(cond, w_mod, b_mod):
    depth, d, n = w_mod.shape
    tn = 1024
    return pl.pallas_call(
        _mod_kernel,
        grid=(depth, n // tn),
        in_specs=[pl.BlockSpec((SUBLANES, d), lambda l, j: (0, 0)),
                  pl.BlockSpec((1, d, tn), lambda l, j: (l, 0, j)),
                  pl.BlockSpec((1, 1, tn), lambda l, j: (l, 0, j))],
        out_specs=pl.BlockSpec((1, SUBLANES, tn), lambda l, j: (l, 0, j)),
        out_shape=jax.ShapeDtypeStruct((depth, SUBLANES, n), F32),
        compiler_params=_cparams("arbitrary", "arbitrary"),
        name="modulation",
    )(cond, w_mod, b_mod.reshape(depth, 1, n))


def _shift_mix(x_ref, xp_ref, xn_ref, mod_ref, nw_ref, has_prev, has_next):
    nw = nw_ref[...]
    sh = mod_ref[0, 0:1, :]
    sc = mod_ref[0, 1:2, :]
    h = _rms_mod(x_ref[...], nw, sh, sc)
    tm = h.shape[0]
    hp = _rms_mod(xp_ref[SUBLANES - 1:SUBLANES, :], nw, sh, sc) * has_prev
    hn = _rms_mod(xn_ref[0:1, :], nw, sh, sc) * has_next
    rows = _iota(h.shape, 0)
    prev = jnp.where(rows == 0, hp, pltpu.roll(h, 1, 0))
    nxt = jnp.where(rows == tm - 1, hn, pltpu.roll(h, tm - 1, 0))
    return h, 0.5 * (prev + nxt) - h


def _rkv_kernel(grp_ref, hp_ref, hn_ref, x_ref, xp_ref, xn_ref, mod_ref, nw_ref, mu_ref, w_ref, o_ref):
    i = pl.program_id(1)
    h, xx = _shift_mix(x_ref, xp_ref, xn_ref, mod_ref, nw_ref,
                       hp_ref[i].astype(F32), hn_ref[i].astype(F32))
    o_ref[0] = _dot(h + xx * mu_ref[0], w_ref[0])


def _lora_kernel(grp_ref, hp_ref, hn_ref, x_ref, xp_ref, xn_ref, mod_ref, nw_ref, mu_ref,
                 g1_ref, g2_ref, w1_ref, w2_ref, a1_ref, a2_ref, w0_ref, a0_ref,
                 gate_ref, lw_ref, a_ref):
    i = pl.program_id(0)
    h, xx = _shift_mix(x_ref, xp_ref, xn_ref, mod_ref, nw_ref,
                       hp_ref[i].astype(F32), hn_ref[i].astype(F32))
    xw = h + xx * mu_ref[1:2, :]
    xa = h + xx * mu_ref[4:5, :]
    xg = h + xx * mu_ref[5:6, :]
    gate_ref[...] = _dot(jax.nn.sigmoid(_dot(xg, g1_ref[...])), g2_ref[...])
    for d in range(2):
        lw = jnp.tanh(_dot(xw, w1_ref[d]))
        w_log = -_softplus(-(w0_ref[d] + _dot(lw, w2_ref[d]))) - 0.5
        lw_ref[d] = -jnp.exp(w_log)
        a_ref[d] = jax.nn.sigmoid(a0_ref[d] + _dot(_dot(xa, a1_ref[d]), a2_ref[d]))


def _tile_tables(regions, tm):
    row_grp, row_pos, row_len = [], [], []
    for n_seq, t_len, groups in regions:
        for s in range(n_seq):
            row_grp += [groups[s]] * t_len
            row_pos += list(range(t_len))
            row_len += [t_len] * t_len
    row_grp, row_pos, row_len = (np.asarray(v).reshape(-1, tm) for v in (row_grp, row_pos, row_len))
    assert (row_grp == row_grp[:, :1]).all(), "a row tile must not straddle modulation groups"
    as_i32 = lambda v: jnp.asarray(np.asarray(v, np.int32))
    return (as_i32(row_grp[:, 0]), as_i32(row_pos[:, 0] > 0),
            as_i32(row_pos[:, -1] < row_len[:, -1] - 1))


def _halo_specs(tm, d, m_rows, n_lead):
    blocks = tm // SUBLANES
    last = m_rows // SUBLANES - 1
    if n_lead == 1:
        cur = lambda j, i, *_: (i, 0)
        prev = lambda j, i, *_: (jnp.maximum(i * blocks - 1, 0), 0)
        nxt = lambda j, i, *_: (jnp.minimum((i + 1) * blocks, last), 0)
    else:
        cur = lambda i, *_: (i, 0)
        prev = lambda i, *_: (jnp.maximum(i * blocks - 1, 0), 0)
        nxt = lambda i, *_: (jnp.minimum((i + 1) * blocks, last), 0)
    return [pl.BlockSpec((tm, d), cur), pl.BlockSpec((SUBLANES, d), prev), pl.BlockSpec((SUBLANES, d), nxt)]


def _rwkv_inputs(x, mod, tables, nw, mu, w_rkv, g1, g2, w1, w2, a1, a2, w0, a0, tm):
    m_rows, d = x.shape
    n_tiles = m_rows // tm
    mu_rkv = jnp.stack([mu[0], mu[2], mu[3]])[:, None, :]
    rkv = pl.pallas_call(
        _rkv_kernel,
        grid_spec=pltpu.PrefetchScalarGridSpec(
            num_scalar_prefetch=3, grid=(3, n_tiles),
            in_specs=_halo_specs(tm, d, m_rows, 1) + [
                pl.BlockSpec((1, 6, d), lambda j, i, g, *_: (g[i], 0, 0)),
                pl.BlockSpec((1, d), lambda j, i, *_: (0, 0)),
                pl.BlockSpec((1, 1, d), lambda j, i, *_: (j, 0, 0)),
                pl.BlockSpec((1, d, d), lambda j, i, *_: (j, 0, 0))],
            out_specs=pl.BlockSpec((1, tm, d), lambda j, i, *_: (j, i, 0))),
        out_shape=jax.ShapeDtypeStruct((3, m_rows, d), F32),
        compiler_params=_cparams("arbitrary", "arbitrary"),
        name="rwkv_rkv_proj",
    )(*tables, x, x, x, mod, nw, mu_rkv, w_rkv)

    full = lambda a: pl.BlockSpec(a.shape, lambda i, *_: (0,) * a.ndim)
    small = [g1, g2, w1, w2, a1, a2, w0, a0]
    gate, lw, a = pl.pallas_call(
        _lora_kernel,
        grid_spec=pltpu.PrefetchScalarGridSpec(
            num_scalar_prefetch=3, grid=(n_tiles,),
            in_specs=_halo_specs(tm, d, m_rows, 0) + [
                pl.BlockSpec((1, 6, d), lambda i, g, *_: (g[i], 0, 0)),
                pl.BlockSpec((1, d), lambda i, *_: (0, 0)),
                full(mu)] + [full(s) for s in small],
            out_specs=[pl.BlockSpec((tm, d), lambda i, *_: (i, 0)),
                       pl.BlockSpec((2, tm, d), lambda i, *_: (0, i, 0)),
                       pl.BlockSpec((2, tm, d), lambda i, *_: (0, i, 0))]),
        out_shape=[jax.ShapeDtypeStruct((m_rows, d), F32),
                   jax.ShapeDtypeStruct((2, m_rows, d), F32),
                   jax.ShapeDtypeStruct((2, m_rows, d), F32)],
        compiler_params=_cparams("arbitrary"),
        name="rwkv_lora_proj",
    )(*tables, x, x, x, mod, nw, mu, *small)
    return rkv, gate, lw, a


def _rwkv_scan_kernel(*refs, t_len, n_sb, has_s0):
    (r_ref, k_ref, v_ref, gate_ref, lw_ref, a_ref, kk_ref, ka_ref, rk_ref, lnw_ref, lnb_ref) = refs[:11]
    rest = refs[11:]
    if has_s0:
        s0_ref, rest = rest[0], rest[1:]
    (y_ref, sfin_ref, r2_buf, a2_buf, uv_buf, bk_buf, y0_buf, w_buf, ybuf, bonus_buf, s_ref) = rest
    c_len, lanes, heads = CHUNK, RW_LANES, RW_GROUP_HEADS
    n_chunks = t_len // c_len
    sc = min(n_chunks, RW_SUPER_CHUNKS)
    n_super = n_chunks // sc
    n_pp = max(1, 8 // (2 * n_sb))
    stack = heads * c_len

    same_head = (_iota((stack, lanes), 0) // c_len == _iota((stack, lanes), 1) // RW_HEAD).astype(F32)
    ones_blk = same_head
    rep = (_iota((RW_HEAD, lanes), 1) % RW_HEAD == _iota((RW_HEAD, lanes), 0)).astype(F32)
    t_idx = _iota((stack, c_len), 0) % c_len
    s_idx = _iota((stack, c_len), 1)
    tri_t = _iota((c_len, c_len), 0)
    tri_s = _iota((c_len, c_len), 1)

    inv_masks = _inverse_masks(c_len)
    kk_p, ka_p, rk_p = kk_ref[...], ka_ref[...], rk_ref[...]
    lnw, lnb = lnw_ref[...], lnb_ref[...]

    def fold(z):
        z = z * same_head
        return z[0:c_len] + z[c_len:2 * c_len] + z[2 * c_len:3 * c_len] + z[3 * c_len:4 * c_len]

    def tile4(z):
        return jnp.concatenate([z, z, z, z], axis=0) * same_head

    for s in range(n_sb):
        for d in range(2):
            if has_s0:
                s_ref[s * 2 + d] = _mm(s0_ref[s, d], rep, "nn", "ra") * ones_blk
            else:
                s_ref[s * 2 + d] = jnp.zeros((lanes, lanes), F32)
    ybuf[...] = jnp.zeros_like(ybuf)
    bonus_buf[...] = jnp.zeros_like(bonus_buf)

    def chunk_rows(s, d, p):
        c = p if d == 0 else n_chunks - 1 - p
        return pl.ds(pl.multiple_of((s * n_chunks + c) * c_len, c_len), c_len)

    incl_f = [(tri_s <= tri_t).astype(F32), (tri_s >= tri_t).astype(F32)]
    strict4 = [s_idx < t_idx, s_idx > t_idx]
    t_idx2 = _iota((stack, 2 * c_len), 0) % c_len
    s_idx2 = _iota((stack, 2 * c_len), 1) % c_len
    right = _iota((stack, 2 * c_len), 1) >= c_len
    strict_right = [right & (s_idx2 < t_idx2), right & (s_idx2 > t_idx2)]
    incl_both = [s_idx2 <= t_idx2, s_idx2 >= t_idx2]

    def phase_a(p0, j0):
        units = [(s, d, q) for s in range(n_sb) for d in range(2) for q in range(n_pp)]
        dirs = [d for _, d, _ in units]
        rows = [chunk_rows(s, d, p0 + q) for s, d, q in units]
        r = [r_ref[rw, :] for rw in rows]
        k = [k_ref[rw, :] for rw in rows]
        v = [v_ref[rw, :] for rw in rows]
        lw = [lw_ref[d, rw, :] for d, rw in zip(dirs, rows)]
        a = [a_ref[d, rw, :] for d, rw in zip(dirs, rows)]
        bonus_old = [bonus_buf[rw, :] for rw in rows]
        kx = [ki * kk_p for ki in k]
        n_u = len(units)
        rows_of = lambda z: [z[i * c_len:(i + 1) * c_len] for i in range(n_u)]
        ss = rows_of(_mm(jnp.concatenate([x * x for x in kx], axis=0), ones_blk, "nn", P_SUM))
        cw = [_mm(incl_f[d], x, "nn", P_CUM) for d, x in zip(dirs, lw)]
        bsum = rows_of(_mm(jnp.concatenate([ri * ki * (1.0 + (ai - 1.0) * ka_p) * rk_p
                                            for ri, ki, ai in zip(r, k, a)], axis=0), ones_blk, "nn", P_SUM))
        kkn = [x * lax.rsqrt(y + EPS) for x, y in zip(kx, ss)]
        kd = [ki * (1.0 + (ai - 1.0) * ka_p) for ki, ai in zip(k, a)]
        w_row = [jnp.exp(jnp.sum(x, axis=0, keepdims=True)) for x in lw]
        e_neg = [jnp.exp(-x) for x in cw]
        at = [-kn * jnp.exp(c - l) for kn, c, l in zip(kkn, cw, lw)]
        bt = [kn * ai * e for kn, ai, e in zip(kkn, a, e_neg)]
        kt = [x * e for x, e in zip(kd, e_neg)]
        rt = [ri * jnp.exp(c) for ri, c in zip(r, cw)]
        lhs = [jnp.concatenate([tile4(x), tile4(y)], axis=0) for x, y in zip(at, rt)]
        rhs = [jnp.concatenate([x, y], axis=0) for x, y in zip(bt, kt)]
        gram = [_mm(x, y, "nt", P_GRAM) for x, y in zip(lhs, rhs)]
        n_s = [jnp.where(strict4[d], g[0:stack, 0:c_len], 0.0) for d, g in zip(dirs, gram)]
        ak_m = [jnp.where(strict_right[d], g[0:stack], 0.0) for d, g in zip(dirs, gram)]
        r_m = [jnp.where(incl_both[d], g[stack:], 0.0) for d, g in zip(dirs, gram)]
        sel = [[i for i, di in enumerate(dirs) if di == d] for d in range(2)]
        t_d = _unit_tri_inverse(*[jnp.concatenate([n_s[i].reshape(heads, c_len, c_len) for i in sl], axis=0)
                                  for sl in sel], inv_masks)
        t_s = [None] * len(units)
        for d in range(2):
            for n, i in enumerate(sel[d]):
                t_s[i] = t_d[d][n * heads:(n + 1) * heads].reshape(stack, c_len)
        a2 = [fold(_mm(t, x, "nn", P_APPLY)) for t, x in zip(t_s, at)]
        akv = [fold(_mm(x, jnp.concatenate([vi, vi], axis=0), "nn", P_APPLY)) for x, vi in zip(ak_m, v)]
        u0 = [fold(_mm(t, x, "nn", P_APPLY)) for t, x in zip(t_s, akv)]
        r2 = [x + fold(_mm(rm[:, 0:c_len], y, "nn", P_APPLY)) for x, rm, y in zip(rt, r_m, a2)]
        y0 = [fold(_mm(rm, jnp.concatenate([x, vi], axis=0), "nn", P_APPLY)) for rm, x, vi in zip(r_m, u0, v)]
        for i, (s, d, q) in enumerate(units):
            slot = (s * 2 + d) * sc + j0 + q
            r2_buf[slot] = r2[i].astype(BF16)
            a2_buf[slot] = a2[i].astype(BF16)
            uv_buf[slot, 0:c_len, :] = u0[i].astype(BF16)
            uv_buf[slot, c_len:, :] = v[i].astype(BF16)
            bk_buf[slot, 0:c_len, :] = (bt[i] * w_row[i]).astype(BF16)
            bk_buf[slot, c_len:, :] = (kt[i] * w_row[i]).astype(BF16)
            y0_buf[slot] = y0[i]
            w_buf[slot] = jnp.broadcast_to(w_row[i], (SUBLANES, lanes))
            bonus_buf[rows[i], :] = bonus_old[i] + bsum[i] * v[i]

    def phase_b(p, j):
        units = [(s, d) for s in range(n_sb) for d in range(2)]
        rows = [chunk_rows(s, d, p) for s, d in units]
        slots = [(s * 2 + d) * sc + j for s, d in units]
        s_mat = [s_ref[s * 2 + d] for s, d in units]
        y_old = [ybuf[rw, :] for rw in rows]
        s_bf = [x.astype(BF16) for x in s_mat]
        m_s = [_mm(a2_buf[sl], bk_buf[sl, 0:c_len, :], "tn", P_APPLY) * ones_blk for sl in slots]
        s_add = [_mm(uv_buf[sl], bk_buf[sl], "tn", P_APPLY) * ones_blk for sl in slots]
        y = [_mm(r2_buf[sl], x, "nt", P_APPLY) + y0_buf[sl] for sl, x in zip(slots, s_bf)]
        s_new = [x * w_buf[sl, 0:1, :] + _mm(xb, m, "nn", P_APPLY) + z
                 for x, xb, m, z, sl in zip(s_mat, s_bf, m_s, s_add, slots)]
        for i, (s, d) in enumerate(units):
            ybuf[rows[i], :] = y_old[i] + y[i]
            s_ref[s * 2 + d] = s_new[i]

    def super_body(sp, carry):
        def a_body(j, c2):
            phase_a(sp * sc + j * n_pp, j * n_pp)
            return c2

        def b_body(j, c2):
            phase_b(sp * sc + j, j)
            return c2

        lax.fori_loop(0, sc // n_pp, a_body, 0)
        lax.fori_loop(0, sc, b_body, 0)
        return carry

    lax.fori_loop(0, n_super, super_body, 0)

    n_cc = 4

    def c_body(i, carry):
        rows = [pl.ds(pl.multiple_of((i * n_cc + q) * c_len, c_len), c_len) for q in range(n_cc)]
        y = [ybuf[rw, :] for rw in rows]
        mean = [_mm(x, ones_blk, "nn", P_SUM) * (1.0 / RW_HEAD) for x in y]
        yc = [x - m for x, m in zip(y, mean)]
        var = [_mm(x * x, ones_blk, "nn", P_SUM) * (1.0 / RW_HEAD) for x in yc]
        for rw, x, vr in zip(rows, yc, var):
            yn = x * lax.rsqrt(vr + RW_HEAD * 1e-5) * lnw + lnb
            y_ref[rw, :] = ((yn + bonus_buf[rw, :]) * gate_ref[rw, :]).astype(y_ref.dtype)
        return carry

    lax.fori_loop(0, n_sb * n_chunks // n_cc, c_body, 0)
    rep_t = (_iota((lanes, RW_HEAD), 0) % RW_HEAD == _iota((lanes, RW_HEAD), 1)).astype(F32)
    for s in range(n_sb):
        for d in range(2):
            sfin_ref[s, d] = _mm(s_ref[s * 2 + d], rep_t, "nn", "ra")


def _rwkv_scan(rkv, gate, lw, a, kk, ka, rk, lnw, lnb, s0, row0, n_seq, t_len, n_sb):
    _, m_rows, d = rkv.shape
    n_groups = d // RW_LANES
    blk_rows = n_sb * t_len
    assert row0 % blk_rows == 0 and n_seq % n_sb == 0
    blk0 = row0 // blk_rows
    sc = min(t_len // CHUNK, RW_SUPER_CHUNKS)
    assert (t_len // CHUNK) % sc == 0
    n_slots = n_sb * 2 * sc
    tok = lambda b, g: (blk0 + b, g)
    in_specs = [pl.BlockSpec((None, blk_rows, RW_LANES), lambda b, g, j=j: (j, blk0 + b, g)) for j in range(3)]
    in_specs += [pl.BlockSpec((blk_rows, RW_LANES), tok),
                 pl.BlockSpec((2, blk_rows, RW_LANES), lambda b, g: (0, blk0 + b, g)),
                 pl.BlockSpec((2, blk_rows, RW_LANES), lambda b, g: (0, blk0 + b, g))]
    in_specs += [pl.BlockSpec((1, RW_LANES), lambda b, g: (0, g))] * 5
    args = [rkv, rkv, rkv, gate, lw, a, kk, ka, rk, lnw, lnb]
    if s0 is not None:
        in_specs.append(pl.BlockSpec((n_sb, 2, RW_LANES, RW_HEAD), lambda b, g: (b, 0, g, 0)))
        args.append(s0)
    y, sfin = pl.pallas_call(
        functools.partial(_rwkv_scan_kernel, t_len=t_len, n_sb=n_sb, has_s0=s0 is not None),
        grid=(n_seq // n_sb, n_groups),
        in_specs=in_specs,
        out_specs=[pl.BlockSpec((blk_rows, RW_LANES), lambda b, g: (b, g)),
                   pl.BlockSpec((n_sb, 2, RW_LANES, RW_HEAD), lambda b, g: (b, 0, g, 0))],
        out_shape=[jax.ShapeDtypeStruct((n_seq * t_len, d), BF16),
                   jax.ShapeDtypeStruct((n_seq, 2, d, RW_HEAD), F32)],
        scratch_shapes=[pltpu.VMEM((n_slots, CHUNK, RW_LANES), BF16),
                        pltpu.VMEM((n_slots, CHUNK, RW_LANES), BF16),
                        pltpu.VMEM((n_slots, 2 * CHUNK, RW_LANES), BF16),
                        pltpu.VMEM((n_slots, 2 * CHUNK, RW_LANES), BF16),
                        pltpu.VMEM((n_slots, CHUNK, RW_LANES), F32),
                        pltpu.VMEM((n_slots, SUBLANES, RW_LANES), F32),
                        pltpu.VMEM((blk_rows, RW_LANES), F32),
                        pltpu.VMEM((blk_rows, RW_LANES), F32),
                        pltpu.VMEM((2 * n_sb, RW_LANES, RW_LANES), F32)],
        compiler_params=_cparams("arbitrary", "arbitrary"),
        name=f"rwkv_scan_t{t_len}",
    )(*args)
    return y, sfin


def _out_proj_kernel(grp_ref, y_ref, w_ref, x_ref, mod_ref, o_ref, *, gate_row):
    o_ref[...] = x_ref[...] + mod_ref[0, gate_row:gate_row + 1, :] * _dot(y_ref[...], w_ref[...])


def _out_proj(y, w, x, mod, grp, gate_row, tm):
    m_rows, k_dim = y.shape
    d = w.shape[1]
    tn = 1024
    return pl.pallas_call(
        functools.partial(_out_proj_kernel, gate_row=gate_row),
        grid_spec=pltpu.PrefetchScalarGridSpec(
            num_scalar_prefetch=1, grid=(d // tn, m_rows // tm),
            in_specs=[pl.BlockSpec((tm, k_dim), lambda n, i, g: (i, 0)),
                      pl.BlockSpec((k_dim, tn), lambda n, i, g: (0, n)),
                      pl.BlockSpec((tm, tn), lambda n, i, g: (i, n)),
                      pl.BlockSpec((1, 6, tn), lambda n, i, g: (g[i], 0, n))],
            out_specs=pl.BlockSpec((tm, tn), lambda n, i, g: (i, n))),
        out_shape=jax.ShapeDtypeStruct((m_rows, d), F32),
        compiler_params=_cparams("arbitrary", "arbitrary"),
        name="out_proj",
    )(grp, y, w, x, mod)


def _in_proj_kernel(grp_ref, x_ref, mod_ref, nw_ref, w_ref, o_ref, h_buf):
    @pl.when(pl.program_id(1) == 0)
    def _():
        h_buf[...] = _rms_mod(x_ref[...], nw_ref[...], mod_ref[0, 0:1, :], mod_ref[0, 1:2, :]).astype(BF16)

    o_ref[...] = jnp.dot(h_buf[...], w_ref[...], preferred_element_type=F32)


def _in_proj(x, mod, grp, nw, w, tm, tn):
    m_rows, d = x.shape
    n = w.shape[1]
    return pl.pallas_call(
        _in_proj_kernel,
        grid_spec=pltpu.PrefetchScalarGridSpec(
            num_scalar_prefetch=1, grid=(m_rows // tm, n // tn),
            in_specs=[pl.BlockSpec((tm, d), lambda i, n_, g: (i, 0)),
                      pl.BlockSpec((1, 6, d), lambda i, n_, g: (g[i], 0, 0)),
                      pl.BlockSpec((1, d), lambda i, n_, g: (0, 0)),
                      pl.BlockSpec((d, tn), lambda i, n_, g: (0, n_))],
            out_specs=pl.BlockSpec((tm, tn), lambda i, n_, g: (i, n_)),
            scratch_shapes=[pltpu.VMEM((tm, d), BF16)]),
        out_shape=jax.ShapeDtypeStruct((m_rows, n), F32),
        compiler_params=_cparams("arbitrary", "arbitrary"),
        name="gdn_in_proj",
    )(grp, x, mod, nw, w)


def _ffn_kernel(grp_ref, x_ref, mod_ref, nw_ref, w1_ref, w3_ref, w2_ref, fw_ref, o_ref, h_buf, acc,
                *, final_norm):
    f = pl.program_id(1)

    @pl.when(f == 0)
    def _():
        h_buf[...] = _rms_mod(x_ref[...], nw_ref[...], mod_ref[0, 3:4, :], mod_ref[0, 4:5, :]).astype(BF16)
        acc[...] = jnp.zeros_like(acc)

    h = h_buf[...]
    gate = jnp.dot(h, w1_ref[...], preferred_element_type=F32)
    up = jnp.dot(h, w3_ref[...], preferred_element_type=F32)
    acc[...] += _dot(_silu(gate) * up, w2_ref[...])

    @pl.when(f == pl.num_programs(1) - 1)
    def _():
        y = x_ref[...] + mod_ref[0, 5:6, :] * acc[...]
        if final_norm:
            y = y * lax.rsqrt(jnp.mean(y * y, -1, keepdims=True) + EPS) * fw_ref[...]
        o_ref[...] = y


def _ffn(x, mod, grp, nw, w1, w3, w2, fw, final_norm, tm):
    m_rows, d = x.shape
    d_ff = w1.shape[1]
    tf = 512
    return pl.pallas_call(
        functools.partial(_ffn_kernel, final_norm=final_norm),
        grid_spec=pltpu.PrefetchScalarGridSpec(
            num_scalar_prefetch=1, grid=(m_rows // tm, d_ff // tf),
            in_specs=[pl.BlockSpec((tm, d), lambda i, f, g: (i, 0)),
                      pl.BlockSpec((1, 6, d), lambda i, f, g: (g[i], 0, 0)),
                      pl.BlockSpec((1, d), lambda i, f, g: (0, 0)),
                      pl.BlockSpec((d, tf), lambda i, f, g: (0, f)),
                      pl.BlockSpec((d, tf), lambda i, f, g: (0, f)),
                      pl.BlockSpec((tf, d), lambda i, f, g: (f, 0)),
                      pl.BlockSpec((1, d), lambda i, f, g: (0, 0))],
            out_specs=pl.BlockSpec((tm, d), lambda i, f, g: (i, 0)),
            scratch_shapes=[pltpu.VMEM((tm, d), BF16), pltpu.VMEM((tm, d), F32)]),
        out_shape=jax.ShapeDtypeStruct((m_rows, d), F32),
        compiler_params=_cparams("arbitrary", "arbitrary"),
        name="ffn",
    )(grp, x, mod, nw, w1, w3, w2, fw)


def _gdn_kernel(*refs, t_len, n_sb, has_s0):
    (q_ref, k_ref, v_ref, z_ref, ab_ref, cq_ref, ck_ref, cv_ref, alog_ref, dtb_ref, nw_ref) = refs[:11]
    rest = refs[11:]
    if has_s0:
        s0_ref, rest = rest[0], rest[1:]
    o_ref, sfin_ref, wq_buf, u_buf, attn_buf, ket_buf, gl_buf, obuf, s_ref, ext_buf = rest
    c_len = CHUNK
    n_chunks = t_len // c_len
    n_tot = n_sb * n_chunks
    last_row0 = n_sb * t_len - SUBLANES
    kh = pl.program_id(1)
    n_vh = 32
    tri_t = _iota((c_len, c_len), 0)
    tri_s = _iota((c_len, c_len), 1)
    lane_ab = _iota((c_len, 128), 1)
    row_abt = _iota((128, c_len), 0)
    lane_1 = _iota((1, 128), 1)
    lower = (tri_s <= tri_t).astype(F32)
    inv_masks = _inverse_masks(c_len)

    for s in range(n_sb):
        for j in range(4):
            if has_s0:
                s_ref[s * 4 + j] = s0_ref[s, j // 2, j % 2]
            else:
                s_ref[s * 4 + j] = jnp.zeros((GD_DK, GD_DV), F32)

    def conv_silu(ref, w_ref, cg):
        c = cg % n_chunks
        r0 = pl.multiple_of(cg * c_len, c_len)
        main = ref[pl.ds(r0, c_len), :]
        up0 = pl.multiple_of(jnp.maximum(r0 - SUBLANES, 0), SUBLANES)
        dn0 = pl.multiple_of(jnp.minimum(r0 + c_len, last_row0), SUBLANES)
        up = ref[pl.ds(up0, SUBLANES), :] * jnp.where(c > 0, 1.0, 0.0)
        dn = ref[pl.ds(dn0, SUBLANES), :] * jnp.where(c < n_chunks - 1, 1.0, 0.0)
        wd = main.shape[1]
        ext_buf[0:SUBLANES, 0:wd] = up
        ext_buf[SUBLANES:SUBLANES + c_len, 0:wd] = main
        ext_buf[SUBLANES + c_len:2 * SUBLANES + c_len, 0:wd] = dn
        w = w_ref[...]
        acc = main * w[1:2]
        for j in (0, 2, 3):
            acc = acc + ext_buf[SUBLANES - 1 + j:SUBLANES - 1 + j + c_len, 0:wd] * w[j:j + 1]
        return _silu(acc)

    def l2n(z):
        return z * lax.rsqrt(jnp.sum(z * z, -1, keepdims=True) + EPS)

    incl_m = [tri_s <= tri_t, tri_s >= tri_t]
    strict_m = [tri_s < tri_t, tri_s > tri_t]
    n_cu = 8

    def phase_a(cg0):
        cgs = [cg0 + i for i in range(n_cu)]
        rows = [pl.ds(pl.multiple_of(cg * c_len, c_len), c_len) for cg in cgs]
        q = [l2n(conv_silu(q_ref, cq_ref, cg)) * (GD_DK ** -0.5) for cg in cgs]
        k = [l2n(conv_silu(k_ref, ck_ref, cg)) for cg in cgs]
        v2 = [conv_silu(v_ref, cv_ref, cg) for cg in cgs]
        ab = [ab_ref[rw, :] for rw in rows]
        k_t = [x.T for x in k]
        g_all = [-jnp.exp(alog_ref[...]) * _softplus(x + dtb_ref[...]) for x in ab]
        beta_all = [jax.nn.sigmoid(x) for x in ab]
        prefix = [_mm(lower, g, "nn", P_CUM) for g in g_all]
        g_kk = [_mm(x, y, "nn", P_GRAM) for x, y in zip(k, k_t)]
        g_qk = [_mm(x, y, "nn", P_GRAM) for x, y in zip(q, k_t)]
        gtot_all = [jnp.sum(g, axis=0, keepdims=True) for g in g_all]
        gc_all = [jnp.where(lane_ab < 64, p, t - p + g) for p, t, g in zip(prefix, gtot_all, g_all)]
        gct_all = [x.T for x in gc_all]
        per = []
        for i in range(n_cu):
            for d in range(2):
                for vl in range(2):
                    col_g = d * 64 + 2 * kh + vl
                    col_b = col_g + n_vh
                    gc_col = jnp.sum(jnp.where(lane_ab == col_g, gc_all[i], 0.0), axis=1, keepdims=True)
                    beta = jnp.sum(jnp.where(lane_ab == col_b, beta_all[i], 0.0), axis=1, keepdims=True)
                    gc_row = jnp.sum(jnp.where(row_abt == col_g, gct_all[i], 0.0), axis=0, keepdims=True)
                    g_last = jnp.sum(jnp.where(lane_1 == col_g, gtot_all[i], 0.0), axis=1, keepdims=True)
                    decay = jnp.where(incl_m[d], jnp.exp(jnp.minimum(gc_col - gc_row, 0.0)), 0.0)
                    a_mat = jnp.where(strict_m[d], beta * g_kk[i] * decay, 0.0)
                    per.append((i, 2 * d + vl, gc_col, beta, gc_row, g_last, decay, a_mat))
        sel = [[n for n, p in enumerate(per) if p[1] // 2 == d] for d in range(2)]
        t_d = _unit_tri_inverse(*[jnp.stack([-per[n][7] for n in sl]) for sl in sel], inv_masks)
        t_all = [None] * len(per)
        for d in range(2):
            for m, n in enumerate(sel[d]):
                t_all[n] = t_d[d][m]
        rhs = [jnp.concatenate([v2[i][:, (j % 2) * GD_DV:(j % 2 + 1) * GD_DV] * beta,
                                k[i] * (beta * jnp.exp(gc_col))], axis=1)
               for i, j, gc_col, beta, _, _, _, _ in per]
        uw = [_mm(t_all[n], x, "nn", P_APPLY) for n, x in enumerate(rhs)]
        for n, (i, j, gc_col, beta, gc_row, g_last, decay, _) in enumerate(per):
            idx = cgs[i] * 4 + j
            u_buf[idx] = uw[n][:, :GD_DV]
            wq_buf[idx, 0:c_len, :] = uw[n][:, GD_DV:].astype(BF16)
            wq_buf[idx, c_len:, :] = (q[i] * jnp.exp(gc_col)).astype(BF16)
            attn_buf[idx] = (g_qk[i] * decay).astype(BF16)
            ket_buf[idx] = (k_t[i] * jnp.exp(g_last - gc_row)).astype(BF16)
            gl_buf[idx] = jnp.broadcast_to(jnp.exp(g_last), (SUBLANES, GD_DV))

    def a_body(i, carry):
        phase_a(n_cu * i)
        return carry

    lax.fori_loop(0, n_tot // n_cu, a_body, 0)

    def b_body(i, carry):
        chains = [(s, d, vl) for s in range(n_sb) for d in range(2) for vl in range(2)]
        cg = [s * n_chunks + (i if d == 0 else n_chunks - 1 - i) for s, d, _ in chains]
        idx = [c * 4 + 2 * d + vl for c, (_, d, vl) in zip(cg, chains)]
        s_mat = [s_ref[s * 4 + 2 * d + vl] for s, d, vl in chains]
        ws_qs = [_mm(wq_buf[n], x, "nn", P_APPLY) for n, x in zip(idx, s_mat)]
        v_new = [(u_buf[n] - x[0:c_len]).astype(BF16) for n, x in zip(idx, ws_qs)]
        o = [x[c_len:] + _mm(attn_buf[n], y, "nn", P_APPLY) for n, x, y in zip(idx, ws_qs, v_new)]
        s_new = [x * gl_buf[n, 0:1, :] + _mm(ket_buf[n], y, "nn", P_APPLY) for n, x, y in zip(idx, s_mat, v_new)]
        for n, (s, d, vl) in enumerate(chains):
            rows = pl.ds(pl.multiple_of(cg[n] * c_len, c_len), c_len)
            obuf[d, rows, pl.ds(vl * GD_DV, GD_DV)] = o[n]
            s_ref[s * 4 + 2 * d + vl] = s_new[n]
        return carry

    lax.fori_loop(0, n_chunks, b_body, 0)

    n_cc = 4

    def c_body(i, carry):
        tiles = [(pl.ds(pl.multiple_of((i * n_cc + q) * c_len, c_len), c_len), pl.ds(vl * GD_DV, GD_DV))
                 for q in range(n_cc) for vl in range(2)]
        o = [obuf[0, rw, cl] + obuf[1, rw, cl] for rw, cl in tiles]
        ms = [jnp.mean(x * x, -1, keepdims=True) for x in o]
        gate = [_silu(z_ref[rw, cl]) for rw, cl in tiles]
        for (rw, cl), x, m, g in zip(tiles, o, ms, gate):
            o_ref[rw, cl] = (x * lax.rsqrt(m + EPS) * nw_ref[...] * g).astype(o_ref.dtype)
        return carry

    lax.fori_loop(0, n_tot // n_cc, c_body, 0)
    for s in range(n_sb):
        for j in range(4):
            sfin_ref[s, j // 2, j % 2] = s_ref[s * 4 + j]


def _gdn_scan(proj, ab, conv_w, alog_row, dtb_row, nw, s0, row0, n_seq, t_len, n_sb):
    n_kh = 16
    blk_rows = n_sb * t_len
    assert row0 % blk_rows == 0 and n_seq % n_sb == 0 and (n_sb * t_len // CHUNK) % 2 == 0
    blk0 = row0 // blk_rows
    vw = 2 * GD_DV
    n_units = n_sb * (t_len // CHUNK) * 4
    in_specs = [pl.BlockSpec((blk_rows, GD_DK), lambda b, h: (blk0 + b, h)),
                pl.BlockSpec((blk_rows, GD_DK), lambda b, h: (blk0 + b, n_kh + h)),
                pl.BlockSpec((blk_rows, vw), lambda b, h: (blk0 + b, n_kh + h)),
                pl.BlockSpec((blk_rows, vw), lambda b, h: (blk0 + b, 2 * n_kh + h)),
                pl.BlockSpec((blk_rows, 128), lambda b, h: (blk0 + b, 0)),
                pl.BlockSpec((GD_CONV, GD_DK), lambda b, h: (0, h)),
                pl.BlockSpec((GD_CONV, GD_DK), lambda b, h: (0, n_kh + h)),
                pl.BlockSpec((GD_CONV, vw), lambda b, h: (0, n_kh + h)),
                pl.BlockSpec((1, 128), lambda b, h: (0, 0)),
                pl.BlockSpec((1, 128), lambda b, h: (0, 0)),
                pl.BlockSpec((1, GD_DV), lambda b, h: (0, 0))]
    args = [proj, proj, proj, proj, ab, conv_w, conv_w, conv_w, alog_row, dtb_row, nw]
    if s0 is not None:
        in_specs.append(pl.BlockSpec((n_sb, 2, 2, GD_DK, GD_DV), lambda b, h: (b, 0, h, 0, 0)))
        args.append(s0)
    o, sfin = pl.pallas_call(
        functools.partial(_gdn_kernel, t_len=t_len, n_sb=n_sb, has_s0=s0 is not None),
        grid=(n_seq // n_sb, n_kh),
        in_specs=in_specs,
        out_specs=[pl.BlockSpec((blk_rows, vw), lambda b, h: (b, h)),
                   pl.BlockSpec((n_sb, 2, 2, GD_DK, GD_DV), lambda b, h: (b, 0, h, 0, 0))],
        out_shape=[jax.ShapeDtypeStruct((n_seq * t_len, 2 * n_kh * GD_DV), BF16),
                   jax.ShapeDtypeStruct((n_seq, 2, 2 * n_kh, GD_DK, GD_DV), F32)],
        scratch_shapes=[pltpu.VMEM((n_units, 2 * CHUNK, GD_DV), BF16),
                        pltpu.VMEM((n_units, CHUNK, GD_DV), F32),
                        pltpu.VMEM((n_units, CHUNK, CHUNK), BF16),
                        pltpu.VMEM((n_units, GD_DK, CHUNK), BF16),
                        pltpu.VMEM((n_units, SUBLANES, GD_DV), F32),
                        pltpu.VMEM((2, blk_rows, vw), F32),
                        pltpu.VMEM((4 * n_sb, GD_DK, GD_DV), F32),
                        pltpu.VMEM((CHUNK + 2 * SUBLANES, vw), F32)],
        compiler_params=_cparams("arbitrary", "arbitrary"),
        name=f"gdn_scan_t{t_len}",
    )(*args)
    return o, sfin


def _grid_pos_embed(n_tokens, d_model):
    rows = n_tokens // GRID_W
    f32 = np.float32
    row = np.broadcast_to(np.arange(rows, dtype=f32)[:, None], (rows, GRID_W)).reshape(-1)
    col = np.broadcast_to(np.arange(GRID_W, dtype=f32)[None, :], (rows, GRID_W)).reshape(-1)
    quarter = d_model // 4
    omega = (f32(1.0) / (f32(POS_BASE) ** (np.arange(quarter, dtype=f32) / f32(quarter)))).astype(f32)
    ar = (row[:, None] * omega).astype(f32)
    ac = (col[:, None] * omega).astype(f32)
    return jnp.asarray(np.concatenate([np.sin(ar), np.cos(ar), np.sin(ac), np.cos(ac)], -1).astype(f32))


def _pad_axis(a, axis, size):
    pad = [(0, 0)] * a.ndim
    pad[axis] = (0, size - a.shape[axis])
    return jnp.pad(a, pad)


def kernel(x_prompt, x_sample, state_rwkv, state_gdn, c, c_ctx, norm_mix, norm_ffn, norm_final, w_mod, b_mod, ffn_w1, ffn_w3, ffn_w2, rw_mu, rw_wr, rw_wk, rw_wv, rw_wo, rw_w0, rw_w1, rw_w2, rw_a0, rw_a1, rw_a2, rw_g1, rw_g2, rw_kk, rw_ka, rw_rk, rw_ln_w, rw_ln_b, gd_w_in, gd_conv, gd_a_log, gd_dt_bias, gd_norm, gd_w_out):
    n_p, t_p, d = x_prompt.shape
    n_s, t_s, _ = x_sample.shape
    bf = lambda a: a.astype(BF16)
    row = lambda a: a.reshape(1, -1)
    tm_shift = 256
    tm_big = 512
    tm_wide = 1024

    regions = [
        dict(x=x_prompt.reshape(n_p * t_p, d), n_seq=n_p, t_len=t_p, groups=[0] * n_p, n_sb=4, n_sb_gd=8,
             s0_rw=None, s0_gd=None),
        dict(x=(x_sample + _grid_pos_embed(t_s, d)[None]).reshape(n_s * t_s, d), n_seq=n_s, t_len=t_s,
             groups=list(range(1, n_s + 1)), n_sb=1, n_sb_gd=1,
             s0_rw=state_rwkv[:, 0].reshape(n_s, 2, d, RW_HEAD), s0_gd=state_gdn[:, 0]),
    ]
    for reg in regions:
        layout = [(reg["n_seq"], reg["t_len"], reg["groups"])]
        reg["tables"] = _tile_tables(layout, tm_shift)
        reg["grp"] = _tile_tables(layout, tm_big)[0]
        reg["grp_wide"] = _tile_tables(layout, tm_wide)[0]

    cond = _pad_axis(jnp.concatenate([c_ctx[None], c], axis=0), 0, SUBLANES)
    mod = _modulation(cond, w_mod, b_mod).reshape(w_mod.shape[0], SUBLANES, 6, d)

    w_rkv = bf(jnp.stack([rw_wr[0], rw_wk[0], rw_wv[0]]))
    lp = RW_LORA_PAD
    lora_w = (bf(rw_g1[0]), bf(rw_g2[0]),
              bf(_pad_axis(rw_w1[0], 2, lp)), bf(_pad_axis(rw_w2[0], 1, lp)),
              bf(_pad_axis(rw_a1[0], 2, lp)), bf(_pad_axis(rw_a2[0], 1, lp)),
              rw_w0[0][:, None, :], rw_a0[0][:, None, :])
    scan_p = (row(rw_kk[0]), row(rw_ka[0]), row(rw_rk[0]), row(rw_ln_w[0]), row(rw_ln_b[0]))
    w_o, ffn0 = _layer_bf16(rw_wo, 0), (_layer_bf16(ffn_w1, 0), _layer_bf16(ffn_w3, 0), _layer_bf16(ffn_w2, 0))
    for reg in regions:
        rkv, gate, lw, a = _rwkv_inputs(reg["x"], mod[0], reg["tables"], row(norm_mix[0]), rw_mu[0], w_rkv,
                                        *lora_w, tm_shift)
        y, reg["s_rwkv"] = _rwkv_scan(rkv, gate, lw, a, *scan_p, reg["s0_rw"], 0, reg["n_seq"], reg["t_len"],
                                      reg["n_sb"])
        x = _out_proj(y, w_o, reg["x"], mod[0], reg["grp"], 2, tm_big)
        reg["x"] = _ffn(x, mod[0], reg["grp"], row(norm_ffn[0]), *ffn0, row(norm_final), False, tm_big)

    n_main = 3 * 4096
    w_in_main = _layer_bf16(gd_w_in, 0, 0, n_main)
    w_in_ab = _layer_bf16(gd_w_in, 0, n_main, gd_w_in.shape[2] - n_main)
    zeros32 = jnp.zeros((2, 32), F32)
    alog_row = jnp.stack([gd_a_log[0], zeros32], axis=1).reshape(1, 128)
    dtb_row = jnp.stack([gd_dt_bias[0], zeros32], axis=1).reshape(1, 128)
    w_out, ffn1 = _layer_bf16(gd_w_out, 0), (_layer_bf16(ffn_w1, 1), _layer_bf16(ffn_w3, 1), _layer_bf16(ffn_w2, 1))
    for reg in regions:
        proj = _in_proj(reg["x"], mod[1], reg["grp_wide"], row(norm_mix[1]), w_in_main, tm_wide, 1024)
        ab = _in_proj(reg["x"], mod[1], reg["grp_wide"], row(norm_mix[1]), w_in_ab, tm_wide, 128)
        o, reg["s_gdn"] = _gdn_scan(proj, ab, gd_conv[0], alog_row, dtb_row, row(gd_norm[0]), reg["s0_gd"], 0,
                                    reg["n_seq"], reg["t_len"], reg["n_sb_gd"])
        x = _out_proj(o, w_out, reg["x"], mod[1], reg["grp"], 2, tm_big)
        reg["x"] = _ffn(x, mod[1], reg["grp"], row(norm_ffn[1]), *ffn1, row(norm_final), True, tm_big)

    y_prompt = regions[0]["x"].reshape(n_p, t_p, d)
    y_sample = regions[1]["x"].reshape(n_s, t_s, d)
    new_state_rwkv = regions[0]["s_rwkv"].reshape(n_p, 1, 2, d // RW_HEAD, RW_HEAD, RW_HEAD)
    new_state_gdn = regions[0]["s_gdn"].reshape(n_p, 1, 2, 32, GD_DK, GD_DV)
    return (y_prompt, y_sample, new_state_rwkv, new_state_gdn)
```

```python
import functools

import numpy as np
import jax
import jax.numpy as jnp
from jax import lax
from jax.experimental import pallas as pl
from jax.experimental.pallas import tpu as pltpu

F32 = jnp.float32
BF16 = jnp.bfloat16

EPS = 1e-6
GRID_W = 64
POS_BASE = 10000.0
RW_HEAD = 64
RW_GROUP_HEADS = 4
RW_LANES = RW_HEAD * RW_GROUP_HEADS
RW_LORA_PAD = 128
DECAY_SCALE = 0.6065306597126334
GD_DK = 128
GD_DV = 128
GD_CONV = 4
CHUNK = 64
RW_SUPER_CHUNKS = 8
SUBLANES = 8
VMEM_LIMIT = 56 * 1024 * 1024
CAST_BLOCK_BYTES = 8 * 1024 * 1024


def _cparams(*sem):
    return pltpu.CompilerParams(dimension_semantics=sem, vmem_limit_bytes=VMEM_LIMIT)


def _dot(a, b):
    return jnp.dot(a.astype(BF16), b.astype(BF16), preferred_element_type=F32)


_DIMS = {"nn": (((1,), (0,)), ((), ())),
         "nt": (((1,), (1,)), ((), ())),
         "tn": (((0,), (0,)), ((), ())),
         "bnn": (((2,), (1,)), ((0,), (0,)))}


def _split_bf16(x, pieces):
    out = []
    for _ in range(pieces - 1):
        p = x.astype(BF16)
        out.append(p)
        x = x - p.astype(F32)
    out.append(x.astype(BF16))
    return out


def _mm(a, b, dims="nn", mode="b"):
    dn = _DIMS[dims]
    dg = lambda x, y: lax.dot_general(x, y, dn, preferred_element_type=F32)
    if mode == "b":
        return dg(a.astype(BF16), b.astype(BF16))
    if mode in ("la", "la2"):
        a0 = a.astype(BF16)
        return sum(dg(a0, p) for p in _split_bf16(b, 3 if mode == "la" else 2))
    if mode in ("ra", "ra2"):
        b0 = b.astype(BF16)
        return sum(dg(p, b0) for p in _split_bf16(a, 3 if mode == "ra" else 2))
    raise ValueError(mode)


P_INV = "b"
P_GRAM = "b"
P_APPLY = "b"
P_SUM = "ra2"
P_CUM = "la2"


def _iota(shape, dim):
    return lax.broadcasted_iota(jnp.int32, shape, dim)


def _silu(x):
    return x * jax.nn.sigmoid(x)


def _softplus(x):
    return jnp.maximum(x, 0.0) + jnp.log(1.0 + jnp.exp(-jnp.abs(x)))


def _rms_mod(x, nw, sh, sc):
    y = x * lax.rsqrt(jnp.mean(x * x, -1, keepdims=True) + EPS)
    return (y * nw) * (1.0 + sc) + sh


def _inverse_masks(c_len):
    ij = _iota((c_len, c_len), 0) ^ _iota((c_len, c_len), 1)
    levels = []
    half = 2
    while half < c_len:
        levels.append(((ij >= half) & (ij < 2 * half)).astype(BF16))
        half *= 2
    return (ij == 0).astype(F32), (ij == 1).astype(F32), levels


def _unit_tri_inverse(n_lower, n_upper, masks):
    eye, pair, levels = masks
    mats = [n_lower, n_upper]
    c_len = n_lower.shape[-1]
    d_mats = [eye + m * pair for m in mats]
    n_bf = [m.astype(BF16) for m in mats]
    half = 2
    for lvl in levels:
        l_mats = [m * lvl for m in n_bf]
        if half < SUBLANES:
            dl = [_mm(d, l, "bnn", P_INV) for d, l in zip(d_mats, l_mats)]
            d_mats = [d + _mm(x, d, "bnn", P_INV) for d, x in zip(d_mats, dl)]
        else:
            first = [slice(b0, b0 + half) for b0 in range(0, c_len, 2 * half)]
            second = [slice(b0 + half, b0 + 2 * half) for b0 in range(0, c_len, 2 * half)]
            act = [second, first]
            d_act = [jnp.concatenate([d[:, sl, :] for sl in rows], axis=1) for d, rows in zip(d_mats, act)]
            dl = [_mm(x, l, "bnn", P_INV) for x, l in zip(d_act, l_mats)]
            upd = [x + _mm(y, d, "bnn", P_INV) for x, y, d in zip(d_act, dl, d_mats)]
            out = []
            for g, (d, u) in enumerate(zip(d_mats, upd)):
                parts = []
                for n in range(len(first)):
                    new = u[:, n * half:(n + 1) * half, :]
                    parts += [d[:, first[n], :], new] if g == 0 else [new, d[:, second[n], :]]
                out.append(jnp.concatenate(parts, axis=1))
            d_mats = out
        half *= 2
    return d_mats


def _cast_kernel(w_ref, o_ref):
    o_ref[...] = w_ref[...].astype(BF16)


def _layer_bf16(w, layer, col0=0, n_cols=None):
    _, k_dim, n = w.shape
    n_cols = n if n_cols is None else n_cols
    assert col0 % n_cols == 0
    tr = 128
    while tr * 2 * n_cols * 4 <= CAST_BLOCK_BYTES and k_dim % (tr * 2) == 0:
        tr *= 2
    return pl.pallas_call(
        _cast_kernel,
        grid=(k_dim // tr,),
        in_specs=[pl.BlockSpec((None, tr, n_cols), lambda i: (layer, i, col0 // n_cols))],
        out_specs=pl.BlockSpec((tr, n_cols), lambda i: (i, 0)),
        out_shape=jax.ShapeDtypeStruct((k_dim, n_cols), BF16),
        compiler_params=_cparams("arbitrary"),
        name="weight_to_bf16",
    )(w)


def _mod_kernel(c_ref, w_ref, b_ref, o_ref):
    o_ref[0] = _dot(_silu(c_ref[...]), w_ref[0]) + b_ref[0]


def _modulation(cond, w_mod, b_mod):
    depth, d, n = w_mod.shape
    tn = 1024
    return pl.pallas_call(
        _mod_kernel,
        grid=(depth, n // tn),
        in_specs=[pl.BlockSpec((SUBLANES, d), lambda l, j: (0, 0)),
                  pl.BlockSpec((1, d, tn), lambda l, j: (l, 0, j)),
                  pl.BlockSpec((1, 1, tn), lambda l, j: (l, 0, j))],
        out_specs=pl.BlockSpec((1, SUBLANES, tn), lambda l, j: (l, 0, j)),
        out_shape=jax.ShapeDtypeStruct((depth, SUBLANES, n), F32),
        compiler_params=_cparams("arbitrary", "arbitrary"),
        name="modulation",
    )(cond, w_mod, b_mod.reshape(depth, 1, n))


def _shift_mix(x_ref, xp_ref, xn_ref, mod_ref, nw_ref, has_prev, has_next):
    nw = nw_ref[...]
    sh = mod_ref[0, 0:1, :]
    sc = mod_ref[0, 1:2, :]
    h = _rms_mod(x_ref[...], nw, sh, sc)
    tm = h.shape[0]
    hp = _rms_mod(xp_ref[SUBLANES - 1:SUBLANES, :], nw, sh, sc) * has_prev
    hn = _rms_mod(xn_ref[0:1, :], nw, sh, sc) * has_next
    rows = _iota(h.shape, 0)
    prev = jnp.where(rows == 0, hp, pltpu.roll(h, 1, 0))
    nxt = jnp.where(rows == tm - 1, hn, pltpu.roll(h, tm - 1, 0))
    return h, 0.5 * (prev + nxt) - h


def _rkv_kernel(grp_ref, hp_ref, hn_ref, x_ref, xp_ref, xn_ref, mod_ref, nw_ref, mu_ref, w_ref, o_ref):
    i = pl.program_id(1)
    h, xx = _shift_mix(x_ref, xp_ref, xn_ref, mod_ref, nw_ref,
                       hp_ref[i].astype(F32), hn_ref[i].astype(F32))
    o_ref[0] = _dot(h + xx * mu_ref[0], w_ref[0])


def _lora_kernel(grp_ref, hp_ref, hn_ref, x_ref, xp_ref, xn_ref, mod_ref, nw_ref, mu_ref,
                 g1_ref, g2_ref, w1_ref, w2_ref, a1_ref, a2_ref, w0_ref, a0_ref,
                 gate_ref, lw_ref, a_ref):
    i = pl.program_id(0)
    h, xx = _shift_mix(x_ref, xp_ref, xn_ref, mod_ref, nw_ref,
                       hp_ref[i].astype(F32), hn_ref[i].astype(F32))
    xw = h + xx * mu_ref[1:2, :]
    xa = h + xx * mu_ref[4:5, :]
    xg = h + xx * mu_ref[5:6, :]
    gate_ref[...] = _dot(jax.nn.sigmoid(_dot(xg, g1_ref[...])), g2_ref[...])
    for d in range(2):
        lw = jnp.tanh(_dot(xw, w1_ref[d]))
        lw_ref[d] = -DECAY_SCALE * jax.nn.sigmoid(w0_ref[d] + _dot(lw, w2_ref[d]))
        a_ref[d] = jax.nn.sigmoid(a0_ref[d] + _dot(_dot(xa, a1_ref[d]), a2_ref[d]))


def _tile_tables(regions, tm):
    row_grp, row_pos, row_len = [], [], []
    for n_seq, t_len, groups in regions:
        for s in range(n_seq):
            row_grp += [groups[s]] * t_len
            row_pos += list(range(t_len))
            row_len += [t_len] * t_len
    row_grp, row_pos, row_len = (np.asarray(v).reshape(-1, tm) for v in (row_grp, row_pos, row_len))
    assert (row_grp == row_grp[:, :1]).all(), "a row tile must not straddle modulation groups"
    as_i32 = lambda v: jnp.asarray(np.asarray(v, np.int32))
    return (as_i32(row_grp[:, 0]), as_i32(row_pos[:, 0] > 0),
            as_i32(row_pos[:, -1] < row_len[:, -1] - 1))


def _halo_specs(tm, d, m_rows, n_lead):
    blocks = tm // SUBLANES
    last = m_rows // SUBLANES - 1
    if n_lead == 1:
        cur = lambda j, i, *_: (i, 0)
        prev = lambda j, i, *_: (jnp.maximum(i * blocks - 1, 0), 0)
        nxt = lambda j, i, *_: (jnp.minimum((i + 1) * blocks, last), 0)
    else:
        cur = lambda i, *_: (i, 0)
        prev = lambda i, *_: (jnp.maximum(i * blocks - 1, 0), 0)
        nxt = lambda i, *_: (jnp.minimum((i + 1) * blocks, last), 0)
    return [pl.BlockSpec((tm, d), cur), pl.BlockSpec((SUBLANES, d), prev), pl.BlockSpec((SUBLANES, d), nxt)]


def _rwkv_inputs(x, mod, tables, nw, mu, w_rkv, g1, g2, w1, w2, a1, a2, w0, a0, tm):
    m_rows, d = x.shape
    n_tiles = m_rows // tm
    mu_rkv = jnp.stack([mu[0], mu[2], mu[3]])[:, None, :]
    rkv = pl.pallas_call(
        _rkv_kernel,
        grid_spec=pltpu.PrefetchScalarGridSpec(
            num_scalar_prefetch=3, grid=(3, n_tiles),
            in_specs=_halo_specs(tm, d, m_rows, 1) + [
                pl.BlockSpec((1, 6, d), lambda j, i, g, *_: (g[i], 0, 0)),
                pl.BlockSpec((1, d), lambda j, i, *_: (0, 0)),
                pl.BlockSpec((1, 1, d), lambda j, i, *_: (j, 0, 0)),
                pl.BlockSpec((1, d, d), lambda j, i, *_: (j, 0, 0))],
            out_specs=pl.BlockSpec((1, tm, d), lambda j, i, *_: (j, i, 0))),
        out_shape=jax.ShapeDtypeStruct((3, m_rows, d), F32),
        compiler_params=_cparams("arbitrary", "arbitrary"),
        name="rwkv_rkv_proj",
    )(*tables, x, x, x, mod, nw, mu_rkv, w_rkv)

    full = lambda a: pl.BlockSpec(a.shape, lambda i, *_: (0,) * a.ndim)
    small = [g1, g2, w1, w2, a1, a2, w0, a0]
    gate, lw, a = pl.pallas_call(
        _lora_kernel,
        grid_spec=pltpu.PrefetchScalarGridSpec(
            num_scalar_prefetch=3, grid=(n_tiles,),
            in_specs=_halo_specs(tm, d, m_rows, 0) + [
                pl.BlockSpec((1, 6, d), lambda i, g, *_: (g[i], 0, 0)),
                pl.BlockSpec((1, d), lambda i, *_: (0, 0)),
                full(mu)] + [full(s) for s in small],
            out_specs=[pl.BlockSpec((tm, d), lambda i, *_: (i, 0)),
                       pl.BlockSpec((2, tm, d), lambda i, *_: (0, i, 0)),
                       pl.BlockSpec((2, tm, d), lambda i, *_: (0, i, 0))]),
        out_shape=[jax.ShapeDtypeStruct((m_rows, d), F32),
                   jax.ShapeDtypeStruct((2, m_rows, d), F32),
                   jax.ShapeDtypeStruct((2, m_rows, d), F32)],
        compiler_params=_cparams("arbitrary"),
        name="rwkv_lora_proj",
    )(*tables, x, x, x, mod, nw, mu, *small)
    return rkv, gate, lw, a


def _rwkv_scan_kernel(*refs, t_len, n_sb, has_s0):
    (r_ref, k_ref, v_ref, gate_ref, lw_ref, a_ref, kk_ref, ka_ref, rk_ref, lnw_ref, lnb_ref) = refs[:11]
    rest = refs[11:]
    if has_s0:
        s0_ref, rest = rest[0], rest[1:]
    (y_ref, sfin_ref, r2_buf, a2_buf, uv_buf, bk_buf, y0_buf, w_buf, ybuf, bonus_buf, s_ref) = rest
    c_len, lanes, heads = CHUNK, RW_LANES, RW_GROUP_HEADS
    n_chunks = t_len // c_len
    sc = min(n_chunks, RW_SUPER_CHUNKS)
    n_super = n_chunks // sc
    n_pp = max(1, 8 // (2 * n_sb))
    stack = heads * c_len

    same_head = (_iota((stack, lanes), 0) // c_len == _iota((stack, lanes), 1) // RW_HEAD).astype(F32)
    ones_blk = same_head
    rep = (_iota((RW_HEAD, lanes), 1) % RW_HEAD == _iota((RW_HEAD, lanes), 0)).astype(F32)
    t_idx = _iota((stack, c_len), 0) % c_len
    s_idx = _iota((stack, c_len), 1)
    tri_t = _iota((c_len, c_len), 0)
    tri_s = _iota((c_len, c_len), 1)

    inv_masks = _inverse_masks(c_len)
    kk_p, ka_p, rk_p = kk_ref[...], ka_ref[...], rk_ref[...]
    lnw, lnb = lnw_ref[...], lnb_ref[...]

    def fold(z):
        z = z * same_head
        return z[0:c_len] + z[c_len:2 * c_len] + z[2 * c_len:3 * c_len] + z[3 * c_len:4 * c_len]

    def tile4(z):
        return jnp.concatenate([z, z, z, z], axis=0) * same_head

    for s in range(n_sb):
        for d in range(2):
            if has_s0:
                s_ref[s * 2 + d] = _mm(s0_ref[s, d], rep, "nn", "ra") * ones_blk
            else:
                s_ref[s * 2 + d] = jnp.zeros((lanes, lanes), F32)
    ybuf[...] = jnp.zeros_like(ybuf)
    bonus_buf[...] = jnp.zeros_like(bonus_buf)

    def chunk_rows(s, d, p):
        c = p if d == 0 else n_chunks - 1 - p
        return pl.ds(pl.multiple_of((s * n_chunks + c) * c_len, c_len), c_len)

    incl_f = [(tri_s <= tri_t).astype(F32), (tri_s >= tri_t).astype(F32)]
    strict4 = [s_idx < t_idx, s_idx > t_idx]
    t_idx2 = _iota((stack, 2 * c_len), 0) % c_len
    s_idx2 = _iota((stack, 2 * c_len), 1) % c_len
    right = _iota((stack, 2 * c_len), 1) >= c_len
    strict_right = [right & (s_idx2 < t_idx2), right & (s_idx2 > t_idx2)]
    incl_both = [s_idx2 <= t_idx2, s_idx2 >= t_idx2]

    def phase_a(p0, j0):
        units = [(s, d, q) for s in range(n_sb) for d in range(2) for q in range(n_pp)]
        dirs = [d for _, d, _ in units]
        rows = [chunk_rows(s, d, p0 + q) for s, d, q in units]
        r = [r_ref[rw, :] for rw in rows]
        k = [k_ref[rw, :] for rw in rows]
        v = [v_ref[rw, :] for rw in rows]
        lw = [lw_ref[d, rw, :] for d, rw in zip(dirs, rows)]
        a = [a_ref[d, rw, :] for d, rw in zip(dirs, rows)]
        bonus_old = [bonus_buf[rw, :] for rw in rows]
        kx = [ki * kk_p for ki in k]
        n_u = len(units)
        rows_of = lambda z: [z[i * c_len:(i + 1) * c_len] for i in range(n_u)]
        ss = rows_of(_mm(jnp.concatenate([x * x for x in kx], axis=0), ones_blk, "nn", P_SUM))
        cw = [_mm(incl_f[d], x, "nn", P_CUM) for d, x in zip(dirs, lw)]
        bsum = rows_of(_mm(jnp.concatenate([ri * ki * (1.0 + (ai - 1.0) * ka_p) * rk_p
                                            for ri, ki, ai in zip(r, k, a)], axis=0), ones_blk, "nn", P_SUM))
        kkn = [x * lax.rsqrt(y + EPS) for x, y in zip(kx, ss)]
        kd = [ki * (1.0 + (ai - 1.0) * ka_p) for ki, ai in zip(k, a)]
        w_row = [jnp.exp(jnp.sum(x, axis=0, keepdims=True)) for x in lw]
        e_neg = [jnp.exp(-x) for x in cw]
        at = [-kn * jnp.exp(c - l) for kn, c, l in zip(kkn, cw, lw)]
        bt = [kn * ai * e for kn, ai, e in zip(kkn, a, e_neg)]
        kt = [x * e for x, e in zip(kd, e_neg)]
        rt = [ri * jnp.exp(c) for ri, c in zip(r, cw)]
        lhs = [jnp.concatenate([tile4(x), tile4(y)], axis=0) for x, y in zip(at, rt)]
        rhs = [jnp.concatenate([x, y], axis=0) for x, y in zip(bt, kt)]
        gram = [_mm(x, y, "nt", P_GRAM) for x, y in zip(lhs, rhs)]
        n_s = [jnp.where(strict4[d], g[0:stack, 0:c_len], 0.0) for d, g in zip(dirs, gram)]
        ak_m = [jnp.where(strict_right[d], g[0:stack], 0.0) for d, g in zip(dirs, gram)]
        r_m = [jnp.where(incl_both[d], g[stack:], 0.0) for d, g in zip(dirs, gram)]
        sel = [[i for i, di in enumerate(dirs) if di == d] for d in range(2)]
        t_d = _unit_tri_inverse(*[jnp.concatenate([n_s[i].reshape(heads, c_len, c_len) for i in sl], axis=0)
                                  for sl in sel], inv_masks)
        t_s = [None] * len(units)
        for d in range(2):
            for n, i in enumerate(sel[d]):
                t_s[i] = t_d[d][n * heads:(n + 1) * heads].reshape(stack, c_len)
        a2 = [fold(_mm(t, x, "nn", P_APPLY)) for t, x in zip(t_s, at)]
        akv = [fold(_mm(x, jnp.concatenate([vi, vi], axis=0), "nn", P_APPLY)) for x, vi in zip(ak_m, v)]
        u0 = [fold(_mm(t, x, "nn", P_APPLY)) for t, x in zip(t_s, akv)]
        r2 = [x + fold(_mm(rm[:, 0:c_len], y, "nn", P_APPLY)) for x, rm, y in zip(rt, r_m, a2)]
        y0 = [fold(_mm(rm, jnp.concatenate([x, vi], axis=0), "nn", P_APPLY)) for rm, x, vi in zip(r_m, u0, v)]
        for i, (s, d, q) in enumerate(units):
            slot = (s * 2 + d) * sc + j0 + q
            r2_buf[slot] = r2[i].astype(BF16)
            a2_buf[slot] = a2[i].astype(BF16)
            uv_buf[slot, 0:c_len, :] = u0[i].astype(BF16)
            uv_buf[slot, c_len:, :] = v[i].astype(BF16)
            bk_buf[slot, 0:c_len, :] = (bt[i] * w_row[i]).astype(BF16)
            bk_buf[slot, c_len:, :] = (kt[i] * w_row[i]).astype(BF16)
            y0_buf[slot] = y0[i]
            w_buf[slot] = jnp.broadcast_to(w_row[i], (SUBLANES, lanes))
            bonus_buf[rows[i], :] = bonus_old[i] + bsum[i] * v[i]

    def phase_b(p, j):
        units = [(s, d) for s in range(n_sb) for d in range(2)]
        rows = [chunk_rows(s, d, p) for s, d in units]
        slots = [(s * 2 + d) * sc + j for s, d in units]
        s_mat = [s_ref[s * 2 + d] for s, d in units]
        y_old = [ybuf[rw, :] for rw in rows]
        s_bf = [x.astype(BF16) for x in s_mat]
        m_s = [_mm(a2_buf[sl], bk_buf[sl, 0:c_len, :], "tn", P_APPLY) * ones_blk for sl in slots]
        s_add = [_mm(uv_buf[sl], bk_buf[sl], "tn", P_APPLY) * ones_blk for sl in slots]
        y = [_mm(r2_buf[sl], x, "nt", P_APPLY) + y0_buf[sl] for sl, x in zip(slots, s_bf)]
        s_new = [x * w_buf[sl, 0:1, :] + _mm(xb, m, "nn", P_APPLY) + z
                 for x, xb, m, z, sl in zip(s_mat, s_bf, m_s, s_add, slots)]
        for i, (s, d) in enumerate(units):
            ybuf[rows[i], :] = y_old[i] + y[i]
            s_ref[s * 2 + d] = s_new[i]

    def super_body(sp, carry):
        def a_body(j, c2):
            phase_a(sp * sc + j * n_pp, j * n_pp)
            return c2

        def b_body(j, c2):
            phase_b(sp * sc + j, j)
            return c2

        lax.fori_loop(0, sc // n_pp, a_body, 0)
        lax.fori_loop(0, sc, b_body, 0)
        return carry

    lax.fori_loop(0, n_super, super_body, 0)

    n_cc = 4

    def c_body(i, carry):
        rows = [pl.ds(pl.multiple_of((i * n_cc + q) * c_len, c_len), c_len) for q in range(n_cc)]
        y = [ybuf[rw, :] for rw in rows]
        mean = [_mm(x, ones_blk, "nn", P_SUM) * (1.0 / RW_HEAD) for x in y]
        yc = [x - m for x, m in zip(y, mean)]
        var = [_mm(x * x, ones_blk, "nn", P_SUM) * (1.0 / RW_HEAD) for x in yc]
        for rw, x, vr in zip(rows, yc, var):
            yn = x * lax.rsqrt(vr + RW_HEAD * 1e-5) * lnw + lnb
            y_ref[rw, :] = ((yn + bonus_buf[rw, :]) * gate_ref[rw, :]).astype(y_ref.dtype)
        return carry

    lax.fori_loop(0, n_sb * n_chunks // n_cc, c_body, 0)
    rep_t = (_iota((lanes, RW_HEAD), 0) % RW_HEAD == _iota((lanes, RW_HEAD), 1)).astype(F32)
    for s in range(n_sb):
        for d in range(2):
            sfin_ref[s, d] = _mm(s_ref[s * 2 + d], rep_t, "nn", "ra")


def _rwkv_scan(rkv, gate, lw, a, kk, ka, rk, lnw, lnb, s0, row0, n_seq, t_len, n_sb):
    _, m_rows, d = rkv.shape
    n_groups = d // RW_LANES
    blk_rows = n_sb * t_len
    assert row0 % blk_rows == 0 and n_seq % n_sb == 0
    blk0 = row0 // blk_rows
    sc = min(t_len // CHUNK, RW_SUPER_CHUNKS)
    assert (t_len // CHUNK) % sc == 0
    n_slots = n_sb * 2 * sc
    tok = lambda b, g: (blk0 + b, g)
    in_specs = [pl.BlockSpec((None, blk_rows, RW_LANES), lambda b, g, j=j: (j, blk0 + b, g)) for j in range(3)]
    in_specs += [pl.BlockSpec((blk_rows, RW_LANES), tok),
                 pl.BlockSpec((2, blk_rows, RW_LANES), lambda b, g: (0, blk0 + b, g)),
                 pl.BlockSpec((2, blk_rows, RW_LANES), lambda b, g: (0, blk0 + b, g))]
    in_specs += [pl.BlockSpec((1, RW_LANES), lambda b, g: (0, g))] * 5
    args = [rkv, rkv, rkv, gate, lw, a, kk, ka, rk, lnw, lnb]
    if s0 is not None:
        in_specs.append(pl.BlockSpec((n_sb, 2, RW_LANES, RW_HEAD), lambda b, g: (b, 0, g, 0)))
        args.append(s0)
    y, sfin = pl.pallas_call(
        functools.partial(_rwkv_scan_kernel, t_len=t_len, n_sb=n_sb, has_s0=s0 is not None),
        grid=(n_seq // n_sb, n_groups),
        in_specs=in_specs,
        out_specs=[pl.BlockSpec((blk_rows, RW_LANES), lambda b, g: (b, g)),
                   pl.BlockSpec((n_sb, 2, RW_LANES, RW_HEAD), lambda b, g: (b, 0, g, 0))],
        out_shape=[jax.ShapeDtypeStruct((n_seq * t_len, d), BF16),
                   jax.ShapeDtypeStruct((n_seq, 2, d, RW_HEAD), F32)],
        scratch_shapes=[pltpu.VMEM((n_slots, CHUNK, RW_LANES), BF16),
                        pltpu.VMEM((n_slots, CHUNK, RW_LANES), BF16),
                        pltpu.VMEM((n_slots, 2 * CHUNK, RW_LANES), BF16),
                        pltpu.VMEM((n_slots, 2 * CHUNK, RW_LANES), BF16),
                        pltpu.VMEM((n_slots, CHUNK, RW_LANES), F32),
                        pltpu.VMEM((n_slots, SUBLANES, RW_LANES), F32),
                        pltpu.VMEM((blk_rows, RW_LANES), F32),
                        pltpu.VMEM((blk_rows, RW_LANES), F32),
                        pltpu.VMEM((2 * n_sb, RW_LANES, RW_LANES), F32)],
        compiler_params=_cparams("arbitrary", "arbitrary"),
        name=f"rwkv_scan_t{t_len}",
    )(*args)
    return y, sfin


def _out_proj_kernel(grp_ref, y_ref, w_ref, x_ref, mod_ref, o_ref, *, gate_row):
    o_ref[...] = x_ref[...] + mod_ref[0, gate_row:gate_row + 1, :] * _dot(y_ref[...], w_ref[...])


def _out_proj(y, w, x, mod, grp, gate_row, tm):
    m_rows, k_dim = y.shape
    d = w.shape[1]
    tn = 1024
    return pl.pallas_call(
        functools.partial(_out_proj_kernel, gate_row=gate_row),
        grid_spec=pltpu.PrefetchScalarGridSpec(
            num_scalar_prefetch=1, grid=(d // tn, m_rows // tm),
            in_specs=[pl.BlockSpec((tm, k_dim), lambda n, i, g: (i, 0)),
                      pl.BlockSpec((k_dim, tn), lambda n, i, g: (0, n)),
                      pl.BlockSpec((tm, tn), lambda n, i, g: (i, n)),
                      pl.BlockSpec((1, 6, tn), lambda n, i, g: (g[i], 0, n))],
            out_specs=pl.BlockSpec((tm, tn), lambda n, i, g: (i, n))),
        out_shape=jax.ShapeDtypeStruct((m_rows, d), F32),
        compiler_params=_cparams("arbitrary", "arbitrary"),
        name="out_proj",
    )(grp, y, w, x, mod)


def _in_proj_kernel(grp_ref, x_ref, mod_ref, nw_ref, w_ref, o_ref, h_buf):
    @pl.when(pl.program_id(1) == 0)
    def _():
        h_buf[...] = _rms_mod(x_ref[...], nw_ref[...], mod_ref[0, 0:1, :], mod_ref[0, 1:2, :]).astype(BF16)

    o_ref[...] = jnp.dot(h_buf[...], w_ref[...], preferred_element_type=F32)


def _in_proj(x, mod, grp, nw, w, tm, tn):
    m_rows, d = x.shape
    n = w.shape[1]
    return pl.pallas_call(
        _in_proj_kernel,
        grid_spec=pltpu.PrefetchScalarGridSpec(
            num_scalar_prefetch=1, grid=(m_rows // tm, n // tn),
            in_specs=[pl.BlockSpec((tm, d), lambda i, n_, g: (i, 0)),
                      pl.BlockSpec((1, 6, d), lambda i, n_, g: (g[i], 0, 0)),
                      pl.BlockSpec((1, d), lambda i, n_, g: (0, 0)),
                      pl.BlockSpec((d, tn), lambda i, n_, g: (0, n_))],
            out_specs=pl.BlockSpec((tm, tn), lambda i, n_, g: (i, n_)),
            scratch_shapes=[pltpu.VMEM((tm, d), BF16)]),
        out_shape=jax.ShapeDtypeStruct((m_rows, n), F32),
        compiler_params=_cparams("arbitrary", "arbitrary"),
        name="gdn_in_proj",
    )(grp, x, mod, nw, w)


def _ffn_kernel(grp_ref, x_ref, mod_ref, nw_ref, w1_ref, w3_ref, w2_ref, fw_ref, o_ref, h_buf, acc,
                *, final_norm):
    f = pl.program_id(1)

    @pl.when(f == 0)
    def _():
        h_buf[...] = _rms_mod(x_ref[...], nw_ref[...], mod_ref[0, 3:4, :], mod_ref[0, 4:5, :]).astype(BF16)
        acc[...] = jnp.zeros_like(acc)

    h = h_buf[...]
    gate = jnp.dot(h, w1_ref[...], preferred_element_type=F32)
    up = jnp.dot(h, w3_ref[...], preferred_element_type=F32)
    acc[...] += _dot(_silu(gate) * up, w2_ref[...])

    @pl.when(f == pl.num_programs(1) - 1)
    def _():
        y = x_ref[...] + mod_ref[0, 5:6, :] * acc[...]
        if final_norm:
            y = y * lax.rsqrt(jnp.mean(y * y, -1, keepdims=True) + EPS) * fw_ref[...]
        o_ref[...] = y


def _ffn(x, mod, grp, nw, w1, w3, w2, fw, final_norm, tm):
    m_rows, d = x.shape
    d_ff = w1.shape[1]
    tf = 512
    return pl.pallas_call(
        functools.partial(_ffn_kernel, final_norm=final_norm),
        grid_spec=pltpu.PrefetchScalarGridSpec(
            num_scalar_prefetch=1, grid=(m_rows // tm, d_ff // tf),
            in_specs=[pl.BlockSpec((tm, d), lambda i, f, g: (i, 0)),
                      pl.BlockSpec((1, 6, d), lambda i, f, g: (g[i], 0, 0)),
                      pl.BlockSpec((1, d), lambda i, f, g: (0, 0)),
                      pl.BlockSpec((d, tf), lambda i, f, g: (0, f)),
                      pl.BlockSpec((d, tf), lambda i, f, g: (0, f)),
                      pl.BlockSpec((tf, d), lambda i, f, g: (f, 0)),
                      pl.BlockSpec((1, d), lambda i, f, g: (0, 0))],
            out_specs=pl.BlockSpec((tm, d), lambda i, f, g: (i, 0)),
            scratch_shapes=[pltpu.VMEM((tm, d), BF16), pltpu.VMEM((tm, d), F32)]),
        out_shape=jax.ShapeDtypeStruct((m_rows, d), F32),
        compiler_params=_cparams("arbitrary", "arbitrary"),
        name="ffn",
    )(grp, x, mod, nw, w1, w3, w2, fw)


def _gdn_kernel(*refs, t_len, n_sb, has_s0):
    (q_ref, k_ref, v_ref, z_ref, ab_ref, cq_ref, ck_ref, cv_ref, alog_ref, dtb_ref, nw_ref) = refs[:11]
    rest = refs[11:]
    if has_s0:
        s0_ref, rest = rest[0], rest[1:]
    o_ref, sfin_ref, wq_buf, u_buf, attn_buf, ket_buf, gl_buf, obuf, s_ref, ext_buf = rest
    c_len = CHUNK
    n_chunks = t_len // c_len
    n_tot = n_sb * n_chunks
    last_row0 = n_sb * t_len - SUBLANES
    kh = pl.program_id(1)
    n_vh = 32
    tri_t = _iota((c_len, c_len), 0)
    tri_s = _iota((c_len, c_len), 1)
    lane_ab = _iota((c_len, 128), 1)
    row_abt = _iota((128, c_len), 0)
    lane_1 = _iota((1, 128), 1)
    lower = (tri_s <= tri_t).astype(F32)
    inv_masks = _inverse_masks(c_len)

    for s in range(n_sb):
        for j in range(4):
            if has_s0:
                s_ref[s * 4 + j] = s0_ref[s, j // 2, j % 2]
            else:
                s_ref[s * 4 + j] = jnp.zeros((GD_DK, GD_DV), F32)

    def conv_silu(ref, w_ref, cg):
        c = cg % n_chunks
        r0 = pl.multiple_of(cg * c_len, c_len)
        main = ref[pl.ds(r0, c_len), :]
        up0 = pl.multiple_of(jnp.maximum(r0 - SUBLANES, 0), SUBLANES)
        dn0 = pl.multiple_of(jnp.minimum(r0 + c_len, last_row0), SUBLANES)
        up = ref[pl.ds(up0, SUBLANES), :] * jnp.where(c > 0, 1.0, 0.0)
        dn = ref[pl.ds(dn0, SUBLANES), :] * jnp.where(c < n_chunks - 1, 1.0, 0.0)
        wd = main.shape[1]
        ext_buf[0:SUBLANES, 0:wd] = up
        ext_buf[SUBLANES:SUBLANES + c_len, 0:wd] = main
        ext_buf[SUBLANES + c_len:2 * SUBLANES + c_len, 0:wd] = dn
        w = w_ref[...]
        acc = main * w[1:2]
        for j in (0, 2, 3):
            acc = acc + ext_buf[SUBLANES - 1 + j:SUBLANES - 1 + j + c_len, 0:wd] * w[j:j + 1]
        return _silu(acc)

    def l2n(z):
        return z * lax.rsqrt(jnp.sum(z * z, -1, keepdims=True) + EPS)

    incl_m = [tri_s <= tri_t, tri_s >= tri_t]
    strict_m = [tri_s < tri_t, tri_s > tri_t]
    n_cu = 8

    def phase_a(cg0):
        cgs = [cg0 + i for i in range(n_cu)]
        rows = [pl.ds(pl.multiple_of(cg * c_len, c_len), c_len) for cg in cgs]
        q = [l2n(conv_silu(q_ref, cq_ref, cg)) * (GD_DK ** -0.5) for cg in cgs]
        k = [l2n(conv_silu(k_ref, ck_ref, cg)) for cg in cgs]
        v2 = [conv_silu(v_ref, cv_ref, cg) for cg in cgs]
        ab = [ab_ref[rw, :] for rw in rows]
        k_t = [x.T for x in k]
        g_all = [-jnp.exp(alog_ref[...]) * _softplus(x + dtb_ref[...]) for x in ab]
        beta_all = [jax.nn.sigmoid(x) for x in ab]
        prefix = [_mm(lower, g, "nn", P_CUM) for g in g_all]
        g_kk = [_mm(x, y, "nn", P_GRAM) for x, y in zip(k, k_t)]
        g_qk = [_mm(x, y, "nn", P_GRAM) for x, y in zip(q, k_t)]
        gtot_all = [jnp.sum(g, axis=0, keepdims=True) for g in g_all]
        gc_all = [jnp.where(lane_ab < 64, p, t - p + g) for p, t, g in zip(prefix, gtot_all, g_all)]
        gct_all = [x.T for x in gc_all]
        per = []
        for i in range(n_cu):
            for d in range(2):
                for vl in range(2):
                    col_g = d * 64 + 2 * kh + vl
                    col_b = col_g + n_vh
                    gc_col = jnp.sum(jnp.where(lane_ab == col_g, gc_all[i], 0.0), axis=1, keepdims=True)
                    beta = jnp.sum(jnp.where(lane_ab == col_b, beta_all[i], 0.0), axis=1, keepdims=True)
                    gc_row = jnp.sum(jnp.where(row_abt == col_g, gct_all[i], 0.0), axis=0, keepdims=True)
                    g_last = jnp.sum(jnp.where(lane_1 == col_g, gtot_all[i], 0.0), axis=1, keepdims=True)
                    decay = jnp.where(incl_m[d], jnp.exp(jnp.minimum(gc_col - gc_row, 0.0)), 0.0)
                    a_mat = jnp.where(strict_m[d], beta * g_kk[i] * decay, 0.0)
                    per.append((i, 2 * d + vl, gc_col, beta, gc_row, g_last, decay, a_mat))
        sel = [[n for n, p in enumerate(per) if p[1] // 2 == d] for d in range(2)]
        t_d = _unit_tri_inverse(*[jnp.stack([-per[n][7] for n in sl]) for sl in sel], inv_masks)
        t_all = [None] * len(per)
        for d in range(2):
            for m, n in enumerate(sel[d]):
                t_all[n] = t_d[d][m]
        rhs = [jnp.concatenate([v2[i][:, (j % 2) * GD_DV:(j % 2 + 1) * GD_DV] * beta,
                                k[i] * (beta * jnp.exp(gc_col))], axis=1)
               for i, j, gc_col, beta, _, _, _, _ in per]
        uw = [_mm(t_all[n], x, "nn", P_APPLY) for n, x in enumerate(rhs)]
        for n, (i, j, gc_col, beta, gc_row, g_last, decay, _) in enumerate(per):
            idx = cgs[i] * 4 + j
            u_buf[idx] = uw[n][:, :GD_DV]
            wq_buf[idx, 0:c_len, :] = uw[n][:, GD_DV:].astype(BF16)
            wq_buf[idx, c_len:, :] = (q[i] * jnp.exp(gc_col)).astype(BF16)
            attn_buf[idx] = (g_qk[i] * decay).astype(BF16)
            ket_buf[idx] = (k_t[i] * jnp.exp(g_last - gc_row)).astype(BF16)
            gl_buf[idx] = jnp.broadcast_to(jnp.exp(g_last), (SUBLANES, GD_DV))

    def a_body(i, carry):
        phase_a(n_cu * i)
        return carry

    lax.fori_loop(0, n_tot // n_cu, a_body, 0)

    def b_body(i, carry):
        chains = [(s, d, vl) for s in range(n_sb) for d in range(2) for vl in range(2)]
        cg = [s * n_chunks + (i if d == 0 else n_chunks - 1 - i) for s, d, _ in chains]
        idx = [c * 4 + 2 * d + vl for c, (_, d, vl) in zip(cg, chains)]
        s_mat = [s_ref[s * 4 + 2 * d + vl] for s, d, vl in chains]
        ws_qs = [_mm(wq_buf[n], x, "nn", P_APPLY) for n, x in zip(idx, s_mat)]
        v_new = [(u_buf[n] - x[0:c_len]).astype(BF16) for n, x in zip(idx, ws_qs)]
        o = [x[c_len:] + _mm(attn_buf[n], y, "nn", P_APPLY) for n, x, y in zip(idx, ws_qs, v_new)]
        s_new = [x * gl_buf[n, 0:1, :] + _mm(ket_buf[n], y, "nn", P_APPLY) for n, x, y in zip(idx, s_mat, v_new)]
        for n, (s, d, vl) in enumerate(chains):
            rows = pl.ds(pl.multiple_of(cg[n] * c_len, c_len), c_len)
            obuf[d, rows, pl.ds(vl * GD_DV, GD_DV)] = o[n]
            s_ref[s * 4 + 2 * d + vl] = s_new[n]
        return carry

    lax.fori_loop(0, n_chunks, b_body, 0)

    n_cc = 4

    def c_body(i, carry):
        tiles = [(pl.ds(pl.multiple_of((i * n_cc + q) * c_len, c_len), c_len), pl.ds(vl * GD_DV, GD_DV))
                 for q in range(n_cc) for vl in range(2)]
        o = [obuf[0, rw, cl] + obuf[1, rw, cl] for rw, cl in tiles]
        ms = [jnp.mean(x * x, -1, keepdims=True) for x in o]
        gate = [_silu(z_ref[rw, cl]) for rw, cl in tiles]
        for (rw, cl), x, m, g in zip(tiles, o, ms, gate):
            o_ref[rw, cl] = (x * lax.rsqrt(m + EPS) * nw_ref[...] * g).astype(o_ref.dtype)
        return carry

    lax.fori_loop(0, n_tot // n_cc, c_body, 0)
    for s in range(n_sb):
        for j in range(4):
            sfin_ref[s, j // 2, j % 2] = s_ref[s * 4 + j]


def _gdn_scan(proj, ab, conv_w, alog_row, dtb_row, nw, s0, row0, n_seq, t_len, n_sb):
    n_kh = 16
    blk_rows = n_sb * t_len
    assert row0 % blk_rows == 0 and n_seq % n_sb == 0 and (n_sb * t_len // CHUNK) % 2 == 0
    blk0 = row0 // blk_rows
    vw = 2 * GD_DV
    n_units = n_sb * (t_len // CHUNK) * 4
    in_specs = [pl.BlockSpec((blk_rows, GD_DK), lambda b, h: (blk0 + b, h)),
                pl.BlockSpec((blk_rows, GD_DK), lambda b, h: (blk0 + b, n_kh + h)),
                pl.BlockSpec((blk_rows, vw), lambda b, h: (blk0 + b, n_kh + h)),
                pl.BlockSpec((blk_rows, vw), lambda b, h: (blk0 + b, 2 * n_kh + h)),
                pl.BlockSpec((blk_rows, 128), lambda b, h: (blk0 + b, 0)),
                pl.BlockSpec((GD_CONV, GD_DK), lambda b, h: (0, h)),
                pl.BlockSpec((GD_CONV, GD_DK), lambda b, h: (0, n_kh + h)),
                pl.BlockSpec((GD_CONV, vw), lambda b, h: (0, n_kh + h)),
                pl.BlockSpec((1, 128), lambda b, h: (0, 0)),
                pl.BlockSpec((1, 128), lambda b, h: (0, 0)),
                pl.BlockSpec((1, GD_DV), lambda b, h: (0, 0))]
    args = [proj, proj, proj, proj, ab, conv_w, conv_w, conv_w, alog_row, dtb_row, nw]
    if s0 is not None:
        in_specs.append(pl.BlockSpec((n_sb, 2, 2, GD_DK, GD_DV), lambda b, h: (b, 0, h, 0, 0)))
        args.append(s0)
    o, sfin = pl.pallas_call(
        functools.partial(_gdn_kernel, t_len=t_len, n_sb=n_sb, has_s0=s0 is not None),
        grid=(n_seq // n_sb, n_kh),
        in_specs=in_specs,
        out_specs=[pl.BlockSpec((blk_rows, vw), lambda b, h: (b, h)),
                   pl.BlockSpec((n_sb, 2, 2, GD_DK, GD_DV), lambda b, h: (b, 0, h, 0, 0))],
        out_shape=[jax.ShapeDtypeStruct((n_seq * t_len, 2 * n_kh * GD_DV), BF16),
                   jax.ShapeDtypeStruct((n_seq, 2, 2 * n_kh, GD_DK, GD_DV), F32)],
        scratch_shapes=[pltpu.VMEM((n_units, 2 * CHUNK, GD_DV), BF16),
                        pltpu.VMEM((n_units, CHUNK, GD_DV), F32),
                        pltpu.VMEM((n_units, CHUNK, CHUNK), BF16),
                        pltpu.VMEM((n_units, GD_DK, CHUNK), BF16),
                        pltpu.VMEM((n_units, SUBLANES, GD_DV), F32),
                        pltpu.VMEM((2, blk_rows, vw), F32),
                        pltpu.VMEM((4 * n_sb, GD_DK, GD_DV), F32),
                        pltpu.VMEM((CHUNK + 2 * SUBLANES, vw), F32)],
        compiler_params=_cparams("arbitrary", "arbitrary"),
        name=f"gdn_scan_t{t_len}",
    )(*args)
    return o, sfin


def _grid_pos_embed(n_tokens, d_model):
    rows = n_tokens // GRID_W
    f32 = np.float32
    row = np.broadcast_to(np.arange(rows, dtype=f32)[:, None], (rows, GRID_W)).reshape(-1)
    col = np.broadcast_to(np.arange(GRID_W, dtype=f32)[None, :], (rows, GRID_W)).reshape(-1)
    quarter = d_model // 4
    omega = (f32(1.0) / (f32(POS_BASE) ** (np.arange(quarter, dtype=f32) / f32(quarter)))).astype(f32)
    ar = (row[:, None] * omega).astype(f32)
    ac = (col[:, None] * omega).astype(f32)
    return jnp.asarray(np.concatenate([np.sin(ar), np.cos(ar), np.sin(ac), np.cos(ac)], -1).astype(f32))


def _pad_axis(a, axis, size):
    pad = [(0, 0)] * a.ndim
    pad[axis] = (0, size - a.shape[axis])
    return jnp.pad(a, pad)


def kernel(x_prompt, x_sample, state_rwkv, state_gdn, c, c_ctx, norm_mix, norm_ffn, norm_final, w_mod, b_mod, ffn_w1, ffn_w3, ffn_w2, rw_mu, rw_wr, rw_wk, rw_wv, rw_wo, rw_w0, rw_w1, rw_w2, rw_a0, rw_a1, rw_a2, rw_g1, rw_g2, rw_kk, rw_ka, rw_rk, rw_ln_w, rw_ln_b, gd_w_in, gd_conv, gd_a_log, gd_dt_bias, gd_norm, gd_w_out):
    n_p, t_p, d = x_prompt.shape
    n_s, t_s, _ = x_sample.shape
    bf = lambda a: a.astype(BF16)
    row = lambda a: a.reshape(1, -1)
    tm_shift = 256
    tm_big = 512
    tm_wide = 1024

    regions = [
        dict(x=x_prompt.reshape(n_p * t_p, d), n_seq=n_p, t_len=t_p, groups=[0] * n_p, n_sb=4, n_sb_gd=8,
             s0_rw=None, s0_gd=None),
        dict(x=(x_sample + _grid_pos_embed(t_s, d)[None]).reshape(n_s * t_s, d), n_seq=n_s, t_len=t_s,
             groups=list(range(1, n_s + 1)), n_sb=1, n_sb_gd=1,
             s0_rw=state_rwkv[:, 0].reshape(n_s, 2, d, RW_HEAD), s0_gd=state_gdn[:, 0]),
    ]
    for reg in regions:
        layout = [(reg["n_seq"], reg["t_len"], reg["groups"])]
        reg["tables"] = _tile_tables(layout, tm_shift)
        reg["grp"] = _tile_tables(layout, tm_big)[0]
        reg["grp_wide"] = _tile_tables(layout, tm_wide)[0]

    cond = _pad_axis(jnp.concatenate([c_ctx[None], c], axis=0), 0, SUBLANES)
    mod = _modulation(cond, w_mod, b_mod).reshape(w_mod.shape[0], SUBLANES, 6, d)

    w_rkv = bf(jnp.stack([rw_wr[0], rw_wk[0], rw_wv[0]]))
    lp = RW_LORA_PAD
    lora_w = (bf(rw_g1[0]), bf(rw_g2[0]),
              bf(_pad_axis(rw_w1[0], 2, lp)), bf(_pad_axis(rw_w2[0], 1, lp)),
              bf(_pad_axis(rw_a1[0], 2, lp)), bf(_pad_axis(rw_a2[0], 1, lp)),
              rw_w0[0][:, None, :], rw_a0[0][:, None, :])
    scan_p = (row(rw_kk[0]), row(rw_ka[0]), row(rw_rk[0]), row(rw_ln_w[0]), row(rw_ln_b[0]))
    w_o, ffn0 = _layer_bf16(rw_wo, 0), (_layer_bf16(ffn_w1, 0), _layer_bf16(ffn_w3, 0), _layer_bf16(ffn_w2, 0))
    for reg in regions:
        rkv, gate, lw, a = _rwkv_inputs(reg["x"], mod[0], reg["tables"], row(norm_mix[0]), rw_mu[0], w_rkv,
                                        *lora_w, tm_shift)
        y, reg["s_rwkv"] = _rwkv_scan(rkv, gate, lw, a, *scan_p, reg["s0_rw"], 0, reg["n_seq"], reg["t_len"],
                                      reg["n_sb"])
        x = _out_proj(y, w_o, reg["x"], mod[0], reg["grp"], 2, tm_big)
        reg["x"] = _ffn(x, mod[0], reg["grp"], row(norm_ffn[0]), *ffn0, row(norm_final), False, tm_big)

    n_main = 3 * 4096
    w_in_main = _layer_bf16(gd_w_in, 0, 0, n_main)
    w_in_ab = _layer_bf16(gd_w_in, 0, n_main, gd_w_in.shape[2] - n_main)
    zeros32 = jnp.zeros((2, 32), F32)
    alog_row = jnp.stack([gd_a_log[0], zeros32], axis=1).reshape(1, 128)
    dtb_row = jnp.stack([gd_dt_bias[0], zeros32], axis=1).reshape(1, 128)
    w_out, ffn1 = _layer_bf16(gd_w_out, 0), (_layer_bf16(ffn_w1, 1), _layer_bf16(ffn_w3, 1), _layer_bf16(ffn_w2, 1))
    for reg in regions:
        proj = _in_proj(reg["x"], mod[1], reg["grp_wide"], row(norm_mix[1]), w_in_main, tm_wide, 1024)
        ab = _in_proj(reg["x"], mod[1], reg["grp_wide"], row(norm_mix[1]), w_in_ab, tm_wide, 128)
        o, reg["s_gdn"] = _gdn_scan(proj, ab, gd_conv[0], alog_row, dtb_row, row(gd_norm[0]), reg["s0_gd"], 0,
                                    reg["n_seq"], reg["t_len"], reg["n_sb_gd"])
        x = _out_proj(o, w_out, reg["x"], mod[1], reg["grp"], 2, tm_big)
        reg["x"] = _ffn(x, mod[1], reg["grp"], row(norm_ffn[1]), *ffn1, row(norm_final), True, tm_big)

    y_prompt = regions[0]["x"].reshape(n_p, t_p, d)
    y_sample = regions[1]["x"].reshape(n_s, t_s, d)
    new_state_rwkv = regions[0]["s_rwkv"].reshape(n_p, 1, 2, d // RW_HEAD, RW_HEAD, RW_HEAD)
    new_state_gdn = regions[0]["s_gdn"].reshape(n_p, 1, 2, 32, GD_DK, GD_DV)
    return (y_prompt, y_sample, new_state_rwkv, new_state_gdn)
```
